```python
import math
import jax
import jax.numpy as jnp
from jax import lax
import numpy as np

D_MODEL = 4096
BATCH = 2
SEQ = 8192
DEPTH = 4

N_MIXERS = 3

N_MEM = 256
MEM_HEADS = 4
MEM_HEAD_DIM = 128
MEM_Q = MEM_HEADS * MEM_HEAD_DIM

GDN_K_HEADS = 16
GDN_V_HEADS = 32
GDN_HEAD_DIM = 128
GDN_KEY_DIM = GDN_K_HEADS * GDN_HEAD_DIM
GDN_VAL_DIM = GDN_V_HEADS * GDN_HEAD_DIM
GDN_CONV_DIM = 2 * GDN_KEY_DIM + GDN_VAL_DIM
GDN_CONV_TAPS = 4
GDN_CHUNK = 64
GDN_IN = GDN_CONV_DIM + GDN_VAL_DIM + 2 * GDN_V_HEADS
GDN_OUT = GDN_VAL_DIM

DIL_GROUPS = ((128, 1), (512, 4), (2048, 16))
DIL_HEADS = 16
DIL_HEAD_DIM = 128
DIL_IN = len(DIL_GROUPS) * 3 * DIL_HEADS * DIL_HEAD_DIM
DIL_OUT = DIL_HEADS * DIL_HEAD_DIM

DSA_HEADS = 32
DSA_NOPE_DIM = 128
DSA_ROPE_DIM = 64
DSA_V_DIM = 128
DSA_Q_LORA = 1024
DSA_KV_LORA = 512
IDX_HEADS = 32
IDX_DIM = 128
IDX_ROPE_DIM = 64
DSA_TOPK_MAX = 256
DSA_Q_BLOCK = 128
ROPE_THETA = 10000.0
DSA_IN = DSA_Q_LORA + DSA_KV_LORA + DSA_ROPE_DIM + IDX_DIM + IDX_HEADS
DSA_OUT = DSA_HEADS * DSA_V_DIM

N_EXPERTS = 32
TOP_K = 4
MOE_FF = 384
SWIGLU_ALPHA = 1.702
SWIGLU_LIMIT = 7.0
MOE_BLOCK = 128

DEEPNORM_ALPHA = (2 * DEPTH) ** 0.25
DEEPNORM_BETA = (8 * DEPTH) ** -0.25
LN_EPS = 1e-5
RMS_EPS = 1e-6

kernel_name = 'hybrid_gdn_dilated_dsa_moe_trunk'


def layer_norm(x, g, b):
    xf = x.astype(jnp.float32)
    mu = jnp.mean(xf, axis=-1, keepdims=True)
    var = jnp.mean(jnp.square(xf - mu), axis=-1, keepdims=True)
    return ((xf - mu) * lax.rsqrt(var + LN_EPS) * g + b).astype(x.dtype)


def rms_norm(x, g):
    xf = x.astype(jnp.float32)
    return (xf * lax.rsqrt(jnp.mean(jnp.square(xf), axis=-1, keepdims=True) + RMS_EPS) * g).astype(x.dtype)


def l2_normalize(x):
    xf = x.astype(jnp.float32)
    return xf * lax.rsqrt(jnp.sum(jnp.square(xf), axis=-1, keepdims=True) + RMS_EPS)


def rope_tables(T, dim):
    inv_freq = ROPE_THETA ** (-jnp.arange(0, dim, 2, dtype=jnp.float32) / dim)
    ang = jnp.arange(T, dtype=jnp.float32)[:, None] * inv_freq[None, :]
    return jnp.cos(ang), jnp.sin(ang)


def apply_rope(x, cos, sin):
    x1, x2 = jnp.split(x.astype(jnp.float32), 2, axis=-1)
    return jnp.concatenate([x1 * cos - x2 * sin, x2 * cos + x1 * sin], axis=-1).astype(x.dtype)


def causal_depthwise_conv(x, w):
    taps, c = w.shape
    return lax.conv_general_dilated(x, w[:, None, :].astype(x.dtype), window_strides=(1,),
                                    padding=((taps - 1, 0),), dimension_numbers=('NWC', 'WIO', 'NWC'),
                                    feature_group_count=c)


def gated_delta_rule_chunked(q, k, v, g, beta):
    B, T, H, DK = q.shape
    DV = v.shape[-1]
    C = GDN_CHUNK
    N = T // C
    f32 = jnp.float32

    def to_chunks(a):
        a = a.astype(f32).reshape((B, N, C, H) + a.shape[3:])
        return jnp.moveaxis(a, 3, 1)

    q, k, v, g, beta = (to_chunks(a) for a in (q, k, v, g, beta))
    gc = jnp.cumsum(g, axis=-1)
    causal = jnp.tril(jnp.ones((C, C), dtype=bool))
    strict = jnp.tril(jnp.ones((C, C), dtype=bool), -1)
    diff = gc[..., :, None] - gc[..., None, :]
    decay = jnp.where(causal, jnp.exp(jnp.where(causal, diff, 0.0)), 0.0)
    kb = k * beta[..., None]
    lower = jnp.where(strict, jnp.einsum('bhnid,bhnjd->bhnij', kb, k) * decay, 0.0)
    eye = jnp.broadcast_to(jnp.eye(C, dtype=f32), lower.shape)
    tinv = lax.linalg.triangular_solve(eye + lower, eye, left_side=True, lower=True, unit_diagonal=True)
    u = tinv @ (v * beta[..., None])
    w = tinv @ (kb * jnp.exp(gc)[..., None])
    a_intra = jnp.where(causal, jnp.einsum('bhnid,bhnjd->bhnij', q, k) * decay, 0.0)
    q_dec = q * jnp.exp(gc)[..., None]
    k_dec = k * jnp.exp(gc[..., -1:] - gc)[..., None]
    g_last = jnp.exp(gc[..., -1])

    def step(state, xs):
        qd, kd, u_i, w_i, a_i, gl = xs
        v_new = u_i - w_i @ state
        o = qd @ state + a_i @ v_new
        state = state * gl[..., None, None] + jnp.einsum('bhcd,bhce->bhde', kd, v_new)
        return state, o

    xs = tuple(jnp.moveaxis(a, 2, 0) for a in (q_dec, k_dec, u, w, a_intra, g_last))
    _, o = lax.scan(step, jnp.zeros((B, H, DK, DV), f32), xs)
    return jnp.transpose(o, (1, 0, 3, 2, 4)).reshape(B, T, H, DV)


def gdn_mixer(h, conv_w, a_log, dt_bias, norm_g):
    B, T, _ = h.shape
    f32 = jnp.float32
    qkv, z, b, a = jnp.split(h, [GDN_CONV_DIM, GDN_CONV_DIM + GDN_VAL_DIM,
                                 GDN_CONV_DIM + GDN_VAL_DIM + GDN_V_HEADS], axis=-1)
    qkv = jax.nn.silu(causal_depthwise_conv(qkv, conv_w))
    q, k, v = jnp.split(qkv, [GDN_KEY_DIM, 2 * GDN_KEY_DIM], axis=-1)
    rep = GDN_V_HEADS // GDN_K_HEADS
    q = jnp.repeat(q.reshape(B, T, GDN_K_HEADS, GDN_HEAD_DIM), rep, axis=2)
    k = jnp.repeat(k.reshape(B, T, GDN_K_HEADS, GDN_HEAD_DIM), rep, axis=2)
    v = v.reshape(B, T, GDN_V_HEADS, GDN_HEAD_DIM)
    q = l2_normalize(q) * GDN_HEAD_DIM ** -0.5
    k = l2_normalize(k)
    beta = jax.nn.sigmoid(b.astype(f32))
    g = -jnp.exp(a_log.astype(f32)) * jax.nn.softplus(a.astype(f32) + dt_bias.astype(f32))
    o = gated_delta_rule_chunked(q, k, v, g, beta)
    o = rms_norm(o, norm_g) * jax.nn.silu(z.reshape(B, T, GDN_V_HEADS, GDN_HEAD_DIM).astype(f32))
    return o.reshape(B, T, GDN_VAL_DIM).astype(h.dtype)


def dilated_window_attention(q, k, v, window, dilation):
    B, T, H, E = q.shape
    f32 = jnp.float32
    d = dilation
    w = window // dilation
    ls = T // d
    nb = -(-ls // w)
    pad = nb * w - ls

    def to_sub(a):
        a = jnp.swapaxes(a.reshape(B, ls, d, H, E), 1, 2)
        return jnp.pad(a, ((0, 0), (0, 0), (0, pad), (0, 0), (0, 0)))

    def band(a):
        a = jnp.pad(a, ((0, 0), (0, 0), (w, 0), (0, 0), (0, 0))).reshape(B, d, nb + 1, w, H, E)
        return jnp.concatenate([a[:, :, :-1], a[:, :, 1:]], axis=3)

    qs = to_sub(q).reshape(B, d, nb, w, H, E)
    kb = band(to_sub(k))
    vb = band(to_sub(v))
    s = jnp.einsum('bdnqhe,bdnkhe->bdnhqk', qs, kb).astype(f32) * E ** -0.5
    qi = jnp.arange(w)[:, None]
    kj = jnp.arange(2 * w)[None, :]
    dist = w + qi - kj
    key_sub = jnp.arange(nb)[:, None, None] * w + kj[None] - w
    valid = (dist >= 0) & (dist <= w) & (key_sub >= 0)
    s = jnp.where(valid[:, None], s, -jnp.inf)
    m = jnp.max(s, axis=-1, keepdims=True)
    pexp = jnp.exp(s - m)
    l = jnp.sum(pexp, axis=-1, keepdims=True)
    o = jnp.einsum('bdnhqk,bdnkhe->bdnqhe', pexp, vb.astype(f32)) / jnp.swapaxes(l, 3, 4)
    lse = jnp.swapaxes((m + jnp.log(l))[..., 0], 3, 4)

    def from_sub(a):
        a = a.reshape((B, d, nb * w) + a.shape[4:])[:, :, :ls]
        return jnp.swapaxes(a, 1, 2).reshape((B, T) + a.shape[3:])

    return from_sub(o), from_sub(lse)


def dilated_mixer(h):
    B, T, _ = h.shape
    qkv = h.reshape(B, T, len(DIL_GROUPS), 3, DIL_HEADS, DIL_HEAD_DIM)
    outs, lses = [], []
    for gi, (window, dilation) in enumerate(DIL_GROUPS):
        o, lse = dilated_window_attention(qkv[:, :, gi, 0], qkv[:, :, gi, 1], qkv[:, :, gi, 2], window, dilation)
        outs.append(o)
        lses.append(lse)
    wts = jax.nn.softmax(jnp.stack(lses, axis=0), axis=0)
    o = jnp.sum(wts[..., None] * jnp.stack(outs, axis=0), axis=0)
    return o.reshape(B, T, DIL_OUT).astype(h.dtype)


def dsa_mixer(h, q_norm_g, kv_norm_g, w_q_up, w_idx_q, idx_k_norm_g, idx_k_norm_b, w_kv_up):
    B, T, _ = h.shape
    f32 = jnp.float32
    o1 = DSA_Q_LORA
    o2 = o1 + DSA_KV_LORA
    o3 = o2 + DSA_ROPE_DIM
    o4 = o3 + IDX_DIM
    q_lat, kv_lat, k_rope, idx_k, idx_w = jnp.split(h, [o1, o2, o3, o4], axis=-1)
    cos, sin = rope_tables(T, DSA_ROPE_DIM)
    cq = rms_norm(q_lat, q_norm_g)
    q = (cq @ w_q_up).reshape(B, T, DSA_HEADS, DSA_NOPE_DIM + DSA_ROPE_DIM)
    q_nope = q[..., :DSA_NOPE_DIM]
    q_rope = apply_rope(q[..., DSA_NOPE_DIM:], cos[:, None], sin[:, None])
    c_kv = rms_norm(kv_lat, kv_norm_g)
    k_rope = apply_rope(k_rope, cos, sin)
    w_kv = w_kv_up.reshape(DSA_KV_LORA, DSA_HEADS, DSA_NOPE_DIM + DSA_V_DIM)
    w_uk = w_kv[..., :DSA_NOPE_DIM]
    w_uv = w_kv[..., DSA_NOPE_DIM:]
    iq = (cq @ w_idx_q).reshape(B, T, IDX_HEADS, IDX_DIM)
    iq = jnp.concatenate([apply_rope(iq[..., :IDX_ROPE_DIM], cos[:, None], sin[:, None]),
                          iq[..., IDX_ROPE_DIM:]], axis=-1).astype(f32)
    ik = layer_norm(idx_k, idx_k_norm_g, idx_k_norm_b)
    ik = jnp.concatenate([apply_rope(ik[..., :IDX_ROPE_DIM], cos, sin), ik[..., IDX_ROPE_DIM:]],
                         axis=-1).astype(f32)
    iw = idx_w.astype(f32) * (IDX_HEADS ** -0.5 * IDX_DIM ** -0.5)
    topk = min(DSA_TOPK_MAX, T // 4)
    scale = (DSA_NOPE_DIM + DSA_ROPE_DIM) ** -0.5
    key_pos = jnp.arange(T)
    gather = jax.vmap(lambda table, idx: table[idx])

    def attend_block(args):
        qn, qr, iqb, iwb, qpos = args
        isc = jax.nn.relu(jnp.einsum('bqhd,bsd->bqhs', iqb, ik))
        isc = jnp.einsum('bqhs,bqh->bqs', isc, iwb)
        isc = jnp.where(key_pos[None, None, :] <= qpos[None, :, None], isc, -jnp.inf)
        _, sel = lax.top_k(isc, topk)
        c_sel = gather(c_kv, sel)
        r_sel = gather(k_rope, sel)
        q_abs = jnp.einsum('bqhd,chd->bqhc', qn, w_uk)
        s = (jnp.einsum('bqhc,bqkc->bqhk', q_abs, c_sel)
             + jnp.einsum('bqhr,bqkr->bqhk', qr, r_sel)).astype(f32) * scale
        s = jnp.where((sel <= qpos[None, :, None])[:, :, None, :], s, -jnp.inf)
        p = jax.nn.softmax(s, axis=-1).astype(c_sel.dtype)
        o_lat = jnp.einsum('bqhk,bqkc->bqhc', p, c_sel)
        return jnp.einsum('bqhc,chd->bqhd', o_lat, w_uv)

    nqb = T // DSA_Q_BLOCK

    def blocks(a):
        return jnp.moveaxis(a.reshape((B, nqb, DSA_Q_BLOCK) + a.shape[2:]), 1, 0)

    qpos = jnp.arange(T).reshape(nqb, DSA_Q_BLOCK)
    out = lax.map(attend_block, (blocks(q_nope), blocks(q_rope), blocks(iq), blocks(iw), qpos))
    return jnp.moveaxis(out, 0, 1).reshape(B, T, DSA_OUT).astype(h.dtype)


def memory_attention(q_mem, mem_k, mem_v):
    B, T = q_mem.shape[:2]
    s = jnp.einsum('bthe,bmhe->bhtm', q_mem, mem_k).astype(jnp.float32) * MEM_HEAD_DIM ** -0.5
    p = jax.nn.softmax(s, axis=-1).astype(mem_v.dtype)
    return jnp.einsum('bhtm,bmhe->bthe', p, mem_v).reshape(B, T, MEM_Q)


def moe_ffn(x, router_w, router_b, w1, b1, w2, b2):
    B, T, D = x.shape
    n = B * T
    xt = x.reshape(n, D)
    logits = (xt @ router_w + router_b).astype(jnp.float32)
    top_v, top_i = lax.top_k(logits, TOP_K)
    gate = jax.nn.softmax(top_v, axis=-1)
    dense_gate = jnp.einsum('nk,nke->ne', gate, jax.nn.one_hot(top_i, N_EXPERTS, dtype=jnp.float32))
    nblk = n // MOE_BLOCK

    def expert_block(args):
        xb, gb = args
        hb = jnp.einsum('nd,edf->nef', xb, w1) + b1
        glu = jnp.minimum(hb[..., ::2], SWIGLU_LIMIT)
        lin = jnp.clip(hb[..., 1::2], -SWIGLU_LIMIT, SWIGLU_LIMIT)
        act = glu * jax.nn.sigmoid(SWIGLU_ALPHA * glu) * (lin + 1.0)
        yb = jnp.einsum('nef,efd->ned', act, w2) + b2
        return jnp.einsum('ned,ne->nd', yb, gb.astype(yb.dtype))

    y = lax.map(expert_block, (xt.reshape(nblk, MOE_BLOCK, D), dense_gate.reshape(nblk, MOE_BLOCK, N_EXPERTS)))
    return y.reshape(B, T, D)


def setup_inputs(seed: int = 0) -> dict:
    key = jax.random.key(seed)
    keys = iter(jax.random.split(key, 8 + 24 * DEPTH))
    f32 = jnp.float32

    def normal(shape, scale):
        return jax.random.normal(next(keys), shape, f32) * scale

    def gain(n):
        return 1.0 + normal((n,), 0.02)

    def small(shape):
        return normal(shape, 0.01)

    p = {}
    p['x'] = normal((BATCH, SEQ, D_MODEL), 1.0)
    p['mem'] = normal((BATCH, N_MEM, D_MODEL), 1.0)
    p['mem_ln_g'] = gain(D_MODEL)
    p['mem_ln_b'] = small((D_MODEL,))
    p['w_mem_kv'] = normal((D_MODEL, 2 * MEM_Q), D_MODEL ** -0.5)
    for i in range(DEPTH):
        kind = i % N_MIXERS
        if kind == 0:
            p[f'w_in_{i}'] = normal((D_MODEL, GDN_IN + MEM_Q), D_MODEL ** -0.5)
            p[f'conv_w_{i}'] = normal((GDN_CONV_TAPS, GDN_CONV_DIM), GDN_CONV_TAPS ** -0.5)
            p[f'a_log_{i}'] = jnp.log(jax.random.uniform(next(keys), (GDN_V_HEADS,), f32, 1.0, 16.0))
            dt = jnp.exp(jax.random.uniform(next(keys), (GDN_V_HEADS,), f32, math.log(1e-3), math.log(1e-1)))
            p[f'dt_bias_{i}'] = dt + jnp.log(-jnp.expm1(-dt))
            p[f'gdn_norm_g_{i}'] = gain(GDN_HEAD_DIM)
            out_in = GDN_OUT + MEM_Q
        elif kind == 1:
            p[f'w_in_{i}'] = normal((D_MODEL, DIL_IN + MEM_Q), D_MODEL ** -0.5)
            out_in = DIL_OUT + MEM_Q
        else:
            p[f'w_in_{i}'] = normal((D_MODEL, DSA_IN + MEM_Q), D_MODEL ** -0.5)
            p[f'q_norm_g_{i}'] = gain(DSA_Q_LORA)
            p[f'kv_norm_g_{i}'] = gain(DSA_KV_LORA)
            p[f'w_q_up_{i}'] = normal((DSA_Q_LORA, DSA_HEADS * (DSA_NOPE_DIM + DSA_ROPE_DIM)), DSA_Q_LORA ** -0.5)
            p[f'w_idx_q_{i}'] = normal((DSA_Q_LORA, IDX_HEADS * IDX_DIM), DSA_Q_LORA ** -0.5)
            p[f'idx_k_norm_g_{i}'] = gain(IDX_DIM)
            p[f'idx_k_norm_b_{i}'] = small((IDX_DIM,))
            p[f'w_kv_up_{i}'] = normal((DSA_KV_LORA, DSA_HEADS * (DSA_NOPE_DIM + DSA_V_DIM)), DSA_KV_LORA ** -0.5)
            out_in = DSA_OUT + MEM_Q
        p[f'w_out_{i}'] = normal((out_in, D_MODEL), out_in ** -0.5 * DEEPNORM_BETA)
        p[f'ln1_g_{i}'] = gain(D_MODEL)
        p[f'ln1_b_{i}'] = small((D_MODEL,))
        p[f'router_w_{i}'] = normal((D_MODEL, N_EXPERTS), D_MODEL ** -0.5)
        p[f'router_b_{i}'] = small((N_EXPERTS,))
        p[f'moe_w1_{i}'] = normal((N_EXPERTS, D_MODEL, 2 * MOE_FF), D_MODEL ** -0.5)
        p[f'moe_b1_{i}'] = small((N_EXPERTS, 2 * MOE_FF))
        p[f'moe_w2_{i}'] = normal((N_EXPERTS, MOE_FF, D_MODEL), MOE_FF ** -0.5 * DEEPNORM_BETA)
        p[f'moe_b2_{i}'] = small((N_EXPERTS, D_MODEL))
        p[f'ln2_g_{i}'] = gain(D_MODEL)
        p[f'ln2_b_{i}'] = small((D_MODEL,))
    return p


def reference(x, mem, mem_ln_g, mem_ln_b, w_mem_kv,
              w_in_0, conv_w_0, a_log_0, dt_bias_0, gdn_norm_g_0, w_out_0,
              ln1_g_0, ln1_b_0, router_w_0, router_b_0, moe_w1_0, moe_b1_0, moe_w2_0, moe_b2_0, ln2_g_0, ln2_b_0,
              w_in_1, w_out_1,
              ln1_g_1, ln1_b_1, router_w_1, router_b_1, moe_w1_1, moe_b1_1, moe_w2_1, moe_b2_1, ln2_g_1, ln2_b_1,
              w_in_2, q_norm_g_2, kv_norm_g_2, w_q_up_2, w_idx_q_2, idx_k_norm_g_2, idx_k_norm_b_2, w_kv_up_2, w_out_2,
              ln1_g_2, ln1_b_2, router_w_2, router_b_2, moe_w1_2, moe_b1_2, moe_w2_2, moe_b2_2, ln2_g_2, ln2_b_2,
              w_in_3, conv_w_3, a_log_3, dt_bias_3, gdn_norm_g_3, w_out_3,
              ln1_g_3, ln1_b_3, router_w_3, router_b_3, moe_w1_3, moe_b1_3, moe_w2_3, moe_b2_3, ln2_g_3, ln2_b_3):
    B, T, _ = x.shape
    mem_kv = (layer_norm(mem, mem_ln_g, mem_ln_b) @ w_mem_kv).reshape(B, mem.shape[1], 2, MEM_HEADS, MEM_HEAD_DIM)
    mem_k = mem_kv[:, :, 0]
    mem_v = mem_kv[:, :, 1]
    layers = (
        (w_in_0, (conv_w_0, a_log_0, dt_bias_0, gdn_norm_g_0), w_out_0, ln1_g_0, ln1_b_0,
         (router_w_0, router_b_0, moe_w1_0, moe_b1_0, moe_w2_0, moe_b2_0), ln2_g_0, ln2_b_0),
        (w_in_1, (), w_out_1, ln1_g_1, ln1_b_1,
         (router_w_1, router_b_1, moe_w1_1, moe_b1_1, moe_w2_1, moe_b2_1), ln2_g_1, ln2_b_1),
        (w_in_2, (q_norm_g_2, kv_norm_g_2, w_q_up_2, w_idx_q_2, idx_k_norm_g_2, idx_k_norm_b_2, w_kv_up_2), w_out_2,
         ln1_g_2, ln1_b_2, (router_w_2, router_b_2, moe_w1_2, moe_b1_2, moe_w2_2, moe_b2_2), ln2_g_2, ln2_b_2),
        (w_in_3, (conv_w_3, a_log_3, dt_bias_3, gdn_norm_g_3), w_out_3, ln1_g_3, ln1_b_3,
         (router_w_3, router_b_3, moe_w1_3, moe_b1_3, moe_w2_3, moe_b2_3), ln2_g_3, ln2_b_3),
    )
    for i in range(DEPTH):
        w_in, mix_params, w_out, ln1_g, ln1_b, ffn_params, ln2_g, ln2_b = layers[i]
        kind = i % N_MIXERS
        h = x @ w_in
        h_mix = h[..., :-MEM_Q]
        q_mem = h[..., -MEM_Q:].reshape(B, T, MEM_HEADS, MEM_HEAD_DIM)
        if kind == 0:
            y_mix = gdn_mixer(h_mix, *mix_params)
        elif kind == 1:
            y_mix = dilated_mixer(h_mix, *mix_params)
        else:
            y_mix = dsa_mixer(h_mix, *mix_params)
        y_mem = memory_attention(q_mem, mem_k, mem_v)
        y = jnp.concatenate([y_mix, y_mem], axis=-1) @ w_out
        x = layer_norm(DEEPNORM_ALPHA * x + y, ln1_g, ln1_b)
        x = layer_norm(DEEPNORM_ALPHA * x + moe_ffn(x, *ffn_params), ln2_g, ln2_b)
    return x
```

```python
import functools
import math

import jax
import jax.numpy as jnp
from jax import lax
from jax.experimental import pallas as pl
from jax.experimental.pallas import tpu as pltpu

F32 = jnp.float32
BF16 = jnp.bfloat16
I32 = jnp.int32
HIGHEST = lax.Precision.HIGHEST

V7X_VMEM_LIMIT_BYTES = 56 * 1024 * 1024

DEPTH = 4
N_MIXERS = 3
MEM_HEADS = 4
MEM_HEAD_DIM = 128
MEM_Q = MEM_HEADS * MEM_HEAD_DIM

GDN_K_HEADS = 16
GDN_V_HEADS = 32
GDN_HEAD_DIM = 128
GDN_KEY_DIM = GDN_K_HEADS * GDN_HEAD_DIM
GDN_VAL_DIM = GDN_V_HEADS * GDN_HEAD_DIM
GDN_CONV_DIM = 2 * GDN_KEY_DIM + GDN_VAL_DIM
GDN_CONV_TAPS = 4
GDN_CHUNK = 64

DIL_GROUPS = ((128, 1), (512, 4), (2048, 16))
DIL_HEADS = 16
DIL_HEAD_DIM = 128
DIL_OUT = DIL_HEADS * DIL_HEAD_DIM

DSA_HEADS = 32
DSA_NOPE_DIM = 128
DSA_ROPE_DIM = 64
DSA_V_DIM = 128
DSA_Q_LORA = 1024
DSA_KV_LORA = 512
IDX_HEADS = 32
IDX_DIM = 128
IDX_ROPE_DIM = 64
DSA_TOPK_MAX = 256
ROPE_THETA = 10000.0

N_EXPERTS = 32
TOP_K = 4
MOE_FF = 384
SWIGLU_ALPHA = 1.702
SWIGLU_LIMIT = 7.0

DEEPNORM_ALPHA = (2 * DEPTH) ** 0.25
LN_EPS = 1e-5
RMS_EPS = 1e-6


def _cparams(*sem):
    return pltpu.CompilerParams(dimension_semantics=sem, vmem_limit_bytes=V7X_VMEM_LIMIT_BYTES)


def _mm_kernel(*refs, n_pairs):
    o_ref = refs[-1]
    acc = None
    for p in range(n_pairs):
        d = jnp.dot(refs[p][...], refs[n_pairs + p][...], preferred_element_type=F32)
        acc = d if acc is None else acc + d
    o_ref[...] = acc.astype(o_ref.dtype)


def _matmul(a_list, w_list, *, tm, tn, out_dtype):
    m = a_list[0].shape[0]
    n = w_list[0].shape[1]
    tm = min(tm, m)
    tn = min(tn, n)
    assert m % tm == 0 and n % tn == 0, (m, n, tm, tn)
    in_specs = [pl.BlockSpec((tm, a.shape[1]), lambda i, j: (i, 0)) for a in a_list]
    in_specs += [pl.BlockSpec((w.shape[0], tn), lambda i, j: (0, j)) for w in w_list]
    return pl.pallas_call(
        functools.partial(_mm_kernel, n_pairs=len(a_list)),
        grid=(m // tm, n // tn),
        in_specs=in_specs,
        out_specs=pl.BlockSpec((tm, tn), lambda i, j: (i, j)),
        out_shape=jax.ShapeDtypeStruct((m, n), out_dtype),
        compiler_params=_cparams("parallel", "arbitrary"),
    )(*a_list, *w_list)


def _ln_rows(v, g, b):
    mu = jnp.mean(v, axis=-1, keepdims=True)
    c = v - mu
    var = jnp.mean(c * c, axis=-1, keepdims=True)
    return c * lax.rsqrt(var + LN_EPS) * g + b


def _ln_res_kernel(x_ref, y_ref, g_ref, b_ref, of_ref, ol_ref):
    out = _ln_rows(DEEPNORM_ALPHA * x_ref[...] + y_ref[...], g_ref[...], b_ref[...])
    of_ref[...] = out
    _to_row_linear(ol_ref, out, out.shape[1] // LANES)


def _ln_residual(x, y, g, b, *, tr=128):
    n, d = x.shape
    tr = min(tr, n)
    spr = d // LANES
    row = pl.BlockSpec((tr, d), lambda i: (i, 0))
    vec = pl.BlockSpec((1, d), lambda i: (0, 0))
    return pl.pallas_call(
        _ln_res_kernel,
        grid=(n // tr,),
        in_specs=[row, row, vec, vec],
        out_specs=[row, pl.BlockSpec((tr * spr, LANES), lambda i: (i, 0))],
        out_shape=[jax.ShapeDtypeStruct((n, d), F32), jax.ShapeDtypeStruct((n * spr, LANES), F32)],
        compiler_params=_cparams("parallel"),
    )(x, y, g.reshape(1, d), b.reshape(1, d))


def _ln_plain_kernel(x_ref, g_ref, b_ref, ob_ref):
    ob_ref[...] = _ln_rows(x_ref[...], g_ref[...], b_ref[...]).astype(BF16)


def _ln_plain(x, g, b, *, tr=128):
    n, d = x.shape
    tr = min(tr, n)
    row = pl.BlockSpec((tr, d), lambda i: (i, 0))
    vec = pl.BlockSpec((1, d), lambda i: (0, 0))
    return pl.pallas_call(
        _ln_plain_kernel,
        grid=(n // tr,),
        in_specs=[row, vec, vec],
        out_specs=row,
        out_shape=jax.ShapeDtypeStruct((n, d), BF16),
        compiler_params=_cparams("parallel"),
    )(x, g.reshape(1, d), b.reshape(1, d))


def _mem_attn_kernel(q_ref, kv_ref, o_ref):
    scale = MEM_HEAD_DIM ** -0.5
    for h in range(MEM_HEADS):
        lo = h * MEM_HEAD_DIM
        q = q_ref[:, lo:lo + MEM_HEAD_DIM].astype(BF16)
        k = kv_ref[0, :, lo:lo + MEM_HEAD_DIM]
        v = kv_ref[0, :, MEM_Q + lo:MEM_Q + lo + MEM_HEAD_DIM]
        s = lax.dot_general(q, k, (((1,), (1,)), ((), ())), preferred_element_type=F32) * scale
        m = jnp.max(s, axis=-1, keepdims=True)
        p = jnp.exp(s - m)
        l = jnp.sum(p, axis=-1, keepdims=True)
        p = (p / l).astype(BF16)
        o_ref[:, lo:lo + MEM_HEAD_DIM] = jnp.dot(p, v, preferred_element_type=F32).astype(o_ref.dtype)


def _mem_attention(q, mem_kv, *, batch, col_block=0, tq=512):
    n = q.shape[0]
    t = n // batch
    tq = min(tq, t)
    nm = mem_kv.shape[1]
    return pl.pallas_call(
        _mem_attn_kernel,
        grid=(batch, t // tq),
        in_specs=[pl.BlockSpec((tq, MEM_Q), lambda b, i: (b * (t // tq) + i, col_block)),
                  pl.BlockSpec((1, nm, 2 * MEM_Q), lambda b, i: (b, 0, 0))],
        out_specs=pl.BlockSpec((tq, MEM_Q), lambda b, i: (b * (t // tq) + i, 0)),
        out_shape=jax.ShapeDtypeStruct((n, MEM_Q), BF16),
        compiler_params=_cparams("parallel", "parallel"),
    )(q, mem_kv)


MOE_TILE = 256
ROUTER_TILE = 512
COMBINE_TILE = 128


def _router_kernel(x_ref, wt_ref, b_ref, idx_ref, gate_ref, rank_ref, cnt_ref, carry_ref):
    tr = x_ref.shape[0]

    @pl.when(pl.program_id(0) == 0)
    def _():
        carry_ref[...] = jnp.zeros_like(carry_ref)

    logits = lax.dot_general(wt_ref[...], x_ref[...], (((1,), (1,)), ((), ())),
                             precision=HIGHEST, preferred_element_type=F32) + b_ref[...]
    e_iota = lax.broadcasted_iota(I32, (N_EXPERTS, tr), 0)
    work = logits
    vals, onehots = [], []
    for k in range(TOP_K):
        m = jnp.max(work, axis=0, keepdims=True)
        idx = jnp.min(jnp.where(work == m, e_iota, N_EXPERTS), axis=0, keepdims=True)
        oh = e_iota == idx
        vals.append(m)
        onehots.append(oh)
        idx_ref[k:k + 1, :] = idx
        work = jnp.where(oh, -jnp.inf, work)
    exps = [jnp.exp(v - vals[0]) for v in vals]
    denom = exps[0] + exps[1] + exps[2] + exps[3]
    for k in range(TOP_K):
        gate_ref[k:k + 1, :] = exps[k] / denom
    mask = (onehots[0] | onehots[1] | onehots[2] | onehots[3])
    r_i = lax.broadcasted_iota(I32, (tr, tr), 0)
    c_i = lax.broadcasted_iota(I32, (tr, tr), 1)
    before = (r_i < c_i).astype(BF16)
    excl = jnp.dot(mask.astype(BF16), before, preferred_element_type=F32)
    rank = carry_ref[...] + excl
    for k in range(TOP_K):
        rank_ref[k:k + 1, :] = jnp.sum(jnp.where(onehots[k], rank, 0.0), axis=0, keepdims=True).astype(I32)
    carry_ref[...] += jnp.sum(mask.astype(F32), axis=1, keepdims=True)
    cnt_ref[...] = carry_ref[...]


def _router(x, router_w, router_b):
    n, d = x.shape
    tr = min(ROUTER_TILE, n)
    slab = pl.BlockSpec((TOP_K, tr), lambda i: (0, i))
    return pl.pallas_call(
        _router_kernel,
        grid=(n // tr,),
        in_specs=[pl.BlockSpec((tr, d), lambda i: (i, 0)),
                  pl.BlockSpec((N_EXPERTS, d), lambda i: (0, 0)),
                  pl.BlockSpec((N_EXPERTS, 1), lambda i: (0, 0))],
        out_specs=[slab, slab, slab, pl.BlockSpec((N_EXPERTS, 1), lambda i: (0, 0))],
        out_shape=[jax.ShapeDtypeStruct((TOP_K, n), I32), jax.ShapeDtypeStruct((TOP_K, n), F32),
                   jax.ShapeDtypeStruct((TOP_K, n), I32), jax.ShapeDtypeStruct((N_EXPERTS, 1), F32)],
        scratch_shapes=[pltpu.VMEM((N_EXPERTS, 1), F32)],
        compiler_params=_cparams("arbitrary"),
    )(x, router_w.T, router_b.reshape(N_EXPERTS, 1))


LANES = 128


def _gather_rows(idx_ref, base, src_hbm, dst_ref, sem, n_rows, spr):
    def body(r, c):
        src = pl.multiple_of(idx_ref[base + r] * spr, spr)
        dst = pl.multiple_of(r * spr, spr)
        pltpu.make_async_copy(src_hbm.at[pl.ds(src, spr)], dst_ref.at[pl.ds(dst, spr)], sem).start()
        return c
    lax.fori_loop(0, n_rows, body, 0)


def _from_row_linear(ref, first, n_rows, spr):
    return jnp.concatenate([ref[pl.ds(first * spr + c, n_rows, stride=spr), :] for c in range(spr)], axis=1)


def _to_row_linear(ref, val, spr):
    n_rows = val.shape[0]
    for c in range(spr):
        ref[pl.ds(c, n_rows, stride=spr), :] = val[:, c * LANES:(c + 1) * LANES]


def _expert_kernel(tile_e_ref, n_tiles_ref, rowtok_ref, x_hbm, w1g_ref, w1l_ref, b1g_ref, b1l_ref, w2_ref, b2_ref,
                   o_ref, xbuf, sem):
    i = pl.program_id(0)
    spr = w1g_ref.shape[1] // LANES
    tm = xbuf.shape[1] // spr
    slot = i % 2
    n_tiles = n_tiles_ref[0]

    @pl.when(i == 0)
    def _():
        _gather_rows(rowtok_ref, 0, x_hbm, xbuf.at[0], sem.at[0], tm, spr)

    @pl.when(i + 1 < n_tiles)
    def _():
        _gather_rows(rowtok_ref, (i + 1) * tm, x_hbm, xbuf.at[1 - slot], sem.at[1 - slot], tm, spr)

    @pl.when(i < n_tiles)
    def _():
        pltpu.make_async_copy(xbuf.at[slot], xbuf.at[slot], sem.at[slot]).wait()
        xs = _from_row_linear(xbuf.at[slot], 0, tm, spr).astype(BF16)
        glu = jnp.dot(xs, w1g_ref[0], preferred_element_type=F32) + b1g_ref[0]
        lin = jnp.dot(xs, w1l_ref[0], preferred_element_type=F32) + b1l_ref[0]
        glu = jnp.minimum(glu, SWIGLU_LIMIT)
        lin = jnp.clip(lin, -SWIGLU_LIMIT, SWIGLU_LIMIT)
        act = glu * jax.nn.sigmoid(SWIGLU_ALPHA * glu) * (lin + 1.0)
        y = jnp.dot(act.astype(BF16), w2_ref[0], preferred_element_type=F32) + b2_ref[0]
        _to_row_linear(o_ref, y, spr)

    @pl.when(i >= n_tiles)
    def _():
        o_ref[...] = jnp.zeros_like(o_ref)


def _expert_ffn(x_lin, tile_e, n_tiles, rowtok, w1g, w1l, b1g, b1l, w2, b2, *, max_tiles):
    tm = MOE_TILE
    d, ff = w1g.shape[1], w1g.shape[2]
    spr = d // LANES
    wmap = lambda i, te, nt, rt: (te[i], 0, 0)
    return pl.pallas_call(
        _expert_kernel,
        grid_spec=pltpu.PrefetchScalarGridSpec(
            num_scalar_prefetch=3,
            grid=(max_tiles,),
            in_specs=[pl.BlockSpec(memory_space=pl.ANY),
                      pl.BlockSpec((1, d, ff), wmap), pl.BlockSpec((1, d, ff), wmap),
                      pl.BlockSpec((1, 1, ff), wmap), pl.BlockSpec((1, 1, ff), wmap),
                      pl.BlockSpec((1, ff, d), wmap), pl.BlockSpec((1, 1, d), wmap)],
            out_specs=pl.BlockSpec((tm * spr, LANES), lambda i, te, nt, rt: (i, 0)),
            scratch_shapes=[pltpu.VMEM((2, tm * spr, LANES), F32), pltpu.SemaphoreType.DMA((2,))]),
        out_shape=jax.ShapeDtypeStruct((max_tiles * tm * spr, LANES), F32),
        compiler_params=_cparams("arbitrary"),
    )(tile_e, n_tiles, rowtok, x_lin, w1g, w1l, b1g, b1l, w2, b2)


def _combine_kernel(pos_ref, ys_hbm, x_ref, gate_ref, g_ref, b_ref, of_ref, ob_ref, ybuf, sem):
    i = pl.program_id(0)
    nsteps = pl.num_programs(0)
    tt, d = x_ref.shape
    spr = d // LANES
    n = nsteps * tt
    slot = i % 2

    def fetch(step, s):
        for k in range(TOP_K):
            _gather_rows(pos_ref, k * n + step * tt, ys_hbm, ybuf.at[s, pl.ds(k * tt * spr, tt * spr)], sem.at[s],
                         tt, spr)

    @pl.when(i == 0)
    def _():
        fetch(0, 0)

    @pl.when(i + 1 < nsteps)
    def _():
        fetch(i + 1, 1 - slot)

    pltpu.make_async_copy(ybuf.at[slot], ybuf.at[slot], sem.at[slot]).wait()
    gate = gate_ref[...]
    moe = None
    for k in range(TOP_K):
        term = gate[:, k:k + 1] * _from_row_linear(ybuf.at[slot], k * tt, tt, spr)
        moe = term if moe is None else moe + term
    out = _ln_rows(DEEPNORM_ALPHA * x_ref[...] + moe, g_ref[...], b_ref[...])
    of_ref[...] = out
    ob_ref[...] = out.astype(BF16)


def _combine_ln(pos_flat, ys, x, gate_t, g, b):
    n, d = x.shape
    tt = min(COMBINE_TILE, n)
    spr = d // LANES
    row = pl.BlockSpec((tt, d), lambda i, p: (i, 0))
    vec = pl.BlockSpec((1, d), lambda i, p: (0, 0))
    return pl.pallas_call(
        _combine_kernel,
        grid_spec=pltpu.PrefetchScalarGridSpec(
            num_scalar_prefetch=1,
            grid=(n // tt,),
            in_specs=[pl.BlockSpec(memory_space=pl.ANY), row,
                      pl.BlockSpec((tt, TOP_K), lambda i, p: (i, 0)), vec, vec],
            out_specs=[row, row],
            scratch_shapes=[pltpu.VMEM((2, TOP_K * tt * spr, LANES), F32), pltpu.SemaphoreType.DMA((2,))]),
        out_shape=[jax.ShapeDtypeStruct((n, d), F32), jax.ShapeDtypeStruct((n, d), BF16)],
        compiler_params=_cparams("arbitrary"),
    )(pos_flat, ys, x, gate_t, g.reshape(1, d), b.reshape(1, d))


def _moe_block(x_f32, x_lin, router_w, router_b, w1, b1, w2, b2, ln_g, ln_b):
    n, d = x_f32.shape
    tm = MOE_TILE
    top_i, gate, rank, counts = _router(x_f32, router_w, router_b)
    counts = counts[:, 0].astype(I32)
    padded = (counts + tm - 1) // tm * tm
    ends = jnp.cumsum(padded)
    starts = ends - padded
    max_tiles = (n * TOP_K) // tm + N_EXPERTS
    pos = starts[top_i] + rank
    tile_start = jnp.arange(max_tiles, dtype=I32) * tm
    tile_e = jnp.minimum(jnp.searchsorted(ends, tile_start, side='right'), N_EXPERTS - 1).astype(I32)
    n_tiles = (ends[-1] // tm).astype(I32).reshape(1)
    tok = jnp.broadcast_to(jnp.arange(n, dtype=I32)[None, :], (TOP_K, n))
    rowtok = jnp.zeros((max_tiles * tm,), I32).at[pos.reshape(-1)].set(tok.reshape(-1))
    w1g = w1[:, :, 0::2].astype(BF16)
    w1l = w1[:, :, 1::2].astype(BF16)
    b1g = b1[:, None, 0::2]
    b1l = b1[:, None, 1::2]
    ys = _expert_ffn(x_lin, tile_e, n_tiles, rowtok, w1g, w1l, b1g, b1l, w2.astype(BF16), b2[:, None, :],
                     max_tiles=max_tiles)
    return _combine_ln(pos.reshape(-1), ys, x_f32, gate.T, ln_g, ln_b)


def _rms_norm(x, g):
    xf = x.astype(F32)
    return (xf * lax.rsqrt(jnp.mean(jnp.square(xf), axis=-1, keepdims=True) + RMS_EPS) * g).astype(x.dtype)


def _l2_normalize(x):
    xf = x.astype(F32)
    return xf * lax.rsqrt(jnp.sum(jnp.square(xf), axis=-1, keepdims=True) + RMS_EPS)


def _layer_norm(x, g, b):
    xf = x.astype(F32)
    mu = jnp.mean(xf, axis=-1, keepdims=True)
    var = jnp.mean(jnp.square(xf - mu), axis=-1, keepdims=True)
    return ((xf - mu) * lax.rsqrt(var + LN_EPS) * g + b).astype(x.dtype)


def _rope_tables(T, dim):
    inv_freq = ROPE_THETA ** (-jnp.arange(0, dim, 2, dtype=F32) / dim)
    ang = jnp.arange(T, dtype=F32)[:, None] * inv_freq[None, :]
    return jnp.cos(ang), jnp.sin(ang)


def _apply_rope(x, cos, sin):
    x1, x2 = jnp.split(x.astype(F32), 2, axis=-1)
    return jnp.concatenate([x1 * cos - x2 * sin, x2 * cos + x1 * sin], axis=-1).astype(x.dtype)


def _causal_depthwise_conv(x, w):
    taps, c = w.shape
    return lax.conv_general_dilated(x, w[:, None, :].astype(x.dtype), window_strides=(1,),
                                    padding=((taps - 1, 0),), dimension_numbers=('NWC', 'WIO', 'NWC'),
                                    feature_group_count=c)


def _gated_delta_rule_chunked(q, k, v, g, beta):
    B, T, H, DK = q.shape
    DV = v.shape[-1]
    C = GDN_CHUNK
    N = T // C

    def to_chunks(a):
        a = a.astype(F32).reshape((B, N, C, H) + a.shape[3:])
        return jnp.moveaxis(a, 3, 1)

    q, k, v, g, beta = (to_chunks(a) for a in (q, k, v, g, beta))
    gc = jnp.cumsum(g, axis=-1)
    causal = jnp.tril(jnp.ones((C, C), dtype=bool))
    strict = jnp.tril(jnp.ones((C, C), dtype=bool), -1)
    diff = gc[..., :, None] - gc[..., None, :]
    decay = jnp.where(causal, jnp.exp(jnp.where(causal, diff, 0.0)), 0.0)
    kb = k * beta[..., None]
    lower = jnp.where(strict, jnp.einsum('bhnid,bhnjd->bhnij', kb, k) * decay, 0.0)
    eye = jnp.broadcast_to(jnp.eye(C, dtype=F32), lower.shape)
    tinv = lax.linalg.triangular_solve(eye + lower, eye, left_side=True, lower=True, unit_diagonal=True)
    u = tinv @ (v * beta[..., None])
    w = tinv @ (kb * jnp.exp(gc)[..., None])
    a_intra = jnp.where(causal, jnp.einsum('bhnid,bhnjd->bhnij', q, k) * decay, 0.0)
    q_dec = q * jnp.exp(gc)[..., None]
    k_dec = k * jnp.exp(gc[..., -1:] - gc)[..., None]
    g_last = jnp.exp(gc[..., -1])

    def step(state, xs):
        qd, kd, u_i, w_i, a_i, gl = xs
        v_new = u_i - w_i @ state
        o = qd @ state + a_i @ v_new
        state = state * gl[..., None, None] + jnp.einsum('bhcd,bhce->bhde', kd, v_new)
        return state, o

    xs = tuple(jnp.moveaxis(a, 2, 0) for a in (q_dec, k_dec, u, w, a_intra, g_last))
    _, o = lax.scan(step, jnp.zeros((B, H, DK, DV), F32), xs)
    return jnp.transpose(o, (1, 0, 3, 2, 4)).reshape(B, T, H, DV)


def _gdn_mixer(h, conv_w, a_log, dt_bias, norm_g):
    B, T, _ = h.shape
    qkv, z, b, a = jnp.split(h, [GDN_CONV_DIM, GDN_CONV_DIM + GDN_VAL_DIM,
                                 GDN_CONV_DIM + GDN_VAL_DIM + GDN_V_HEADS], axis=-1)
    qkv = jax.nn.silu(_causal_depthwise_conv(qkv, conv_w))
    q, k, v = jnp.split(qkv, [GDN_KEY_DIM, 2 * GDN_KEY_DIM], axis=-1)
    rep = GDN_V_HEADS // GDN_K_HEADS
    q = jnp.repeat(q.reshape(B, T, GDN_K_HEADS, GDN_HEAD_DIM), rep, axis=2)
    k = jnp.repeat(k.reshape(B, T, GDN_K_HEADS, GDN_HEAD_DIM), rep, axis=2)
    v = v.reshape(B, T, GDN_V_HEADS, GDN_HEAD_DIM)
    q = _l2_normalize(q) * GDN_HEAD_DIM ** -0.5
    k = _l2_normalize(k)
    beta = jax.nn.sigmoid(b.astype(F32))
    g = -jnp.exp(a_log.astype(F32)) * jax.nn.softplus(a.astype(F32) + dt_bias.astype(F32))
    o = _gated_delta_rule_chunked(q, k, v, g, beta)
    o = _rms_norm(o, norm_g) * jax.nn.silu(z.reshape(B, T, GDN_V_HEADS, GDN_HEAD_DIM).astype(F32))
    return o.reshape(B, T, GDN_VAL_DIM).astype(h.dtype)


def _dilated_window_attention(q, k, v, window, dilation):
    B, T, H, E = q.shape
    d = dilation
    w = window // dilation
    ls = T // d
    nb = -(-ls // w)
    pad = nb * w - ls

    def to_sub(a):
        a = jnp.swapaxes(a.reshape(B, ls, d, H, E), 1, 2)
        return jnp.pad(a, ((0, 0), (0, 0), (0, pad), (0, 0), (0, 0)))

    def band(a):
        a = jnp.pad(a, ((0, 0), (0, 0), (w, 0), (0, 0), (0, 0))).reshape(B, d, nb + 1, w, H, E)
        return jnp.concatenate([a[:, :, :-1], a[:, :, 1:]], axis=3)

    qs = to_sub(q).reshape(B, d, nb, w, H, E)
    kb = band(to_sub(k))
    vb = band(to_sub(v))
    s = jnp.einsum('bdnqhe,bdnkhe->bdnhqk', qs, kb).astype(F32) * E ** -0.5
    qi = jnp.arange(w)[:, None]
    kj = jnp.arange(2 * w)[None, :]
    dist = w + qi - kj
    key_sub = jnp.arange(nb)[:, None, None] * w + kj[None] - w
    valid = (dist >= 0) & (dist <= w) & (key_sub >= 0)
    s = jnp.where(valid[:, None], s, -jnp.inf)
    m = jnp.max(s, axis=-1, keepdims=True)
    pexp = jnp.exp(s - m)
    l = jnp.sum(pexp, axis=-1, keepdims=True)
    o = jnp.einsum('bdnhqk,bdnkhe->bdnqhe', pexp, vb.astype(F32)) / jnp.swapaxes(l, 3, 4)
    lse = jnp.swapaxes((m + jnp.log(l))[..., 0], 3, 4)

    def from_sub(a):
        a = a.reshape((B, d, nb * w) + a.shape[4:])[:, :, :ls]
        return jnp.swapaxes(a, 1, 2).reshape((B, T) + a.shape[3:])

    return from_sub(o), from_sub(lse)


def _dilated_mixer(h):
    B, T, _ = h.shape
    qkv = h.reshape(B, T, len(DIL_GROUPS), 3, DIL_HEADS, DIL_HEAD_DIM)
    outs, lses = [], []
    for gi, (window, dilation) in enumerate(DIL_GROUPS):
        o, lse = _dilated_window_attention(qkv[:, :, gi, 0], qkv[:, :, gi, 1], qkv[:, :, gi, 2], window, dilation)
        outs.append(o)
        lses.append(lse)
    wts = jax.nn.softmax(jnp.stack(lses, axis=0), axis=0)
    o = jnp.sum(wts[..., None] * jnp.stack(outs, axis=0), axis=0)
    return o.reshape(B, T, DIL_OUT).astype(h.dtype)


def _dsa_mixer(h, q_norm_g, kv_norm_g, w_q_up, w_idx_q, idx_k_norm_g, idx_k_norm_b, w_kv_up):
    B, T, _ = h.shape
    o1 = DSA_Q_LORA
    o2 = o1 + DSA_KV_LORA
    o3 = o2 + DSA_ROPE_DIM
    o4 = o3 + IDX_DIM
    q_lat, kv_lat, k_rope, idx_k, idx_w = jnp.split(h, [o1, o2, o3, o4], axis=-1)
    cos, sin = _rope_tables(T, DSA_ROPE_DIM)
    cq = _rms_norm(q_lat, q_norm_g)
    q = (cq @ w_q_up).reshape(B, T, DSA_HEADS, DSA_NOPE_DIM + DSA_ROPE_DIM)
    q_nope = q[..., :DSA_NOPE_DIM]
    q_rope = _apply_rope(q[..., DSA_NOPE_DIM:], cos[:, None], sin[:, None])
    c_kv = _rms_norm(kv_lat, kv_norm_g)
    k_rope = _apply_rope(k_rope, cos, sin)
    w_kv = w_kv_up.reshape(DSA_KV_LORA, DSA_HEADS, DSA_NOPE_DIM + DSA_V_DIM)
    w_uk = w_kv[..., :DSA_NOPE_DIM]
    w_uv = w_kv[..., DSA_NOPE_DIM:]
    iq = (cq @ w_idx_q).reshape(B, T, IDX_HEADS, IDX_DIM)
    iq = jnp.concatenate([_apply_rope(iq[..., :IDX_ROPE_DIM], cos[:, None], sin[:, None]),
                          iq[..., IDX_ROPE_DIM:]], axis=-1).astype(F32)
    ik = _layer_norm(idx_k, idx_k_norm_g, idx_k_norm_b)
    ik = jnp.concatenate([_apply_rope(ik[..., :IDX_ROPE_DIM], cos, sin), ik[..., IDX_ROPE_DIM:]],
                         axis=-1).astype(F32)
    iw = idx_w.astype(F32) * (IDX_HEADS ** -0.5 * IDX_DIM ** -0.5)
    topk = min(DSA_TOPK_MAX, T // 4)
    scale = (DSA_NOPE_DIM + DSA_ROPE_DIM) ** -0.5
    key_pos = jnp.arange(T)
    gather = jax.vmap(lambda table, idx: table[idx])
    qb = 128

    def attend_block(args):
        qn, qr, iqb, iwb, qpos = args
        isc = jax.nn.relu(jnp.einsum('bqhd,bsd->bqhs', iqb, ik))
        isc = jnp.einsum('bqhs,bqh->bqs', isc, iwb)
        isc = jnp.where(key_pos[None, None, :] <= qpos[None, :, None], isc, -jnp.inf)
        _, sel = lax.top_k(isc, topk)
        c_sel = gather(c_kv, sel)
        r_sel = gather(k_rope, sel)
        q_abs = jnp.einsum('bqhd,chd->bqhc', qn, w_uk)
        s = (jnp.einsum('bqhc,bqkc->bqhk', q_abs, c_sel)
             + jnp.einsum('bqhr,bqkr->bqhk', qr, r_sel)).astype(F32) * scale
        s = jnp.where((sel <= qpos[None, :, None])[:, :, None, :], s, -jnp.inf)
        p = jax.nn.softmax(s, axis=-1).astype(c_sel.dtype)
        o_lat = jnp.einsum('bqhk,bqkc->bqhc', p, c_sel)
        return jnp.einsum('bqhc,chd->bqhd', o_lat, w_uv)

    nqb = T // qb

    def blocks(a):
        return jnp.moveaxis(a.reshape((B, nqb, qb) + a.shape[2:]), 1, 0)

    qpos = jnp.arange(T).reshape(nqb, qb)
    out = lax.map(attend_block, (blocks(q_nope), blocks(q_rope), blocks(iq), blocks(iw), qpos))
    return jnp.moveaxis(out, 0, 1).reshape(B, T, DSA_HEADS * DSA_V_DIM).astype(h.dtype)


def _trunk(x, mem, mem_ln_g, mem_ln_b, w_mem_kv, layers):
    B, T, D = x.shape
    n = B * T
    nm = mem.shape[1]
    mem_n = _ln_plain(mem.reshape(B * nm, D), mem_ln_g, mem_ln_b)
    mem_kv = _matmul([mem_n], [w_mem_kv.astype(BF16)], tm=512, tn=512, out_dtype=BF16).reshape(B, nm, 2 * MEM_Q)
    x_f = x.reshape(n, D)
    x_b = x_f.astype(BF16)
    for i, (w_in, mix_params, w_out, ln1_g, ln1_b, ffn_params, ln2_g, ln2_b) in enumerate(layers):
        kind = i % N_MIXERS
        n_in = w_in.shape[1]
        n_mix = n_in - MEM_Q
        w_in_b = jnp.pad(w_in.astype(BF16), ((0, 0), (0, -n_in % 512)))
        h = _matmul([x_b], [w_in_b], tm=1024, tn=512, out_dtype=F32)
        h_mix = h[:, :n_mix].reshape(B, T, n_mix)
        q_mem = h[:, n_mix:n_in]
        if kind == 0:
            y_mix = _gdn_mixer(h_mix, *mix_params)
        elif kind == 1:
            y_mix = _dilated_mixer(h_mix)
        else:
            y_mix = _dsa_mixer(h_mix, *mix_params)
        y_mem = _mem_attention(q_mem, mem_kv, batch=B)
        y_mix = y_mix.reshape(n, -1).astype(BF16)
        w_out_b = w_out.astype(BF16)
        n_out = y_mix.shape[1]
        y = _matmul([y_mix, y_mem], [w_out_b[:n_out], w_out_b[n_out:]], tm=1024, tn=512, out_dtype=F32)
        x_f, x_lin = _ln_residual(x_f, y, ln1_g, ln1_b)
        x_f, x_b = _moe_block(x_f, x_lin, *ffn_params, ln2_g, ln2_b)
    return x_f.reshape(B, T, D)


def _pick_tn(n):
    for tn in (512, 384, 256, 128):
        if n % tn == 0:
            return tn
    return n


def kernel(x, mem, mem_ln_g, mem_ln_b, w_mem_kv, w_in_0, conv_w_0, a_log_0, dt_bias_0, gdn_norm_g_0, w_out_0, ln1_g_0, ln1_b_0, router_w_0, router_b_0, moe_w1_0, moe_b1_0, moe_w2_0, moe_b2_0, ln2_g_0, ln2_b_0, w_in_1, w_out_1, ln1_g_1, ln1_b_1, router_w_1, router_b_1, moe_w1_1, moe_b1_1, moe_w2_1, moe_b2_1, ln2_g_1, ln2_b_1, w_in_2, q_norm_g_2, kv_norm_g_2, w_q_up_2, w_idx_q_2, idx_k_norm_g_2, idx_k_norm_b_2, w_kv_up_2, w_out_2, ln1_g_2, ln1_b_2, router_w_2, router_b_2, moe_w1_2, moe_b1_2, moe_w2_2, moe_b2_2, ln2_g_2, ln2_b_2, w_in_3, conv_w_3, a_log_3, dt_bias_3, gdn_norm_g_3, w_out_3, ln1_g_3, ln1_b_3, router_w_3, router_b_3, moe_w1_3, moe_b1_3, moe_w2_3, moe_b2_3, ln2_g_3, ln2_b_3):
    layers = (
        (w_in_0, (conv_w_0, a_log_0, dt_bias_0, gdn_norm_g_0), w_out_0, ln1_g_0, ln1_b_0,
         (router_w_0, router_b_0, moe_w1_0, moe_b1_0, moe_w2_0, moe_b2_0), ln2_g_0, ln2_b_0),
        (w_in_1, (), w_out_1, ln1_g_1, ln1_b_1,
         (router_w_1, router_b_1, moe_w1_1, moe_b1_1, moe_w2_1, moe_b2_1), ln2_g_1, ln2_b_1),
        (w_in_2, (q_norm_g_2, kv_norm_g_2, w_q_up_2, w_idx_q_2, idx_k_norm_g_2, idx_k_norm_b_2, w_kv_up_2), w_out_2,
         ln1_g_2, ln1_b_2, (router_w_2, router_b_2, moe_w1_2, moe_b1_2, moe_w2_2, moe_b2_2), ln2_g_2, ln2_b_2),
        (w_in_3, (conv_w_3, a_log_3, dt_bias_3, gdn_norm_g_3), w_out_3, ln1_g_3, ln1_b_3,
         (router_w_3, router_b_3, moe_w1_3, moe_b1_3, moe_w2_3, moe_b2_3), ln2_g_3, ln2_b_3),
    )
    return _trunk(x, mem, mem_ln_g, mem_ln_b, w_mem_kv, layers)
```

```python
import functools
import math

import jax
import jax.numpy as jnp
from jax import lax
from jax.experimental import pallas as pl
from jax.experimental.pallas import tpu as pltpu

F32 = jnp.float32
BF16 = jnp.bfloat16
I32 = jnp.int32
HIGHEST = lax.Precision.HIGHEST

V7X_VMEM_LIMIT_BYTES = 56 * 1024 * 1024

DEPTH = 4
N_MIXERS = 3
MEM_HEADS = 4
MEM_HEAD_DIM = 128
MEM_Q = MEM_HEADS * MEM_HEAD_DIM

GDN_K_HEADS = 16
GDN_V_HEADS = 32
GDN_HEAD_DIM = 128
GDN_KEY_DIM = GDN_K_HEADS * GDN_HEAD_DIM
GDN_VAL_DIM = GDN_V_HEADS * GDN_HEAD_DIM
GDN_CONV_DIM = 2 * GDN_KEY_DIM + GDN_VAL_DIM
GDN_CONV_TAPS = 4
GDN_CHUNK = 64

DIL_GROUPS = ((128, 1), (512, 4), (2048, 16))
DIL_HEADS = 16
DIL_HEAD_DIM = 128
DIL_OUT = DIL_HEADS * DIL_HEAD_DIM

DSA_HEADS = 32
DSA_NOPE_DIM = 128
DSA_ROPE_DIM = 64
DSA_V_DIM = 128
DSA_Q_LORA = 1024
DSA_KV_LORA = 512
IDX_HEADS = 32
IDX_DIM = 128
IDX_ROPE_DIM = 64
DSA_TOPK_MAX = 256
ROPE_THETA = 10000.0

N_EXPERTS = 32
TOP_K = 4
MOE_FF = 384
SWIGLU_ALPHA = 1.702
SWIGLU_LIMIT = 7.0

DEEPNORM_ALPHA = (2 * DEPTH) ** 0.25
LN_EPS = 1e-5
RMS_EPS = 1e-6


def _cparams(*sem):
    return pltpu.CompilerParams(dimension_semantics=sem, vmem_limit_bytes=V7X_VMEM_LIMIT_BYTES)


def _mm_kernel(*refs, n_pairs):
    o_ref = refs[-1]
    acc = None
    for p in range(n_pairs):
        d = jnp.dot(refs[p][...], refs[n_pairs + p][...], preferred_element_type=F32)
        acc = d if acc is None else acc + d
    o_ref[...] = acc.astype(o_ref.dtype)


def _matmul(a_list, w_list, *, tm, tn, out_dtype):
    m = a_list[0].shape[0]
    n = w_list[0].shape[1]
    tm = min(tm, m)
    tn = min(tn, n)
    assert m % tm == 0 and n % tn == 0, (m, n, tm, tn)
    in_specs = [pl.BlockSpec((tm, a.shape[1]), lambda i, j: (i, 0)) for a in a_list]
    in_specs += [pl.BlockSpec((w.shape[0], tn), lambda i, j: (0, j)) for w in w_list]
    return pl.pallas_call(
        functools.partial(_mm_kernel, n_pairs=len(a_list)),
        grid=(m // tm, n // tn),
        in_specs=in_specs,
        out_specs=pl.BlockSpec((tm, tn), lambda i, j: (i, j)),
        out_shape=jax.ShapeDtypeStruct((m, n), out_dtype),
        compiler_params=_cparams("parallel", "arbitrary"),
    )(*a_list, *w_list)


def _ln_rows(v, g, b):
    mu = jnp.mean(v, axis=-1, keepdims=True)
    c = v - mu
    var = jnp.mean(c * c, axis=-1, keepdims=True)
    return c * lax.rsqrt(var + LN_EPS) * g + b


def _ln_res_kernel(x_ref, y_ref, g_ref, b_ref, of_ref, ol_ref):
    out = _ln_rows(DEEPNORM_ALPHA * x_ref[...] + y_ref[...], g_ref[...], b_ref[...])
    of_ref[...] = out
    _to_row_linear(ol_ref, out, out.shape[1] // LANES)


def _ln_residual(x, y, g, b, *, tr=128):
    n, d = x.shape
    tr = min(tr, n)
    spr = d // LANES
    row = pl.BlockSpec((tr, d), lambda i: (i, 0))
    vec = pl.BlockSpec((1, d), lambda i: (0, 0))
    return pl.pallas_call(
        _ln_res_kernel,
        grid=(n // tr,),
        in_specs=[row, row, vec, vec],
        out_specs=[row, pl.BlockSpec((tr * spr, LANES), lambda i: (i, 0))],
        out_shape=[jax.ShapeDtypeStruct((n, d), F32), jax.ShapeDtypeStruct((n * spr, LANES), F32)],
        compiler_params=_cparams("parallel"),
    )(x, y, g.reshape(1, d), b.reshape(1, d))


def _ln_plain_kernel(x_ref, g_ref, b_ref, ob_ref):
    ob_ref[...] = _ln_rows(x_ref[...], g_ref[...], b_ref[...]).astype(BF16)


def _ln_plain(x, g, b, *, tr=128):
    n, d = x.shape
    tr = min(tr, n)
    row = pl.BlockSpec((tr, d), lambda i: (i, 0))
    vec = pl.BlockSpec((1, d), lambda i: (0, 0))
    return pl.pallas_call(
        _ln_plain_kernel,
        grid=(n // tr,),
        in_specs=[row, vec, vec],
        out_specs=row,
        out_shape=jax.ShapeDtypeStruct((n, d), BF16),
        compiler_params=_cparams("parallel"),
    )(x, g.reshape(1, d), b.reshape(1, d))


def _mem_attn_kernel(q_ref, kv_ref, o_ref):
    scale = MEM_HEAD_DIM ** -0.5
    for h in range(MEM_HEADS):
        lo = h * MEM_HEAD_DIM
        q = q_ref[:, lo:lo + MEM_HEAD_DIM].astype(BF16)
        k = kv_ref[0, :, lo:lo + MEM_HEAD_DIM]
        v = kv_ref[0, :, MEM_Q + lo:MEM_Q + lo + MEM_HEAD_DIM]
        s = lax.dot_general(q, k, (((1,), (1,)), ((), ())), preferred_element_type=F32) * scale
        m = jnp.max(s, axis=-1, keepdims=True)
        p = jnp.exp(s - m)
        l = jnp.sum(p, axis=-1, keepdims=True)
        p = (p / l).astype(BF16)
        o_ref[:, lo:lo + MEM_HEAD_DIM] = jnp.dot(p, v, preferred_element_type=F32).astype(o_ref.dtype)


def _mem_attention(q, mem_kv, *, batch, col_block=0, tq=512):
    n = q.shape[0]
    t = n // batch
    tq = min(tq, t)
    nm = mem_kv.shape[1]
    return pl.pallas_call(
        _mem_attn_kernel,
        grid=(batch, t // tq),
        in_specs=[pl.BlockSpec((tq, MEM_Q), lambda b, i: (b * (t // tq) + i, col_block)),
                  pl.BlockSpec((1, nm, 2 * MEM_Q), lambda b, i: (b, 0, 0))],
        out_specs=pl.BlockSpec((tq, MEM_Q), lambda b, i: (b * (t // tq) + i, 0)),
        out_shape=jax.ShapeDtypeStruct((n, MEM_Q), BF16),
        compiler_params=_cparams("parallel", "parallel"),
    )(q, mem_kv)


MOE_TILE = 256
ROUTER_TILE = 512
COMBINE_TILE = 128


def _router_kernel(x_ref, wt_ref, b_ref, idx_ref, gate_ref, rank_ref, cnt_ref, carry_ref):
    tr = x_ref.shape[0]

    @pl.when(pl.program_id(0) == 0)
    def _():
        carry_ref[...] = jnp.zeros_like(carry_ref)

    logits = lax.dot_general(wt_ref[...], x_ref[...], (((1,), (1,)), ((), ())),
                             precision=HIGHEST, preferred_element_type=F32) + b_ref[...]
    e_iota = lax.broadcasted_iota(I32, (N_EXPERTS, tr), 0)
    work = logits
    vals, onehots = [], []
    for k in range(TOP_K):
        m = jnp.max(work, axis=0, keepdims=True)
        idx = jnp.min(jnp.where(work == m, e_iota, N_EXPERTS), axis=0, keepdims=True)
        oh = e_iota == idx
        vals.append(m)
        onehots.append(oh)
        idx_ref[k:k + 1, :] = idx
        work = jnp.where(oh, -jnp.inf, work)
    exps = [jnp.exp(v - vals[0]) for v in vals]
    denom = exps[0] + exps[1] + exps[2] + exps[3]
    for k in range(TOP_K):
        gate_ref[k:k + 1, :] = exps[k] / denom
    mask = (onehots[0] | onehots[1] | onehots[2] | onehots[3])
    r_i = lax.broadcasted_iota(I32, (tr, tr), 0)
    c_i = lax.broadcasted_iota(I32, (tr, tr), 1)
    before = (r_i < c_i).astype(BF16)
    excl = jnp.dot(mask.astype(BF16), before, preferred_element_type=F32)
    rank = carry_ref[...] + excl
    for k in range(TOP_K):
        rank_ref[k:k + 1, :] = jnp.sum(jnp.where(onehots[k], rank, 0.0), axis=0, keepdims=True).astype(I32)
    carry_ref[...] += jnp.sum(mask.astype(F32), axis=1, keepdims=True)
    cnt_ref[...] = carry_ref[...]


def _router(x, router_w, router_b):
    n, d = x.shape
    tr = min(ROUTER_TILE, n)
    slab = pl.BlockSpec((TOP_K, tr), lambda i: (0, i))
    return pl.pallas_call(
        _router_kernel,
        grid=(n // tr,),
        in_specs=[pl.BlockSpec((tr, d), lambda i: (i, 0)),
                  pl.BlockSpec((N_EXPERTS, d), lambda i: (0, 0)),
                  pl.BlockSpec((N_EXPERTS, 1), lambda i: (0, 0))],
        out_specs=[slab, slab, slab, pl.BlockSpec((N_EXPERTS, 1), lambda i: (0, 0))],
        out_shape=[jax.ShapeDtypeStruct((TOP_K, n), I32), jax.ShapeDtypeStruct((TOP_K, n), F32),
                   jax.ShapeDtypeStruct((TOP_K, n), I32), jax.ShapeDtypeStruct((N_EXPERTS, 1), F32)],
        scratch_shapes=[pltpu.VMEM((N_EXPERTS, 1), F32)],
        compiler_params=_cparams("arbitrary"),
    )(x, router_w.T, router_b.reshape(N_EXPERTS, 1))


LANES = 128


def _gather_rows(idx_ref, base, src_hbm, dst_ref, sem, n_rows, spr):
    def body(r, c):
        src = pl.multiple_of(idx_ref[base + r] * spr, spr)
        dst = pl.multiple_of(r * spr, spr)
        pltpu.make_async_copy(src_hbm.at[pl.ds(src, spr)], dst_ref.at[pl.ds(dst, spr)], sem).start()
        return c
    lax.fori_loop(0, n_rows, body, 0)


def _from_row_linear(ref, first, n_rows, spr):
    return jnp.concatenate([ref[pl.ds(first * spr + c, n_rows, stride=spr), :] for c in range(spr)], axis=1)


def _to_row_linear(ref, val, spr):
    n_rows = val.shape[0]
    for c in range(spr):
        ref[pl.ds(c, n_rows, stride=spr), :] = val[:, c * LANES:(c + 1) * LANES]


def _expert_kernel(tile_e_ref, n_tiles_ref, rowtok_ref, x_hbm, w1g_ref, w1l_ref, b1g_ref, b1l_ref, w2_ref, b2_ref,
                   o_ref, xbuf, sem):
    i = pl.program_id(0)
    spr = w1g_ref.shape[1] // LANES
    tm = xbuf.shape[1] // spr
    slot = i % 2
    n_tiles = n_tiles_ref[0]

    @pl.when(i == 0)
    def _():
        _gather_rows(rowtok_ref, 0, x_hbm, xbuf.at[0], sem.at[0], tm, spr)

    @pl.when(i + 1 < n_tiles)
    def _():
        _gather_rows(rowtok_ref, (i + 1) * tm, x_hbm, xbuf.at[1 - slot], sem.at[1 - slot], tm, spr)

    @pl.when(i < n_tiles)
    def _():
        pltpu.make_async_copy(xbuf.at[slot], xbuf.at[slot], sem.at[slot]).wait()
        xs = _from_row_linear(xbuf.at[slot], 0, tm, spr).astype(BF16)
        glu = jnp.dot(xs, w1g_ref[0], preferred_element_type=F32) + b1g_ref[0]
        lin = jnp.dot(xs, w1l_ref[0], preferred_element_type=F32) + b1l_ref[0]
        glu = jnp.minimum(glu, SWIGLU_LIMIT)
        lin = jnp.clip(lin, -SWIGLU_LIMIT, SWIGLU_LIMIT)
        act = glu * jax.nn.sigmoid(SWIGLU_ALPHA * glu) * (lin + 1.0)
        y = jnp.dot(act.astype(BF16), w2_ref[0], preferred_element_type=F32) + b2_ref[0]
        _to_row_linear(o_ref, y, spr)

    @pl.when(i >= n_tiles)
    def _():
        o_ref[...] = jnp.zeros_like(o_ref)


def _w1_prep_kernel(w_ref, o_ref):
    two_ff = w_ref.shape[2]
    src = lax.broadcasted_iota(I32, (two_ff, two_ff), 0)
    dst = lax.broadcasted_iota(I32, (two_ff, two_ff), 1)
    want = jnp.where(dst < two_ff // 2, 2 * dst, 2 * (dst - two_ff // 2) + 1)
    perm = (src == want).astype(BF16)
    o_ref[0] = jnp.dot(w_ref[0].astype(BF16), perm, preferred_element_type=F32).astype(BF16)


def _w1_prep(w1, *, tr=1024):
    e, d, two_ff = w1.shape
    tr = min(tr, d)
    blk = pl.BlockSpec((1, tr, two_ff), lambda i, j: (i, j, 0))
    return pl.pallas_call(
        _w1_prep_kernel,
        grid=(e, d // tr),
        in_specs=[blk],
        out_specs=blk,
        out_shape=jax.ShapeDtypeStruct(w1.shape, BF16),
        compiler_params=_cparams("parallel", "parallel"),
    )(w1)


def _expert_ffn(x_lin, tile_e, n_tiles, rowtok, w1p, b1g, b1l, w2, b2, *, max_tiles):
    tm = MOE_TILE
    d, ff = w1p.shape[1], w1p.shape[2] // 2
    spr = d // LANES
    wmap = lambda i, te, nt, rt: (te[i], 0, 0)
    w1g = w1l = w1p
    return pl.pallas_call(
        _expert_kernel,
        grid_spec=pltpu.PrefetchScalarGridSpec(
            num_scalar_prefetch=3,
            grid=(max_tiles,),
            in_specs=[pl.BlockSpec(memory_space=pl.ANY),
                      pl.BlockSpec((1, d, ff), wmap), pl.BlockSpec((1, d, ff), lambda i, te, nt, rt: (te[i], 0, 1)),
                      pl.BlockSpec((1, 1, ff), wmap), pl.BlockSpec((1, 1, ff), wmap),
                      pl.BlockSpec((1, ff, d), wmap), pl.BlockSpec((1, 1, d), wmap)],
            out_specs=pl.BlockSpec((tm * spr, LANES), lambda i, te, nt, rt: (i, 0)),
            scratch_shapes=[pltpu.VMEM((2, tm * spr, LANES), F32), pltpu.SemaphoreType.DMA((2,))]),
        out_shape=jax.ShapeDtypeStruct((max_tiles * tm * spr, LANES), F32),
        compiler_params=_cparams("arbitrary"),
    )(tile_e, n_tiles, rowtok, x_lin, w1g, w1l, b1g, b1l, w2, b2)


def _combine_kernel(pos_ref, ys_hbm, x_ref, gate_ref, g_ref, b_ref, of_ref, ob_ref, ybuf, sem):
    i = pl.program_id(0)
    nsteps = pl.num_programs(0)
    tt, d = x_ref.shape
    spr = d // LANES
    n = nsteps * tt
    slot = i % 2

    def fetch(step, s):
        for k in range(TOP_K):
            _gather_rows(pos_ref, k * n + step * tt, ys_hbm, ybuf.at[s, pl.ds(k * tt * spr, tt * spr)], sem.at[s],
                         tt, spr)

    @pl.when(i == 0)
    def _():
        fetch(0, 0)

    @pl.when(i + 1 < nsteps)
    def _():
        fetch(i + 1, 1 - slot)

    pltpu.make_async_copy(ybuf.at[slot], ybuf.at[slot], sem.at[slot]).wait()
    gate = gate_ref[...]
    moe = None
    for k in range(TOP_K):
        term = gate[:, k:k + 1] * _from_row_linear(ybuf.at[slot], k * tt, tt, spr)
        moe = term if moe is None else moe + term
    out = _ln_rows(DEEPNORM_ALPHA * x_ref[...] + moe, g_ref[...], b_ref[...])
    of_ref[...] = out
    ob_ref[...] = out.astype(BF16)


def _combine_ln(pos_flat, ys, x, gate_t, g, b):
    n, d = x.shape
    tt = min(COMBINE_TILE, n)
    spr = d // LANES
    row = pl.BlockSpec((tt, d), lambda i, p: (i, 0))
    vec = pl.BlockSpec((1, d), lambda i, p: (0, 0))
    return pl.pallas_call(
        _combine_kernel,
        grid_spec=pltpu.PrefetchScalarGridSpec(
            num_scalar_prefetch=1,
            grid=(n // tt,),
            in_specs=[pl.BlockSpec(memory_space=pl.ANY), row,
                      pl.BlockSpec((tt, TOP_K), lambda i, p: (i, 0)), vec, vec],
            out_specs=[row, row],
            scratch_shapes=[pltpu.VMEM((2, TOP_K * tt * spr, LANES), F32), pltpu.SemaphoreType.DMA((2,))]),
        out_shape=[jax.ShapeDtypeStruct((n, d), F32), jax.ShapeDtypeStruct((n, d), BF16)],
        compiler_params=_cparams("arbitrary"),
    )(pos_flat, ys, x, gate_t, g.reshape(1, d), b.reshape(1, d))


def _moe_block(x_f32, x_lin, router_w, router_b, w1, b1, w2, b2, ln_g, ln_b):
    n, d = x_f32.shape
    tm = MOE_TILE
    top_i, gate, rank, counts = _router(x_f32, router_w, router_b)
    counts = counts[:, 0].astype(I32)
    padded = (counts + tm - 1) // tm * tm
    ends = jnp.cumsum(padded)
    starts = ends - padded
    max_tiles = (n * TOP_K) // tm + N_EXPERTS
    pos = starts[top_i] + rank
    tile_start = jnp.arange(max_tiles, dtype=I32) * tm
    tile_e = jnp.minimum(jnp.searchsorted(ends, tile_start, side='right'), N_EXPERTS - 1).astype(I32)
    n_tiles = (ends[-1] // tm).astype(I32).reshape(1)
    tok = jnp.broadcast_to(jnp.arange(n, dtype=I32)[None, :], (TOP_K, n))
    rowtok = jnp.zeros((max_tiles * tm,), I32).at[pos.reshape(-1)].set(tok.reshape(-1))
    b1g = b1[:, None, 0::2]
    b1l = b1[:, None, 1::2]
    ys = _expert_ffn(x_lin, tile_e, n_tiles, rowtok, _w1_prep(w1), b1g, b1l, w2.astype(BF16), b2[:, None, :],
                     max_tiles=max_tiles)
    return _combine_ln(pos.reshape(-1), ys, x_f32, gate.T, ln_g, ln_b)


def _rms_norm(x, g):
    xf = x.astype(F32)
    return (xf * lax.rsqrt(jnp.mean(jnp.square(xf), axis=-1, keepdims=True) + RMS_EPS) * g).astype(x.dtype)


def _l2_normalize(x):
    xf = x.astype(F32)
    return xf * lax.rsqrt(jnp.sum(jnp.square(xf), axis=-1, keepdims=True) + RMS_EPS)


def _layer_norm(x, g, b):
    xf = x.astype(F32)
    mu = jnp.mean(xf, axis=-1, keepdims=True)
    var = jnp.mean(jnp.square(xf - mu), axis=-1, keepdims=True)
    return ((xf - mu) * lax.rsqrt(var + LN_EPS) * g + b).astype(x.dtype)


def _rope_tables(T, dim):
    inv_freq = ROPE_THETA ** (-jnp.arange(0, dim, 2, dtype=F32) / dim)
    ang = jnp.arange(T, dtype=F32)[:, None] * inv_freq[None, :]
    return jnp.cos(ang), jnp.sin(ang)


def _apply_rope(x, cos, sin):
    x1, x2 = jnp.split(x.astype(F32), 2, axis=-1)
    return jnp.concatenate([x1 * cos - x2 * sin, x2 * cos + x1 * sin], axis=-1).astype(x.dtype)


def _causal_depthwise_conv(x, w):
    taps, c = w.shape
    return lax.conv_general_dilated(x, w[:, None, :].astype(x.dtype), window_strides=(1,),
                                    padding=((taps - 1, 0),), dimension_numbers=('NWC', 'WIO', 'NWC'),
                                    feature_group_count=c)


def _gated_delta_rule_chunked(q, k, v, g, beta):
    B, T, H, DK = q.shape
    DV = v.shape[-1]
    C = GDN_CHUNK
    N = T // C

    def to_chunks(a):
        a = a.astype(F32).reshape((B, N, C, H) + a.shape[3:])
        return jnp.moveaxis(a, 3, 1)

    q, k, v, g, beta = (to_chunks(a) for a in (q, k, v, g, beta))
    gc = jnp.cumsum(g, axis=-1)
    causal = jnp.tril(jnp.ones((C, C), dtype=bool))
    strict = jnp.tril(jnp.ones((C, C), dtype=bool), -1)
    diff = gc[..., :, None] - gc[..., None, :]
    decay = jnp.where(causal, jnp.exp(jnp.where(causal, diff, 0.0)), 0.0)
    kb = k * beta[..., None]
    lower = jnp.where(strict, jnp.einsum('bhnid,bhnjd->bhnij', kb, k) * decay, 0.0)
    eye = jnp.broadcast_to(jnp.eye(C, dtype=F32), lower.shape)
    tinv = lax.linalg.triangular_solve(eye + lower, eye, left_side=True, lower=True, unit_diagonal=True)
    u = tinv @ (v * beta[..., None])
    w = tinv @ (kb * jnp.exp(gc)[..., None])
    a_intra = jnp.where(causal, jnp.einsum('bhnid,bhnjd->bhnij', q, k) * decay, 0.0)
    q_dec = q * jnp.exp(gc)[..., None]
    k_dec = k * jnp.exp(gc[..., -1:] - gc)[..., None]
    g_last = jnp.exp(gc[..., -1])

    def step(state, xs):
        qd, kd, u_i, w_i, a_i, gl = xs
        v_new = u_i - w_i @ state
        o = qd @ state + a_i @ v_new
        state = state * gl[..., None, None] + jnp.einsum('bhcd,bhce->bhde', kd, v_new)
        return state, o

    xs = tuple(jnp.moveaxis(a, 2, 0) for a in (q_dec, k_dec, u, w, a_intra, g_last))
    _, o = lax.scan(step, jnp.zeros((B, H, DK, DV), F32), xs)
    return jnp.transpose(o, (1, 0, 3, 2, 4)).reshape(B, T, H, DV)


def _gdn_mixer(h, conv_w, a_log, dt_bias, norm_g):
    B, T, _ = h.shape
    qkv, z, b, a = jnp.split(h, [GDN_CONV_DIM, GDN_CONV_DIM + GDN_VAL_DIM,
                                 GDN_CONV_DIM + GDN_VAL_DIM + GDN_V_HEADS], axis=-1)
    qkv = jax.nn.silu(_causal_depthwise_conv(qkv, conv_w))
    q, k, v = jnp.split(qkv, [GDN_KEY_DIM, 2 * GDN_KEY_DIM], axis=-1)
    rep = GDN_V_HEADS // GDN_K_HEADS
    q = jnp.repeat(q.reshape(B, T, GDN_K_HEADS, GDN_HEAD_DIM), rep, axis=2)
    k = jnp.repeat(k.reshape(B, T, GDN_K_HEADS, GDN_HEAD_DIM), rep, axis=2)
    v = v.reshape(B, T, GDN_V_HEADS, GDN_HEAD_DIM)
    q = _l2_normalize(q) * GDN_HEAD_DIM ** -0.5
    k = _l2_normalize(k)
    beta = jax.nn.sigmoid(b.astype(F32))
    g = -jnp.exp(a_log.astype(F32)) * jax.nn.softplus(a.astype(F32) + dt_bias.astype(F32))
    o = _gated_delta_rule_chunked(q, k, v, g, beta)
    o = _rms_norm(o, norm_g) * jax.nn.silu(z.reshape(B, T, GDN_V_HEADS, GDN_HEAD_DIM).astype(F32))
    return o.reshape(B, T, GDN_VAL_DIM).astype(h.dtype)


def _dilated_window_attention(q, k, v, window, dilation):
    B, T, H, E = q.shape
    d = dilation
    w = window // dilation
    ls = T // d
    nb = -(-ls // w)
    pad = nb * w - ls

    def to_sub(a):
        a = jnp.swapaxes(a.reshape(B, ls, d, H, E), 1, 2)
        return jnp.pad(a, ((0, 0), (0, 0), (0, pad), (0, 0), (0, 0)))

    def band(a):
        a = jnp.pad(a, ((0, 0), (0, 0), (w, 0), (0, 0), (0, 0))).reshape(B, d, nb + 1, w, H, E)
        return jnp.concatenate([a[:, :, :-1], a[:, :, 1:]], axis=3)

    qs = to_sub(q).reshape(B, d, nb, w, H, E)
    kb = band(to_sub(k))
    vb = band(to_sub(v))
    s = jnp.einsum('bdnqhe,bdnkhe->bdnhqk', qs, kb).astype(F32) * E ** -0.5
    qi = jnp.arange(w)[:, None]
    kj = jnp.arange(2 * w)[None, :]
    dist = w + qi - kj
    key_sub = jnp.arange(nb)[:, None, None] * w + kj[None] - w
    valid = (dist >= 0) & (dist <= w) & (key_sub >= 0)
    s = jnp.where(valid[:, None], s, -jnp.inf)
    m = jnp.max(s, axis=-1, keepdims=True)
    pexp = jnp.exp(s - m)
    l = jnp.sum(pexp, axis=-1, keepdims=True)
    o = jnp.einsum('bdnhqk,bdnkhe->bdnqhe', pexp, vb.astype(F32)) / jnp.swapaxes(l, 3, 4)
    lse = jnp.swapaxes((m + jnp.log(l))[..., 0], 3, 4)

    def from_sub(a):
        a = a.reshape((B, d, nb * w) + a.shape[4:])[:, :, :ls]
        return jnp.swapaxes(a, 1, 2).reshape((B, T) + a.shape[3:])

    return from_sub(o), from_sub(lse)


def _dilated_mixer(h):
    B, T, _ = h.shape
    qkv = h.reshape(B, T, len(DIL_GROUPS), 3, DIL_HEADS, DIL_HEAD_DIM)
    outs, lses = [], []
    for gi, (window, dilation) in enumerate(DIL_GROUPS):
        o, lse = _dilated_window_attention(qkv[:, :, gi, 0], qkv[:, :, gi, 1], qkv[:, :, gi, 2], window, dilation)
        outs.append(o)
        lses.append(lse)
    wts = jax.nn.softmax(jnp.stack(lses, axis=0), axis=0)
    o = jnp.sum(wts[..., None] * jnp.stack(outs, axis=0), axis=0)
    return o.reshape(B, T, DIL_OUT).astype(h.dtype)


def _dsa_mixer(h, q_norm_g, kv_norm_g, w_q_up, w_idx_q, idx_k_norm_g, idx_k_norm_b, w_kv_up):
    B, T, _ = h.shape
    o1 = DSA_Q_LORA
    o2 = o1 + DSA_KV_LORA
    o3 = o2 + DSA_ROPE_DIM
    o4 = o3 + IDX_DIM
    q_lat, kv_lat, k_rope, idx_k, idx_w = jnp.split(h, [o1, o2, o3, o4], axis=-1)
    cos, sin = _rope_tables(T, DSA_ROPE_DIM)
    cq = _rms_norm(q_lat, q_norm_g)
    q = (cq @ w_q_up).reshape(B, T, DSA_HEADS, DSA_NOPE_DIM + DSA_ROPE_DIM)
    q_nope = q[..., :DSA_NOPE_DIM]
    q_rope = _apply_rope(q[..., DSA_NOPE_DIM:], cos[:, None], sin[:, None])
    c_kv = _rms_norm(kv_lat, kv_norm_g)
    k_rope = _apply_rope(k_rope, cos, sin)
    w_kv = w_kv_up.reshape(DSA_KV_LORA, DSA_HEADS, DSA_NOPE_DIM + DSA_V_DIM)
    w_uk = w_kv[..., :DSA_NOPE_DIM]
    w_uv = w_kv[..., DSA_NOPE_DIM:]
    iq = (cq @ w_idx_q).reshape(B, T, IDX_HEADS, IDX_DIM)
    iq = jnp.concatenate([_apply_rope(iq[..., :IDX_ROPE_DIM], cos[:, None], sin[:, None]),
                          iq[..., IDX_ROPE_DIM:]], axis=-1).astype(F32)
    ik = _layer_norm(idx_k, idx_k_norm_g, idx_k_norm_b)
    ik = jnp.concatenate([_apply_rope(ik[..., :IDX_ROPE_DIM], cos, sin), ik[..., IDX_ROPE_DIM:]],
                         axis=-1).astype(F32)
    iw = idx_w.astype(F32) * (IDX_HEADS ** -0.5 * IDX_DIM ** -0.5)
    topk = min(DSA_TOPK_MAX, T // 4)
    scale = (DSA_NOPE_DIM + DSA_ROPE_DIM) ** -0.5
    key_pos = jnp.arange(T)
    gather = jax.vmap(lambda table, idx: table[idx])
    qb = 128

    def attend_block(args):
        qn, qr, iqb, iwb, qpos = args
        isc = jax.nn.relu(jnp.einsum('bqhd,bsd->bqhs', iqb, ik))
        isc = jnp.einsum('bqhs,bqh->bqs', isc, iwb)
        isc = jnp.where(key_pos[None, None, :] <= qpos[None, :, None], isc, -jnp.inf)
        _, sel = lax.top_k(isc, topk)
        c_sel = gather(c_kv, sel)
        r_sel = gather(k_rope, sel)
        q_abs = jnp.einsum('bqhd,chd->bqhc', qn, w_uk)
        s = (jnp.einsum('bqhc,bqkc->bqhk', q_abs, c_sel)
             + jnp.einsum('bqhr,bqkr->bqhk', qr, r_sel)).astype(F32) * scale
        s = jnp.where((sel <= qpos[None, :, None])[:, :, None, :], s, -jnp.inf)
        p = jax.nn.softmax(s, axis=-1).astype(c_sel.dtype)
        o_lat = jnp.einsum('bqhk,bqkc->bqhc', p, c_sel)
        return jnp.einsum('bqhc,chd->bqhd', o_lat, w_uv)

    nqb = T // qb

    def blocks(a):
        return jnp.moveaxis(a.reshape((B, nqb, qb) + a.shape[2:]), 1, 0)

    qpos = jnp.arange(T).reshape(nqb, qb)
    out = lax.map(attend_block, (blocks(q_nope), blocks(q_rope), blocks(iq), blocks(iw), qpos))
    return jnp.moveaxis(out, 0, 1).reshape(B, T, DSA_HEADS * DSA_V_DIM).astype(h.dtype)


DIL_BAND = 128
DIL_HEADS_PER_STEP = 4
DIL_MASKED = -1e30


def _dil_attn_kernel(q_ref, kp_ref, ko_ref, vp_ref, vo_ref, o_ref, lse_ref):
    nblk = pl.program_id(2)
    w = DIL_BAND
    e = DIL_HEAD_DIM
    scale = e ** -0.5
    qi = lax.broadcasted_iota(I32, (w, w), 0)
    kj = lax.broadcasted_iota(I32, (w, w), 1)
    own_ok = kj <= qi
    prev_ok = (kj >= qi) & (nblk > 0)
    contract_last = (((1,), (1,)), ((), ()))
    for hh in range(DIL_HEADS_PER_STEP):
        cols = slice(hh * e, (hh + 1) * e)
        q = q_ref[0, :, cols]
        s_own = lax.dot_general(q, ko_ref[0, :, cols], contract_last, preferred_element_type=F32) * scale
        s_prev = lax.dot_general(q, kp_ref[0, :, cols], contract_last, preferred_element_type=F32) * scale
        s_own = jnp.where(own_ok, s_own, DIL_MASKED)
        s_prev = jnp.where(prev_ok, s_prev, DIL_MASKED)
        m = jnp.maximum(jnp.max(s_own, axis=1, keepdims=True), jnp.max(s_prev, axis=1, keepdims=True))
        p_own = jnp.exp(s_own - m)
        p_prev = jnp.exp(s_prev - m)
        l = jnp.sum(p_own, axis=1, keepdims=True) + jnp.sum(p_prev, axis=1, keepdims=True)
        o = jnp.dot(p_own.astype(BF16), vo_ref[0, :, cols], preferred_element_type=F32)
        o = o + jnp.dot(p_prev.astype(BF16), vp_ref[0, :, cols], preferred_element_type=F32)
        o_ref[0, :, cols] = o / l
        lse_ref[0, :, cols] = jnp.broadcast_to(m + jnp.log(l), (w, e))


def _dil_group_attention(h, gi, dilation, *, batch):
    n, wid = h.shape
    seq = n // batch
    d = dilation
    ls = seq // d
    nb = ls // DIL_BAND
    hw = DIL_HEADS_PER_STEP * DIL_HEAD_DIM
    hsteps = DIL_OUT // hw
    wb = wid // hw
    assert wid % hw == 0 and ls % DIL_BAND == 0
    hv = h.reshape(batch, ls, d * wid)
    base = lambda j: (gi * 3 + j) * hsteps
    own = lambda j: pl.BlockSpec((1, DIL_BAND, hw), lambda b, r, s, c: (b, s, r * wb + base(j) + c))
    prev = lambda j: pl.BlockSpec((1, DIL_BAND, hw),
                                  lambda b, r, s, c: (b, jnp.maximum(s - 1, 0), r * wb + base(j) + c))
    out = pl.BlockSpec((1, DIL_BAND, hw), lambda b, r, s, c: (b, s, r * hsteps + c))
    o, lse = pl.pallas_call(
        _dil_attn_kernel,
        grid=(batch, d, nb, hsteps),
        in_specs=[own(0), prev(1), own(1), prev(2), own(2)],
        out_specs=[out, out],
        out_shape=[jax.ShapeDtypeStruct((batch, ls, d * DIL_OUT), F32)] * 2,
        compiler_params=_cparams("parallel", "parallel", "parallel", "parallel"),
    )(hv, hv, hv, hv, hv)
    return o.reshape(n, DIL_OUT), lse.reshape(n, DIL_OUT)


def _dil_combine_kernel(o0, o1, o2, l0, l1, l2, y_ref):
    a, b, c = l0[...], l1[...], l2[...]
    m = jnp.maximum(jnp.maximum(a, b), c)
    ea, eb, ec = jnp.exp(a - m), jnp.exp(b - m), jnp.exp(c - m)
    y_ref[...] = ((ea * o0[...] + eb * o1[...] + ec * o2[...]) / (ea + eb + ec)).astype(y_ref.dtype)


def _dilated_mixer_pallas(h, *, batch):
    n = h.shape[0]
    outs, lses = [], []
    for gi, (window, dilation) in enumerate(DIL_GROUPS):
        assert window // dilation == DIL_BAND
        o, lse = _dil_group_attention(h, gi, dilation, batch=batch)
        outs.append(o)
        lses.append(lse)
    tt = min(512, n)
    blk = pl.BlockSpec((tt, 512), lambda i, c: (i, c))
    return pl.pallas_call(
        _dil_combine_kernel,
        grid=(n // tt, DIL_OUT // 512),
        in_specs=[blk] * 6,
        out_specs=blk,
        out_shape=jax.ShapeDtypeStruct((n, DIL_OUT), BF16),
        compiler_params=_cparams("parallel", "parallel"),
    )(*outs, *lses)


GDN_HEAD_GROUP = 4
GDN_CHUNK_GROUP = 4
GDN_PREP_COLS = 1024
SUBLANES = 8


def _gdn_prep_kernel(x_ref, halo_ref, w_ref, o_ref, *, tiles_per_seq):
    i = pl.program_id(0)
    c = pl.program_id(1)
    x = x_ref[...]
    tt = x.shape[0]
    halo = jnp.where(i % tiles_per_seq == 0, 0.0, halo_ref[...])
    row8 = lax.broadcasted_iota(I32, halo.shape, 0)
    w = w_ref[...]
    y = x * w[GDN_CONV_TAPS - 1:GDN_CONV_TAPS, :]
    for s in range(1, GDN_CONV_TAPS):
        xs = pltpu.roll(x, s, axis=0)
        first = jnp.where(row8 < s, pltpu.roll(halo, s, axis=0), xs[:SUBLANES])
        xs = jnp.concatenate([first, xs[SUBLANES:]], axis=0)
        y = y + xs * w[GDN_CONV_TAPS - 1 - s:GDN_CONV_TAPS - s, :]
    y = y * jax.nn.sigmoid(y)
    n_qk_tiles = 2 * GDN_KEY_DIM // GDN_PREP_COLS

    @pl.when(c >= n_qk_tiles)
    def _():
        o_ref[...] = y

    @pl.when(c < n_qk_tiles)
    def _():
        qscale = jnp.where(c < n_qk_tiles // 2, GDN_HEAD_DIM ** -0.5, 1.0)
        for g in range(GDN_PREP_COLS // GDN_HEAD_DIM):
            ys = y[:, g * GDN_HEAD_DIM:(g + 1) * GDN_HEAD_DIM]
            inv = lax.rsqrt(jnp.sum(ys * ys, axis=-1, keepdims=True) + RMS_EPS)
            o_ref[:, g * GDN_HEAD_DIM:(g + 1) * GDN_HEAD_DIM] = ys * inv * qscale


def _gdn_prep(h_main, conv_w, *, seq):
    n = h_main.shape[0]
    tt = min(256, seq)
    cw = GDN_PREP_COLS
    return pl.pallas_call(
        functools.partial(_gdn_prep_kernel, tiles_per_seq=seq // tt),
        grid=(n // tt, GDN_CONV_DIM // cw),
        in_specs=[pl.BlockSpec((tt, cw), lambda i, c: (i, c)),
                  pl.BlockSpec((SUBLANES, cw), lambda i, c: (jnp.maximum(i * (tt // SUBLANES) - 1, 0), c)),
                  pl.BlockSpec((GDN_CONV_TAPS, cw), lambda i, c: (0, c))],
        out_specs=pl.BlockSpec((tt, cw), lambda i, c: (i, c)),
        out_shape=jax.ShapeDtypeStruct((n, GDN_CONV_DIM), F32),
        compiler_params=_cparams("parallel", "parallel"),
    )(h_main, h_main, conv_w)


def _gdn_gate_kernel(ba_ref, alog_ref, dtb_ref, beta_ref, g_ref):
    ba = ba_ref[...]
    beta = jax.nn.sigmoid(ba)
    xa = ba + dtb_ref[...]
    softplus = jnp.maximum(xa, 0.0) + jnp.log(1.0 + jnp.exp(-jnp.abs(xa)))
    g = -jnp.exp(alog_ref[...]) * softplus
    hg = GDN_HEAD_GROUP
    for j in range(GDN_V_HEADS // hg):
        beta_ref[j] = beta[:, j * hg:(j + 1) * hg]
        g_ref[j] = g[:, GDN_V_HEADS + j * hg:GDN_V_HEADS + (j + 1) * hg]


def _gdn_gates(h_tail, a_log, dt_bias):
    n = h_tail.shape[0]
    tt = min(512, n)
    ng = GDN_V_HEADS // GDN_HEAD_GROUP
    lane_vec = lambda v: jnp.zeros((1, LANES), F32).at[0, GDN_V_HEADS:2 * GDN_V_HEADS].set(v)
    out = pl.BlockSpec((ng, tt, GDN_HEAD_GROUP), lambda i: (0, i, 0))
    return pl.pallas_call(
        _gdn_gate_kernel,
        grid=(n // tt,),
        in_specs=[pl.BlockSpec((tt, LANES), lambda i: (i, MEM_Q // LANES)),
                  pl.BlockSpec((1, LANES), lambda i: (0, 0)), pl.BlockSpec((1, LANES), lambda i: (0, 0))],
        out_specs=[out, out],
        out_shape=[jax.ShapeDtypeStruct((ng, n, GDN_HEAD_GROUP), F32)] * 2,
        compiler_params=_cparams("parallel"),
    )(h_tail, lane_vec(a_log), lane_vec(dt_bias))


def _dot_tril(ones_b, x):
    acc = None
    for _ in range(3):
        x_b = x.astype(BF16)
        d = jnp.dot(ones_b, x_b, preferred_element_type=F32)
        acc = d if acc is None else acc + d
        x = x - x_b.astype(F32)
    return acc


def _gdn_scan_kernel(q_ref, k_ref, v_ref, z_ref, beta_ref, g_ref, ng_ref, o_ref, state_ref):
    C = GDN_CHUNK
    hd = GDN_HEAD_DIM

    @pl.when(pl.program_id(2) == 0)
    def _():
        state_ref[...] = jnp.zeros_like(state_ref)

    r_i = lax.broadcasted_iota(I32, (C, C), 0)
    c_i = lax.broadcasted_iota(I32, (C, C), 1)
    causal = c_i <= r_i
    strict = c_i < r_i
    tril = causal.astype(F32)
    later = (r_i > c_i).astype(F32)
    eye = (r_i == c_i).astype(F32)
    contract_last = (((1,), (1,)), ((), ()))
    norm_g = ng_ref[...]

    tril_b = tril.astype(BF16)
    chunk_heads = [(c, j) for c in range(GDN_CHUNK_GROUP) for j in range(GDN_HEAD_GROUP)]
    rows = lambda c: slice(c * C, (c + 1) * C)
    hcols = lambda j: slice(j * hd, (j + 1) * hd)

    gc_all = [_dot_tril(tril_b, g_ref[0, rows(c), :]) for c in range(GDN_CHUNK_GROUP)]
    kk, qk = {}, {}
    for c in range(GDN_CHUNK_GROUP):
        for kh in range(GDN_HEAD_GROUP // 2):
            k_b = k_ref[rows(c), hcols(kh)].astype(BF16)
            q_b = q_ref[rows(c), hcols(kh)].astype(BF16)
            kk[c, kh] = lax.dot_general(k_b, k_b, contract_last, preferred_element_type=F32)
            qk[c, kh] = lax.dot_general(q_b, k_b, contract_last, preferred_element_type=F32)
    decay, mpow, tinv = {}, {}, {}
    for c, j in chunk_heads:
        diff = _dot_tril(tril_b, g_ref[0, rows(c), j:j + 1] * later)
        decay[c, j] = jnp.where(causal, jnp.exp(jnp.where(causal, diff, 0.0)), 0.0)
        lower = jnp.where(strict, beta_ref[0, rows(c), j:j + 1] * kk[c, j // 2] * decay[c, j], 0.0)
        mpow[c, j] = -lower
        tinv[c, j] = eye - lower
    for _ in range(5):
        for cj in chunk_heads:
            m_b = mpow[cj].astype(BF16)
            mpow[cj] = jnp.dot(m_b, m_b, preferred_element_type=F32)
        for cj in chunk_heads:
            tinv[cj] = tinv[cj] + jnp.dot(tinv[cj].astype(BF16), mpow[cj].astype(BF16), preferred_element_type=F32)
    u, w, a_intra, q_dec, k_dec_t, g_last = {}, {}, {}, {}, {}, {}
    for c, j in chunk_heads:
        k_h = k_ref[rows(c), hcols(j // 2)]
        beta = beta_ref[0, rows(c), j:j + 1]
        gc = gc_all[c][:, j:j + 1]
        egc = jnp.exp(gc)
        gc_last = gc[C - 1:C, :]
        tinv_b = tinv[c, j].astype(BF16)
        u[c, j] = jnp.dot(tinv_b, (v_ref[rows(c), hcols(j)] * beta).astype(BF16), preferred_element_type=F32)
        w[c, j] = jnp.dot(tinv_b, (k_h * (beta * egc)).astype(BF16), preferred_element_type=F32).astype(BF16)
        a_intra[c, j] = jnp.where(causal, qk[c, j // 2] * decay[c, j], 0.0).astype(BF16)
        q_dec[c, j] = (q_ref[rows(c), hcols(j // 2)] * egc).astype(BF16)
        k_dec_t[c, j] = (k_h * jnp.exp(gc_last - gc)).T.astype(BF16)
        g_last[c, j] = jnp.exp(gc_last)

    heads = range(GDN_HEAD_GROUP)
    state = [state_ref[j] for j in heads]
    for c in range(GDN_CHUNK_GROUP):
        state_b = [state[j].astype(BF16) for j in heads]
        v_new = [u[c, j] - jnp.dot(w[c, j], state_b[j], preferred_element_type=F32) for j in heads]
        v_new_b = [v.astype(BF16) for v in v_new]
        state = [state[j] * g_last[c, j] + jnp.dot(k_dec_t[c, j], v_new_b[j], preferred_element_type=F32)
                 for j in heads]
        for j in heads:
            o = jnp.dot(q_dec[c, j], state_b[j], preferred_element_type=F32) + jnp.dot(
                a_intra[c, j], v_new_b[j], preferred_element_type=F32)
            o = o * lax.rsqrt(jnp.mean(o * o, axis=-1, keepdims=True) + RMS_EPS) * norm_g
            z = z_ref[rows(c), hcols(j)]
            o_ref[rows(c), hcols(j)] = (o * (z * jax.nn.sigmoid(z))).astype(o_ref.dtype)
    for j in heads:
        state_ref[j] = state[j]


def _gdn_scan(qkv, h_main, beta, g, norm_g, *, batch):
    n = qkv.shape[0]
    seq = n // batch
    tr = GDN_CHUNK_GROUP * GDN_CHUNK
    nt = seq // tr
    hg = GDN_HEAD_GROUP
    kw = (hg // 2) * GDN_HEAD_DIM
    vw = hg * GDN_HEAD_DIM
    row = lambda b, h, t: b * nt + t
    return pl.pallas_call(
        _gdn_scan_kernel,
        grid=(batch, GDN_V_HEADS // hg, nt),
        in_specs=[pl.BlockSpec((tr, kw), lambda b, h, t: (row(b, h, t), h)),
                  pl.BlockSpec((tr, kw), lambda b, h, t: (row(b, h, t), GDN_KEY_DIM // kw + h)),
                  pl.BlockSpec((tr, vw), lambda b, h, t: (row(b, h, t), 2 * GDN_KEY_DIM // vw + h)),
                  pl.BlockSpec((tr, vw), lambda b, h, t: (row(b, h, t), GDN_CONV_DIM // vw + h)),
                  pl.BlockSpec((1, tr, hg), lambda b, h, t: (h, row(b, h, t), 0)),
                  pl.BlockSpec((1, tr, hg), lambda b, h, t: (h, row(b, h, t), 0)),
                  pl.BlockSpec((1, GDN_HEAD_DIM), lambda b, h, t: (0, 0))],
        out_specs=pl.BlockSpec((tr, vw), lambda b, h, t: (row(b, h, t), h)),
        out_shape=jax.ShapeDtypeStruct((n, GDN_VAL_DIM), BF16),
        scratch_shapes=[pltpu.VMEM((hg, GDN_HEAD_DIM, GDN_HEAD_DIM), F32)],
        compiler_params=_cparams("parallel", "parallel", "arbitrary"),
    )(qkv, qkv, qkv, h_main, beta, g, norm_g.reshape(1, GDN_HEAD_DIM))


def _gdn_in_weights(w_in):
    n_main = GDN_CONV_DIM + GDN_VAL_DIM
    n_ba = 2 * GDN_V_HEADS
    pad = jnp.zeros((w_in.shape[0], LANES - n_ba), w_in.dtype)
    tail = jnp.concatenate([w_in[:, n_main + n_ba:], w_in[:, n_main:n_main + n_ba], pad], axis=1)
    return w_in[:, :n_main], tail


def _gdn_mixer_pallas(h_main, h_tail, mix_params, *, batch):
    conv_w, a_log, dt_bias, norm_g = mix_params
    seq = h_main.shape[0] // batch
    qkv = _gdn_prep(h_main, conv_w, seq=seq)
    beta, g = _gdn_gates(h_tail, a_log, dt_bias)
    return _gdn_scan(qkv, h_main, beta, g, norm_g, batch=batch)


DSA_TQ = 128
DSA_TK = 512
DSA_HEAD_GROUP = 8
DSA_KEY_W = DSA_KV_LORA + LANES
MASK_BIAS = -2e30
M_INIT = -1e30
I32_MIN = -2 ** 31


def _rope_partner(x, half):
    ax = x.ndim - 1
    n = x.shape[ax]
    lane = lax.broadcasted_iota(I32, x.shape, ax) % (2 * half)
    return jnp.where(lane < half, -pltpu.roll(x, n - half, axis=ax), pltpu.roll(x, half, axis=ax))


def _dsa_prep_kernel(ql_ref, kv_ref, ik_ref, misc_ref, cos_ref, sin_ref, cosi_ref, sini_ref,
                     qg_ref, kvg_ref, ikg_ref, ikb_ref,
                     cq_ref, key_ref, iko_ref, iw_ref):
    half = DSA_ROPE_DIM // 2
    ql = ql_ref[...]
    cq_ref[...] = (ql * lax.rsqrt(jnp.mean(ql * ql, axis=-1, keepdims=True) + RMS_EPS) * qg_ref[...]).astype(BF16)
    kv = kv_ref[...]
    key_ref[:, :DSA_KV_LORA] = (
        kv * lax.rsqrt(jnp.mean(kv * kv, axis=-1, keepdims=True) + RMS_EPS) * kvg_ref[...]).astype(BF16)
    misc = misc_ref[...]
    lane = lax.broadcasted_iota(I32, misc.shape, 1)
    kr = jnp.where(lane < DSA_ROPE_DIM, misc * cos_ref[...] + _rope_partner(misc, half) * sin_ref[...], 0.0)
    key_ref[:, DSA_KV_LORA:] = kr.astype(BF16)
    iw = misc * (IDX_HEADS ** -0.5 * IDX_DIM ** -0.5)
    for h in range(IDX_HEADS):
        iw_ref[h] = iw[:, DSA_ROPE_DIM + h:DSA_ROPE_DIM + h + 1]
    ik = _ln_rows(ik_ref[...], ikg_ref[...], ikb_ref[...])
    iko_ref[...] = (ik * cosi_ref[...] + _rope_partner(ik, half) * sini_ref[...]).astype(BF16)


def _dsa_prep(h, tabs, q_norm_g, kv_norm_g, ik_g, ik_b, *, seq):
    n = h.shape[0]
    tq = min(256, seq)
    nt = seq // tq
    cos4, sin4, cosi, sini = tabs
    tab = pl.BlockSpec((tq, LANES), lambda i: (i % nt, 0))
    vec = lambda w: pl.BlockSpec((1, w), lambda i: (0, 0))
    blk = lambda w, j: pl.BlockSpec((tq, w), lambda i: (i, j))
    return pl.pallas_call(
        _dsa_prep_kernel,
        grid=(n // tq,),
        in_specs=[blk(DSA_Q_LORA, 0), blk(DSA_KV_LORA, 3), blk(LANES, 16), blk(LANES, 17), tab, tab, tab, tab,
                  vec(DSA_Q_LORA), vec(DSA_KV_LORA), vec(IDX_DIM), vec(IDX_DIM)],
        out_specs=[blk(DSA_Q_LORA, 0), blk(DSA_KEY_W, 0), blk(IDX_DIM, 0),
                   pl.BlockSpec((IDX_HEADS, tq, 1), lambda i: (0, i, 0))],
        out_shape=[jax.ShapeDtypeStruct((n, DSA_Q_LORA), BF16), jax.ShapeDtypeStruct((n, DSA_KEY_W), BF16),
                   jax.ShapeDtypeStruct((n, IDX_DIM), BF16), jax.ShapeDtypeStruct((IDX_HEADS, n, 1), F32)],
        compiler_params=_cparams("parallel"),
    )(h, h, h, h, cos4, sin4, cosi, sini, q_norm_g.reshape(1, -1), kv_norm_g.reshape(1, -1),
      ik_g.reshape(1, -1), ik_b.reshape(1, -1))


def _dsa_qpost_kernel(qn_ref, qri_ref, wuk_ref, cos_ref, sin_ref, cosi_ref, sini_ref, q_ref, iq_ref):
    half = DSA_ROPE_DIM // 2
    tq = qn_ref.shape[0]
    n_rope = DSA_HEADS * DSA_ROPE_DIM
    qr = qri_ref[:, :n_rope]
    cos = jnp.tile(cos_ref[...], (1, n_rope // LANES))
    sin = jnp.tile(sin_ref[...], (1, n_rope // LANES))
    qr = qr * cos + _rope_partner(qr, half) * sin
    zeros = jnp.zeros((tq, LANES - DSA_ROPE_DIM), F32)
    cosi, sini = cosi_ref[...], sini_ref[...]
    for h in range(DSA_HEADS):
        q_ref[h, :, :DSA_KV_LORA] = jnp.dot(
            qn_ref[:, h * DSA_NOPE_DIM:(h + 1) * DSA_NOPE_DIM], wuk_ref[h], preferred_element_type=F32).astype(BF16)
        q_ref[h, :, DSA_KV_LORA:] = jnp.concatenate(
            [qr[:, h * DSA_ROPE_DIM:(h + 1) * DSA_ROPE_DIM], zeros], axis=1).astype(BF16)
        iq = qri_ref[:, n_rope + h * IDX_DIM:n_rope + (h + 1) * IDX_DIM]
        iq_ref[h] = (iq * cosi + _rope_partner(iq, half) * sini).astype(BF16)


def _dsa_qpost(qn, qri, wuk_t, tabs, *, seq):
    n = qn.shape[0]
    tq = min(128, seq)
    nt = seq // tq
    cos4, sin4, cosi, sini = tabs
    tab = pl.BlockSpec((tq, LANES), lambda i: (i % nt, 0))
    row = lambda w: pl.BlockSpec((tq, w), lambda i: (i, 0))
    hm = lambda w: pl.BlockSpec((DSA_HEADS, tq, w), lambda i: (0, i, 0))
    return pl.pallas_call(
        _dsa_qpost_kernel,
        grid=(n // tq,),
        in_specs=[row(qn.shape[1]), row(qri.shape[1]),
                  pl.BlockSpec(wuk_t.shape, lambda i: (0, 0, 0)), tab, tab, tab, tab],
        out_specs=[hm(DSA_KEY_W), hm(IDX_DIM)],
        out_shape=[jax.ShapeDtypeStruct((DSA_HEADS, n, DSA_KEY_W), BF16),
                   jax.ShapeDtypeStruct((IDX_HEADS, n, IDX_DIM), BF16)],
        compiler_params=_cparams("parallel"),
    )(qn, qri, wuk_t, cos4, sin4, cosi, sini)


def _dsa_index_kernel(iq_ref, iw_ref, ik_ref, bias_ref, key_ref, *, topk):
    qi = pl.program_id(1)
    tq = iq_ref.shape[1]
    tk = DSA_TK
    hg = DSA_HEAD_GROUP
    n_kt = ((qi + 1) * tq + tk - 1) // tk
    qpos = qi * tq + lax.broadcasted_iota(I32, (tq, tk), 0)
    lane = lax.broadcasted_iota(I32, (tq, tk), 1)

    def score_tile(kt, c):
        k0 = pl.multiple_of(kt * tk, tk)
        keys = ik_ref[pl.ds(k0, tk), :]
        acc = jnp.zeros((tq, tk), F32)
        for g in range(IDX_HEADS // hg):
            lhs = iq_ref[g * hg:(g + 1) * hg].reshape(hg * tq, IDX_DIM)
            s = lax.dot_general(lhs, keys, (((1,), (1,)), ((), ())), preferred_element_type=F32)
            s = jnp.maximum(s, 0.0).reshape(hg, tq, tk) * iw_ref[g * hg:(g + 1) * hg]
            acc = acc + jnp.sum(s, axis=0)
        acc = jnp.where(k0 + lane <= qpos, acc, -jnp.inf)
        bits = pltpu.bitcast(acc, I32)
        key_ref[:, pl.ds(k0, tk)] = jnp.where(bits < 0, bits ^ 0x7FFFFFFF, bits)
        return c

    lax.fori_loop(0, n_kt, score_tile, 0)

    res = jnp.zeros((tq, 1), I32)
    for bit in range(31, -1, -1):
        bit_c = jnp.int32(I32_MIN if bit == 31 else (1 << bit))
        cand = (res | bit_c) ^ jnp.int32(I32_MIN)

        def count_tile(kt, cnt, cand=cand):
            blk = key_ref[:, pl.ds(pl.multiple_of(kt * tk, tk), tk)]
            ge = (blk >= cand).astype(I32)
            part = ge[:, 0:LANES]
            for j in range(1, tk // LANES):
                part = part + ge[:, j * LANES:(j + 1) * LANES]
            return cnt + part

        cnt = lax.fori_loop(0, n_kt, count_tile, jnp.zeros((tq, LANES), I32))
        total = jnp.sum(cnt, axis=1, keepdims=True)
        res = jnp.where(total >= topk, res | bit_c, res)
    thr = res ^ jnp.int32(I32_MIN)

    bias_ref[...] = jnp.full(bias_ref.shape, MASK_BIAS, bias_ref.dtype)

    def bias_tile(kt, c):
        k0 = pl.multiple_of(kt * tk, tk)
        sel = (key_ref[:, pl.ds(k0, tk)] >= thr) & (k0 + lane <= qpos)
        bias_ref[:, pl.ds(k0, tk)] = jnp.where(sel, 0.0, MASK_BIAS).astype(bias_ref.dtype)
        return c

    lax.fori_loop(0, n_kt, bias_tile, 0)


def _dsa_index(iq, iw, ik, *, batch, topk):
    n = iq.shape[1]
    seq = n // batch
    tq = min(DSA_TQ, seq)
    nq = seq // tq
    return pl.pallas_call(
        functools.partial(_dsa_index_kernel, topk=topk),
        grid=(batch, nq),
        in_specs=[pl.BlockSpec((IDX_HEADS, tq, IDX_DIM), lambda b, i: (0, b * nq + i, 0)),
                  pl.BlockSpec((IDX_HEADS, tq, 1), lambda b, i: (0, b * nq + i, 0)),
                  pl.BlockSpec((seq, IDX_DIM), lambda b, i: (b, 0))],
        out_specs=pl.BlockSpec((tq, seq), lambda b, i: (b * nq + i, 0)),
        out_shape=jax.ShapeDtypeStruct((n, seq), BF16),
        scratch_shapes=[pltpu.VMEM((tq, seq), I32)],
        compiler_params=_cparams("parallel", "parallel"),
    )(iq, iw, ik)


def _dsa_attn_kernel(q_ref, key_ref, bias_ref, wuv_ref, o_ref, m_ref, l_ref, acc_ref):
    qi = pl.program_id(1)
    kt = pl.program_id(2)
    tq = q_ref.shape[1]
    tk = key_ref.shape[0]
    hg = DSA_HEAD_GROUP
    rows = hg * tq
    last = ((qi + 1) * tq - 1) // tk
    scale = (DSA_NOPE_DIM + DSA_ROPE_DIM) ** -0.5

    @pl.when(kt == 0)
    def _():
        m_ref[...] = jnp.full(m_ref.shape, M_INIT, F32)
        l_ref[...] = jnp.zeros_like(l_ref)
        acc_ref[...] = jnp.zeros_like(acc_ref)

    @pl.when(kt <= last)
    def _():
        bias = bias_ref[...].astype(F32)
        keys = key_ref[...]
        vals = key_ref[:, :DSA_KV_LORA]
        for g in range(DSA_HEADS // hg):
            r = slice(g * rows, (g + 1) * rows)
            lhs = q_ref[g * hg:(g + 1) * hg].reshape(rows, DSA_KEY_W)
            s = lax.dot_general(lhs, keys, (((1,), (1,)), ((), ())), preferred_element_type=F32) * scale
            s = (s.reshape(hg, tq, tk) + bias).reshape(rows, tk)
            m_prev = m_ref[r]
            m_cur = jnp.maximum(m_prev, jnp.max(s, axis=1, keepdims=True))
            alpha = jnp.exp(m_prev - m_cur)
            p = jnp.exp(s - m_cur[:, :1])
            l_ref[r] = alpha * l_ref[r] + jnp.sum(p, axis=1, keepdims=True)
            acc_ref[r] = acc_ref[r] * alpha[:, :1] + jnp.dot(p.astype(BF16), vals, preferred_element_type=F32)
            m_ref[r] = m_cur

    @pl.when(kt == pl.num_programs(2) - 1)
    def _():
        for h in range(DSA_HEADS):
            r = slice(h * tq, (h + 1) * tq)
            o_lat = (acc_ref[r] / l_ref[r][:, :1]).astype(BF16)
            o_ref[:, h * DSA_V_DIM:(h + 1) * DSA_V_DIM] = jnp.dot(
                o_lat, wuv_ref[h], preferred_element_type=F32).astype(o_ref.dtype)


def _dsa_attention(q, keys, bias, wuv, *, batch):
    n = keys.shape[0]
    seq = n // batch
    tq = min(DSA_TQ, seq)
    tk = min(DSA_TK, seq)
    nq, nk = seq // tq, seq // tk
    last = lambda i: ((i + 1) * tq - 1) // tk
    return pl.pallas_call(
        _dsa_attn_kernel,
        grid=(batch, nq, nk),
        in_specs=[pl.BlockSpec((DSA_HEADS, tq, DSA_KEY_W), lambda b, i, k: (0, b * nq + i, 0)),
                  pl.BlockSpec((tk, DSA_KEY_W), lambda b, i, k: (b * nk + jnp.minimum(k, last(i)), 0)),
                  pl.BlockSpec((tq, tk), lambda b, i, k: (b * nq + i, jnp.minimum(k, last(i)))),
                  pl.BlockSpec(wuv.shape, lambda b, i, k: (0, 0, 0))],
        out_specs=pl.BlockSpec((tq, DSA_HEADS * DSA_V_DIM), lambda b, i, k: (b * nq + i, 0)),
        out_shape=jax.ShapeDtypeStruct((n, DSA_HEADS * DSA_V_DIM), BF16),
        scratch_shapes=[pltpu.VMEM((DSA_HEADS * tq, LANES), F32), pltpu.VMEM((DSA_HEADS * tq, LANES), F32),
                        pltpu.VMEM((DSA_HEADS * tq, DSA_KV_LORA), F32)],
        compiler_params=_cparams("parallel", "parallel", "arbitrary"),
    )(q, keys, bias, wuv)


def _dsa_rope_tables(seq):
    inv_freq = ROPE_THETA ** (-jnp.arange(0, DSA_ROPE_DIM, 2, dtype=F32) / DSA_ROPE_DIM)
    ang = jnp.arange(seq, dtype=F32)[:, None] * inv_freq[None, :]
    c, s = jnp.cos(ang), jnp.sin(ang)
    one, zero = jnp.ones_like(c), jnp.zeros_like(c)
    return (jnp.concatenate([c, c, c, c], axis=1), jnp.concatenate([s, s, s, s], axis=1),
            jnp.concatenate([c, c, one, one], axis=1), jnp.concatenate([s, s, zero, zero], axis=1))


def _dsa_in_weight(w_in):
    o1 = DSA_Q_LORA
    o2 = o1 + DSA_KV_LORA
    o3 = o2 + DSA_ROPE_DIM
    o4 = o3 + IDX_DIM
    o5 = o4 + IDX_HEADS
    pad = jnp.zeros((w_in.shape[0], LANES - DSA_ROPE_DIM - IDX_HEADS), w_in.dtype)
    return jnp.concatenate([w_in[:, :o1], w_in[:, o5:], w_in[:, o1:o2], w_in[:, o3:o4], w_in[:, o2:o3],
                            w_in[:, o4:o5], pad], axis=1)


def _dsa_mixer_pallas(h, mix_params, *, batch):
    q_norm_g, kv_norm_g, w_q_up, w_idx_q, ik_g, ik_b, w_kv_up = mix_params
    n = h.shape[0]
    seq = n // batch
    tabs = _dsa_rope_tables(seq)
    cq, keys, ik, iw = _dsa_prep(h, tabs, q_norm_g, kv_norm_g, ik_g, ik_b, seq=seq)
    wq = w_q_up.reshape(DSA_Q_LORA, DSA_HEADS, DSA_NOPE_DIM + DSA_ROPE_DIM).astype(BF16)
    w_nope = wq[:, :, :DSA_NOPE_DIM].reshape(DSA_Q_LORA, -1)
    w_ri = jnp.concatenate([wq[:, :, DSA_NOPE_DIM:].reshape(DSA_Q_LORA, -1), w_idx_q.astype(BF16)], axis=1)
    qn = _matmul([cq], [w_nope], tm=1024, tn=512, out_dtype=BF16)
    qri = _matmul([cq], [w_ri], tm=1024, tn=512, out_dtype=F32)
    w_kv = w_kv_up.reshape(DSA_KV_LORA, DSA_HEADS, DSA_NOPE_DIM + DSA_V_DIM).astype(BF16)
    wuk_t = jnp.transpose(w_kv[:, :, :DSA_NOPE_DIM], (1, 2, 0))
    wuv = jnp.transpose(w_kv[:, :, DSA_NOPE_DIM:], (1, 0, 2))
    q, iq = _dsa_qpost(qn, qri, wuk_t, tabs, seq=seq)
    bias = _dsa_index(iq, iw, ik, batch=batch, topk=min(DSA_TOPK_MAX, seq // 4))
    return _dsa_attention(q, keys, bias, wuv, batch=batch)


def _trunk(x, mem, mem_ln_g, mem_ln_b, w_mem_kv, layers):
    B, T, D = x.shape
    n = B * T
    nm = mem.shape[1]
    mem_n = _ln_plain(mem.reshape(B * nm, D), mem_ln_g, mem_ln_b)
    mem_kv = _matmul([mem_n], [w_mem_kv.astype(BF16)], tm=512, tn=512, out_dtype=BF16).reshape(B, nm, 2 * MEM_Q)
    x_f = x.reshape(n, D)
    x_b = x_f.astype(BF16)
    for i, (w_in, mix_params, w_out, ln1_g, ln1_b, ffn_params, ln2_g, ln2_b) in enumerate(layers):
        kind = i % N_MIXERS
        n_in = w_in.shape[1]
        n_mix = n_in - MEM_Q
        if kind == 2:
            w_in_b = _dsa_in_weight(w_in).astype(BF16)
            h = _matmul([x_b], [w_in_b], tm=1024, tn=_pick_tn(w_in_b.shape[1]), out_dtype=F32)
            y_mix = _dsa_mixer_pallas(h, mix_params, batch=B)
            y_mem = _mem_attention(h, mem_kv, batch=B, col_block=2)
        elif kind == 0:
            w_main, w_tail = _gdn_in_weights(w_in)
            h_main = _matmul([x_b], [w_main.astype(BF16)], tm=1024, tn=512, out_dtype=F32)
            h_tail = _matmul([x_b], [w_tail.astype(BF16)], tm=1024, tn=w_tail.shape[1], out_dtype=F32)
            y_mix = _gdn_mixer_pallas(h_main, h_tail, mix_params, batch=B)
            y_mem = _mem_attention(h_tail, mem_kv, batch=B)
        else:
            h = _matmul([x_b], [w_in.astype(BF16)], tm=1024, tn=512, out_dtype=BF16)
            y_mix = _dilated_mixer_pallas(h, batch=B)
            y_mem = _mem_attention(h, mem_kv, batch=B, col_block=n_mix // MEM_Q)
        w_out_b = w_out.astype(BF16)
        n_out = y_mix.shape[1]
        y = _matmul([y_mix, y_mem], [w_out_b[:n_out], w_out_b[n_out:]], tm=1024, tn=512, out_dtype=F32)
        x_f, x_lin = _ln_residual(x_f, y, ln1_g, ln1_b)
        x_f, x_b = _moe_block(x_f, x_lin, *ffn_params, ln2_g, ln2_b)
    return x_f.reshape(B, T, D)


def _pick_tn(n):
    for tn in (512, 384, 256, 128):
        if n % tn == 0:
            return tn
    return n


def kernel(x, mem, mem_ln_g, mem_ln_b, w_mem_kv, w_in_0, conv_w_0, a_log_0, dt_bias_0, gdn_norm_g_0, w_out_0, ln1_g_0, ln1_b_0, router_w_0, router_b_0, moe_w1_0, moe_b1_0, moe_w2_0, moe_b2_0, ln2_g_0, ln2_b_0, w_in_1, w_out_1, ln1_g_1, ln1_b_1, router_w_1, router_b_1, moe_w1_1, moe_b1_1, moe_w2_1, moe_b2_1, ln2_g_1, ln2_b_1, w_in_2, q_norm_g_2, kv_norm_g_2, w_q_up_2, w_idx_q_2, idx_k_norm_g_2, idx_k_norm_b_2, w_kv_up_2, w_out_2, ln1_g_2, ln1_b_2, router_w_2, router_b_2, moe_w1_2, moe_b1_2, moe_w2_2, moe_b2_2, ln2_g_2, ln2_b_2, w_in_3, conv_w_3, a_log_3, dt_bias_3, gdn_norm_g_3, w_out_3, ln1_g_3, ln1_b_3, router_w_3, router_b_3, moe_w1_3, moe_b1_3, moe_w2_3, moe_b2_3, ln2_g_3, ln2_b_3):
    layers = (
        (w_in_0, (conv_w_0, a_log_0, dt_bias_0, gdn_norm_g_0), w_out_0, ln1_g_0, ln1_b_0,
         (router_w_0, router_b_0, moe_w1_0, moe_b1_0, moe_w2_0, moe_b2_0), ln2_g_0, ln2_b_0),
        (w_in_1, (), w_out_1, ln1_g_1, ln1_b_1,
         (router_w_1, router_b_1, moe_w1_1, moe_b1_1, moe_w2_1, moe_b2_1), ln2_g_1, ln2_b_1),
        (w_in_2, (q_norm_g_2, kv_norm_g_2, w_q_up_2, w_idx_q_2, idx_k_norm_g_2, idx_k_norm_b_2, w_kv_up_2), w_out_2,
         ln1_g_2, ln1_b_2, (router_w_2, router_b_2, moe_w1_2, moe_b1_2, moe_w2_2, moe_b2_2), ln2_g_2, ln2_b_2),
        (w_in_3, (conv_w_3, a_log_3, dt_bias_3, gdn_norm_g_3), w_out_3, ln1_g_3, ln1_b_3,
         (router_w_3, router_b_3, moe_w1_3, moe_b1_3, moe_w2_3, moe_b2_3), ln2_g_3, ln2_b_3),
    )
    return _trunk(x, mem, mem_ln_g, mem_ln_b, w_mem_kv, layers)
```

```python
import functools
import math

import jax
import jax.numpy as jnp
from jax import lax
from jax.experimental import pallas as pl
from jax.experimental.pallas import tpu as pltpu

F32 = jnp.float32
BF16 = jnp.bfloat16
I32 = jnp.int32
HIGHEST = lax.Precision.HIGHEST

V7X_VMEM_LIMIT_BYTES = 56 * 1024 * 1024

DEPTH = 4
N_MIXERS = 3
MEM_HEADS = 4
MEM_HEAD_DIM = 128
MEM_Q = MEM_HEADS * MEM_HEAD_DIM

GDN_K_HEADS = 16
GDN_V_HEADS = 32
GDN_HEAD_DIM = 128
GDN_KEY_DIM = GDN_K_HEADS * GDN_HEAD_DIM
GDN_VAL_DIM = GDN_V_HEADS * GDN_HEAD_DIM
GDN_CONV_DIM = 2 * GDN_KEY_DIM + GDN_VAL_DIM
GDN_CONV_TAPS = 4
GDN_CHUNK = 64

DIL_GROUPS = ((128, 1), (512, 4), (2048, 16))
DIL_HEADS = 16
DIL_HEAD_DIM = 128
DIL_OUT = DIL_HEADS * DIL_HEAD_DIM

DSA_HEADS = 32
DSA_NOPE_DIM = 128
DSA_ROPE_DIM = 64
DSA_V_DIM = 128
DSA_Q_LORA = 1024
DSA_KV_LORA = 512
IDX_HEADS = 32
IDX_DIM = 128
IDX_ROPE_DIM = 64
DSA_TOPK_MAX = 256
ROPE_THETA = 10000.0

N_EXPERTS = 32
TOP_K = 4
MOE_FF = 384
SWIGLU_ALPHA = 1.702
SWIGLU_LIMIT = 7.0

DEEPNORM_ALPHA = (2 * DEPTH) ** 0.25
LN_EPS = 1e-5
RMS_EPS = 1e-6


def _cparams(*sem):
    return pltpu.CompilerParams(dimension_semantics=sem, vmem_limit_bytes=V7X_VMEM_LIMIT_BYTES)


def _mm_kernel(*refs, n_pairs):
    o_ref = refs[-1]
    acc = None
    for p in range(n_pairs):
        d = jnp.dot(refs[p][...], refs[n_pairs + p][...], preferred_element_type=F32)
        acc = d if acc is None else acc + d
    o_ref[...] = acc.astype(o_ref.dtype)


def _matmul(a_list, w_list, *, tm, tn, out_dtype):
    m = a_list[0].shape[0]
    n = w_list[0].shape[1]
    tm = min(tm, m)
    tn = min(tn, n)
    assert m % tm == 0 and n % tn == 0, (m, n, tm, tn)
    in_specs = [pl.BlockSpec((tm, a.shape[1]), lambda i, j: (i, 0)) for a in a_list]
    in_specs += [pl.BlockSpec((w.shape[0], tn), lambda i, j: (0, j)) for w in w_list]
    return pl.pallas_call(
        functools.partial(_mm_kernel, n_pairs=len(a_list)),
        grid=(m // tm, n // tn),
        in_specs=in_specs,
        out_specs=pl.BlockSpec((tm, tn), lambda i, j: (i, j)),
        out_shape=jax.ShapeDtypeStruct((m, n), out_dtype),
        compiler_params=_cparams("parallel", "arbitrary"),
    )(*a_list, *w_list)


def _ln_rows(v, g, b):
    mu = jnp.mean(v, axis=-1, keepdims=True)
    c = v - mu
    var = jnp.mean(c * c, axis=-1, keepdims=True)
    return c * lax.rsqrt(var + LN_EPS) * g + b


def _ln_res_kernel(x_ref, y_ref, g_ref, b_ref, of_ref, ol_ref):
    out = _ln_rows(DEEPNORM_ALPHA * x_ref[...] + y_ref[...], g_ref[...], b_ref[...])
    of_ref[...] = out
    spr = out.shape[1] // LANES
    _to_row_linear(ol_ref, out, spr, spr)


def _ln_residual(x, y, g, b, *, tr=128):
    n, d = x.shape
    tr = min(tr, n)
    spr = d // LANES
    row = pl.BlockSpec((tr, d), lambda i: (i, 0))
    vec = pl.BlockSpec((1, d), lambda i: (0, 0))
    return pl.pallas_call(
        _ln_res_kernel,
        grid=(n // tr,),
        in_specs=[row, row, vec, vec],
        out_specs=[row, pl.BlockSpec((tr * spr, LANES), lambda i: (i, 0))],
        out_shape=[jax.ShapeDtypeStruct((n, d), F32), jax.ShapeDtypeStruct((n * spr, LANES), F32)],
        compiler_params=_cparams("parallel"),
    )(x, y, g.reshape(1, d), b.reshape(1, d))


def _ln_plain_kernel(x_ref, g_ref, b_ref, ob_ref):
    ob_ref[...] = _ln_rows(x_ref[...], g_ref[...], b_ref[...]).astype(BF16)


def _ln_plain(x, g, b, *, tr=128):
    n, d = x.shape
    tr = min(tr, n)
    row = pl.BlockSpec((tr, d), lambda i: (i, 0))
    vec = pl.BlockSpec((1, d), lambda i: (0, 0))
    return pl.pallas_call(
        _ln_plain_kernel,
        grid=(n // tr,),
        in_specs=[row, vec, vec],
        out_specs=row,
        out_shape=jax.ShapeDtypeStruct((n, d), BF16),
        compiler_params=_cparams("parallel"),
    )(x, g.reshape(1, d), b.reshape(1, d))


def _mem_attn_kernel(q_ref, kv_ref, o_ref):
    scale = MEM_HEAD_DIM ** -0.5
    for h in range(MEM_HEADS):
        lo = h * MEM_HEAD_DIM
        q = q_ref[:, lo:lo + MEM_HEAD_DIM].astype(BF16)
        k = kv_ref[0, :, lo:lo + MEM_HEAD_DIM]
        v = kv_ref[0, :, MEM_Q + lo:MEM_Q + lo + MEM_HEAD_DIM]
        s = lax.dot_general(q, k, (((1,), (1,)), ((), ())), preferred_element_type=F32) * scale
        m = jnp.max(s, axis=-1, keepdims=True)
        p = jnp.exp(s - m)
        l = jnp.sum(p, axis=-1, keepdims=True)
        p = (p / l).astype(BF16)
        o_ref[:, lo:lo + MEM_HEAD_DIM] = jnp.dot(p, v, preferred_element_type=F32).astype(o_ref.dtype)


def _mem_attention(q, mem_kv, *, batch, col_block=0, tq=512):
    n = q.shape[0]
    t = n // batch
    tq = min(tq, t)
    nm = mem_kv.shape[1]
    return pl.pallas_call(
        _mem_attn_kernel,
        grid=(batch, t // tq),
        in_specs=[pl.BlockSpec((tq, MEM_Q), lambda b, i: (b * (t // tq) + i, col_block)),
                  pl.BlockSpec((1, nm, 2 * MEM_Q), lambda b, i: (b, 0, 0))],
        out_specs=pl.BlockSpec((tq, MEM_Q), lambda b, i: (b * (t // tq) + i, 0)),
        out_shape=jax.ShapeDtypeStruct((n, MEM_Q), BF16),
        compiler_params=_cparams("parallel", "parallel"),
    )(q, mem_kv)


MOE_TILE = 256
ROUTER_TILE = 512
COMBINE_TILE = 128


def _router_kernel(x_ref, wt_ref, b_ref, idx_ref, gate_ref, rank_ref, cnt_ref, carry_ref):
    tr = x_ref.shape[0]

    @pl.when(pl.program_id(0) == 0)
    def _():
        carry_ref[...] = jnp.zeros_like(carry_ref)

    logits = lax.dot_general(wt_ref[...], x_ref[...], (((1,), (1,)), ((), ())),
                             precision=HIGHEST, preferred_element_type=F32) + b_ref[...]
    e_iota = lax.broadcasted_iota(I32, (N_EXPERTS, tr), 0)
    work = logits
    vals, onehots = [], []
    for k in range(TOP_K):
        m = jnp.max(work, axis=0, keepdims=True)
        idx = jnp.min(jnp.where(work == m, e_iota, N_EXPERTS), axis=0, keepdims=True)
        oh = e_iota == idx
        vals.append(m)
        onehots.append(oh)
        idx_ref[k:k + 1, :] = idx
        work = jnp.where(oh, -jnp.inf, work)
    exps = [jnp.exp(v - vals[0]) for v in vals]
    denom = exps[0] + exps[1] + exps[2] + exps[3]
    for k in range(TOP_K):
        gate_ref[k:k + 1, :] = exps[k] / denom
    mask = (onehots[0] | onehots[1] | onehots[2] | onehots[3])
    r_i = lax.broadcasted_iota(I32, (tr, tr), 0)
    c_i = lax.broadcasted_iota(I32, (tr, tr), 1)
    before = (r_i < c_i).astype(BF16)
    excl = jnp.dot(mask.astype(BF16), before, preferred_element_type=F32)
    rank = carry_ref[...] + excl
    for k in range(TOP_K):
        rank_ref[k:k + 1, :] = jnp.sum(jnp.where(onehots[k], rank, 0.0), axis=0, keepdims=True).astype(I32)
    carry_ref[...] += jnp.sum(mask.astype(F32), axis=1, keepdims=True)
    cnt_ref[...] = carry_ref[...]


def _router(x, router_w, router_b):
    n, d = x.shape
    tr = min(ROUTER_TILE, n)
    slab = pl.BlockSpec((TOP_K, tr), lambda i: (0, i))
    return pl.pallas_call(
        _router_kernel,
        grid=(n // tr,),
        in_specs=[pl.BlockSpec((tr, d), lambda i: (i, 0)),
                  pl.BlockSpec((N_EXPERTS, d), lambda i: (0, 0)),
                  pl.BlockSpec((N_EXPERTS, 1), lambda i: (0, 0))],
        out_specs=[slab, slab, slab, pl.BlockSpec((N_EXPERTS, 1), lambda i: (0, 0))],
        out_shape=[jax.ShapeDtypeStruct((TOP_K, n), I32), jax.ShapeDtypeStruct((TOP_K, n), F32),
                   jax.ShapeDtypeStruct((TOP_K, n), I32), jax.ShapeDtypeStruct((N_EXPERTS, 1), F32)],
        scratch_shapes=[pltpu.VMEM((N_EXPERTS, 1), F32)],
        compiler_params=_cparams("arbitrary"),
    )(x, router_w.T, router_b.reshape(N_EXPERTS, 1))


LANES = 128
ROW_PAD = 8


def _gather_rows(idx_ref, base, src_hbm, dst_ref, sem, n_rows, spr, src_pitch, dst_pitch):
    def body(r, c):
        src = pl.multiple_of(idx_ref[base + r] * src_pitch, SUBLANES)
        dst = pl.multiple_of(r * dst_pitch, SUBLANES)
        pltpu.make_async_copy(src_hbm.at[pl.ds(src, spr)], dst_ref.at[pl.ds(dst, spr)], sem).start()
        return c
    lax.fori_loop(0, n_rows, body, 0)


def _from_row_linear(ref, first, n_rows, spr, pitch):
    return jnp.concatenate([ref[pl.ds(first * pitch + c, n_rows, stride=pitch), :] for c in range(spr)], axis=1)


def _to_row_linear(ref, val, spr, pitch):
    n_rows = val.shape[0]
    for c in range(spr):
        ref[pl.ds(c, n_rows, stride=pitch), :] = val[:, c * LANES:(c + 1) * LANES]
    for c in range(spr, pitch):
        ref[pl.ds(c, n_rows, stride=pitch), :] = jnp.zeros((n_rows, LANES), val.dtype)


def _expert_kernel(tile_e_ref, n_tiles_ref, rowtok_ref, x_hbm, w1g_ref, w1l_ref, b1g_ref, b1l_ref, w2_ref, b2_ref,
                   o_ref, xbuf, sem):
    i = pl.program_id(0)
    spr = w1g_ref.shape[1] // LANES
    pitch = spr + ROW_PAD
    tm = xbuf.shape[1] // pitch
    slot = i % 2
    n_tiles = n_tiles_ref[0]

    @pl.when(i == 0)
    def _():
        _gather_rows(rowtok_ref, 0, x_hbm, xbuf.at[0], sem.at[0], tm, spr, spr, pitch)

    @pl.when(i + 1 < n_tiles)
    def _():
        _gather_rows(rowtok_ref, (i + 1) * tm, x_hbm, xbuf.at[1 - slot], sem.at[1 - slot], tm, spr, spr, pitch)

    @pl.when(i < n_tiles)
    def _():
        done = xbuf.at[slot, pl.ds(0, tm * spr)]
        pltpu.make_async_copy(done, done, sem.at[slot]).wait()
        xs = _from_row_linear(xbuf.at[slot], 0, tm, spr, pitch).astype(BF16)
        glu = jnp.dot(xs, w1g_ref[0], preferred_element_type=F32) + b1g_ref[0]
        lin = jnp.dot(xs, w1l_ref[0], preferred_element_type=F32) + b1l_ref[0]
        glu = jnp.minimum(glu, SWIGLU_LIMIT)
        lin = jnp.clip(lin, -SWIGLU_LIMIT, SWIGLU_LIMIT)
        act = glu * jax.nn.sigmoid(SWIGLU_ALPHA * glu) * (lin + 1.0)
        y = jnp.dot(act.astype(BF16), w2_ref[0], preferred_element_type=F32) + b2_ref[0]
        _to_row_linear(o_ref, y, spr, pitch)

    @pl.when(i >= n_tiles)
    def _():
        o_ref[...] = jnp.zeros_like(o_ref)


def _w1_prep_kernel(w_ref, o_ref):
    two_ff = w_ref.shape[2]
    src = lax.broadcasted_iota(I32, (two_ff, two_ff), 0)
    dst = lax.broadcasted_iota(I32, (two_ff, two_ff), 1)
    want = jnp.where(dst < two_ff // 2, 2 * dst, 2 * (dst - two_ff // 2) + 1)
    perm = (src == want).astype(BF16)
    o_ref[0] = jnp.dot(w_ref[0].astype(BF16), perm, preferred_element_type=F32).astype(BF16)


def _w1_prep(w1, *, tr=1024):
    e, d, two_ff = w1.shape
    tr = min(tr, d)
    blk = pl.BlockSpec((1, tr, two_ff), lambda i, j: (i, j, 0))
    return pl.pallas_call(
        _w1_prep_kernel,
        grid=(e, d // tr),
        in_specs=[blk],
        out_specs=blk,
        out_shape=jax.ShapeDtypeStruct(w1.shape, BF16),
        compiler_params=_cparams("parallel", "parallel"),
    )(w1)


def _expert_ffn(x_lin, tile_e, n_tiles, rowtok, w1p, b1g, b1l, w2, b2, *, max_tiles):
    tm = MOE_TILE
    d, ff = w1p.shape[1], w1p.shape[2] // 2
    spr = d // LANES
    pitch = spr + ROW_PAD
    wmap = lambda i, te, nt, rt: (te[i], 0, 0)
    w1g = w1l = w1p
    return pl.pallas_call(
        _expert_kernel,
        grid_spec=pltpu.PrefetchScalarGridSpec(
            num_scalar_prefetch=3,
            grid=(max_tiles,),
            in_specs=[pl.BlockSpec(memory_space=pl.ANY),
                      pl.BlockSpec((1, d, ff), wmap), pl.BlockSpec((1, d, ff), lambda i, te, nt, rt: (te[i], 0, 1)),
                      pl.BlockSpec((1, 1, ff), wmap), pl.BlockSpec((1, 1, ff), wmap),
                      pl.BlockSpec((1, ff, d), wmap), pl.BlockSpec((1, 1, d), wmap)],
            out_specs=pl.BlockSpec((tm * pitch, LANES), lambda i, te, nt, rt: (i, 0)),
            scratch_shapes=[pltpu.VMEM((2, tm * pitch, LANES), F32), pltpu.SemaphoreType.DMA((2,))]),
        out_shape=jax.ShapeDtypeStruct((max_tiles * tm * pitch, LANES), F32),
        compiler_params=_cparams("arbitrary"),
    )(tile_e, n_tiles, rowtok, x_lin, w1g, w1l, b1g, b1l, w2, b2)


def _combine_kernel(pos_ref, ys_hbm, x_ref, gate_ref, g_ref, b_ref, of_ref, ob_ref, ybuf, sem):
    i = pl.program_id(0)
    nsteps = pl.num_programs(0)
    tt, d = x_ref.shape
    spr = d // LANES
    pitch = spr + ROW_PAD
    n = nsteps * tt
    slot = i % 2

    def fetch(step, s):
        for k in range(TOP_K):
            _gather_rows(pos_ref, k * n + step * tt, ys_hbm, ybuf.at[s, pl.ds(k * tt * pitch, tt * pitch)],
                         sem.at[s], tt, spr, pitch, pitch)

    @pl.when(i == 0)
    def _():
        fetch(0, 0)

    @pl.when(i + 1 < nsteps)
    def _():
        fetch(i + 1, 1 - slot)

    done = ybuf.at[slot, pl.ds(0, TOP_K * tt * spr)]
    pltpu.make_async_copy(done, done, sem.at[slot]).wait()
    gate = gate_ref[...]
    moe = None
    for k in range(TOP_K):
        term = gate[:, k:k + 1] * _from_row_linear(ybuf.at[slot], k * tt, tt, spr, pitch)
        moe = term if moe is None else moe + term
    out = _ln_rows(DEEPNORM_ALPHA * x_ref[...] + moe, g_ref[...], b_ref[...])
    of_ref[...] = out
    ob_ref[...] = out.astype(BF16)


def _combine_ln(pos_flat, ys, x, gate_t, g, b):
    n, d = x.shape
    tt = min(COMBINE_TILE, n)
    pitch = d // LANES + ROW_PAD
    row = pl.BlockSpec((tt, d), lambda i, p: (i, 0))
    vec = pl.BlockSpec((1, d), lambda i, p: (0, 0))
    return pl.pallas_call(
        _combine_kernel,
        grid_spec=pltpu.PrefetchScalarGridSpec(
            num_scalar_prefetch=1,
            grid=(n // tt,),
            in_specs=[pl.BlockSpec(memory_space=pl.ANY), row,
                      pl.BlockSpec((tt, TOP_K), lambda i, p: (i, 0)), vec, vec],
            out_specs=[row, row],
            scratch_shapes=[pltpu.VMEM((2, TOP_K * tt * pitch, LANES), F32), pltpu.SemaphoreType.DMA((2,))]),
        out_shape=[jax.ShapeDtypeStruct((n, d), F32), jax.ShapeDtypeStruct((n, d), BF16)],
        compiler_params=_cparams("arbitrary"),
    )(pos_flat, ys, x, gate_t, g.reshape(1, d), b.reshape(1, d))


def _moe_block(x_f32, x_lin, router_w, router_b, w1, b1, w2, b2, ln_g, ln_b):
    n, d = x_f32.shape
    tm = MOE_TILE
    top_i, gate, rank, counts = _router(x_f32, router_w, router_b)
    counts = counts[:, 0].astype(I32)
    padded = (counts + tm - 1) // tm * tm
    ends = jnp.cumsum(padded)
    starts = ends - padded
    max_tiles = (n * TOP_K) // tm + N_EXPERTS
    e_ids = jnp.arange(N_EXPERTS, dtype=I32)
    start_of = jnp.sum(jnp.where(top_i[:, :, None] == e_ids, starts, 0), axis=-1)
    pos = start_of + rank
    tile_start = jnp.arange(max_tiles, dtype=I32) * tm
    tile_e = jnp.minimum(jnp.sum(tile_start[:, None] >= ends[None, :], axis=1), N_EXPERTS - 1).astype(I32)
    n_tiles = (ends[-1] // tm).astype(I32).reshape(1)
    tok = jnp.broadcast_to(jnp.arange(n, dtype=I32)[None, :], (TOP_K, n))
    rowtok = jnp.zeros((max_tiles * tm,), I32).at[pos.reshape(-1)].set(tok.reshape(-1))
    b1g = b1[:, None, 0::2]
    b1l = b1[:, None, 1::2]
    ys = _expert_ffn(x_lin, tile_e, n_tiles, rowtok, _w1_prep(w1), b1g, b1l, w2.astype(BF16), b2[:, None, :],
                     max_tiles=max_tiles)
    return _combine_ln(pos.reshape(-1), ys, x_f32, gate.T, ln_g, ln_b)


def _rms_norm(x, g):
    xf = x.astype(F32)
    return (xf * lax.rsqrt(jnp.mean(jnp.square(xf), axis=-1, keepdims=True) + RMS_EPS) * g).astype(x.dtype)


def _l2_normalize(x):
    xf = x.astype(F32)
    return xf * lax.rsqrt(jnp.sum(jnp.square(xf), axis=-1, keepdims=True) + RMS_EPS)


def _layer_norm(x, g, b):
    xf = x.astype(F32)
    mu = jnp.mean(xf, axis=-1, keepdims=True)
    var = jnp.mean(jnp.square(xf - mu), axis=-1, keepdims=True)
    return ((xf - mu) * lax.rsqrt(var + LN_EPS) * g + b).astype(x.dtype)


def _rope_tables(T, dim):
    inv_freq = ROPE_THETA ** (-jnp.arange(0, dim, 2, dtype=F32) / dim)
    ang = jnp.arange(T, dtype=F32)[:, None] * inv_freq[None, :]
    return jnp.cos(ang), jnp.sin(ang)


def _apply_rope(x, cos, sin):
    x1, x2 = jnp.split(x.astype(F32), 2, axis=-1)
    return jnp.concatenate([x1 * cos - x2 * sin, x2 * cos + x1 * sin], axis=-1).astype(x.dtype)


def _causal_depthwise_conv(x, w):
    taps, c = w.shape
    return lax.conv_general_dilated(x, w[:, None, :].astype(x.dtype), window_strides=(1,),
                                    padding=((taps - 1, 0),), dimension_numbers=('NWC', 'WIO', 'NWC'),
                                    feature_group_count=c)


def _gated_delta_rule_chunked(q, k, v, g, beta):
    B, T, H, DK = q.shape
    DV = v.shape[-1]
    C = GDN_CHUNK
    N = T // C

    def to_chunks(a):
        a = a.astype(F32).reshape((B, N, C, H) + a.shape[3:])
        return jnp.moveaxis(a, 3, 1)

    q, k, v, g, beta = (to_chunks(a) for a in (q, k, v, g, beta))
    gc = jnp.cumsum(g, axis=-1)
    causal = jnp.tril(jnp.ones((C, C), dtype=bool))
    strict = jnp.tril(jnp.ones((C, C), dtype=bool), -1)
    diff = gc[..., :, None] - gc[..., None, :]
    decay = jnp.where(causal, jnp.exp(jnp.where(causal, diff, 0.0)), 0.0)
    kb = k * beta[..., None]
    lower = jnp.where(strict, jnp.einsum('bhnid,bhnjd->bhnij', kb, k) * decay, 0.0)
    eye = jnp.broadcast_to(jnp.eye(C, dtype=F32), lower.shape)
    tinv = lax.linalg.triangular_solve(eye + lower, eye, left_side=True, lower=True, unit_diagonal=True)
    u = tinv @ (v * beta[..., None])
    w = tinv @ (kb * jnp.exp(gc)[..., None])
    a_intra = jnp.where(causal, jnp.einsum('bhnid,bhnjd->bhnij', q, k) * decay, 0.0)
    q_dec = q * jnp.exp(gc)[..., None]
    k_dec = k * jnp.exp(gc[..., -1:] - gc)[..., None]
    g_last = jnp.exp(gc[..., -1])

    def step(state, xs):
        qd, kd, u_i, w_i, a_i, gl = xs
        v_new = u_i - w_i @ state
        o = qd @ state + a_i @ v_new
        state = state * gl[..., None, None] + jnp.einsum('bhcd,bhce->bhde', kd, v_new)
        return state, o

    xs = tuple(jnp.moveaxis(a, 2, 0) for a in (q_dec, k_dec, u, w, a_intra, g_last))
    _, o = lax.scan(step, jnp.zeros((B, H, DK, DV), F32), xs)
    return jnp.transpose(o, (1, 0, 3, 2, 4)).reshape(B, T, H, DV)


def _gdn_mixer(h, conv_w, a_log, dt_bias, norm_g):
    B, T, _ = h.shape
    qkv, z, b, a = jnp.split(h, [GDN_CONV_DIM, GDN_CONV_DIM + GDN_VAL_DIM,
                                 GDN_CONV_DIM + GDN_VAL_DIM + GDN_V_HEADS], axis=-1)
    qkv = jax.nn.silu(_causal_depthwise_conv(qkv, conv_w))
    q, k, v = jnp.split(qkv, [GDN_KEY_DIM, 2 * GDN_KEY_DIM], axis=-1)
    rep = GDN_V_HEADS // GDN_K_HEADS
    q = jnp.repeat(q.reshape(B, T, GDN_K_HEADS, GDN_HEAD_DIM), rep, axis=2)
    k = jnp.repeat(k.reshape(B, T, GDN_K_HEADS, GDN_HEAD_DIM), rep, axis=2)
    v = v.reshape(B, T, GDN_V_HEADS, GDN_HEAD_DIM)
    q = _l2_normalize(q) * GDN_HEAD_DIM ** -0.5
    k = _l2_normalize(k)
    beta = jax.nn.sigmoid(b.astype(F32))
    g = -jnp.exp(a_log.astype(F32)) * jax.nn.softplus(a.astype(F32) + dt_bias.astype(F32))
    o = _gated_delta_rule_chunked(q, k, v, g, beta)
    o = _rms_norm(o, norm_g) * jax.nn.silu(z.reshape(B, T, GDN_V_HEADS, GDN_HEAD_DIM).astype(F32))
    return o.reshape(B, T, GDN_VAL_DIM).astype(h.dtype)


def _dilated_window_attention(q, k, v, window, dilation):
    B, T, H, E = q.shape
    d = dilation
    w = window // dilation
    ls = T // d
    nb = -(-ls // w)
    pad = nb * w - ls

    def to_sub(a):
        a = jnp.swapaxes(a.reshape(B, ls, d, H, E), 1, 2)
        return jnp.pad(a, ((0, 0), (0, 0), (0, pad), (0, 0), (0, 0)))

    def band(a):
        a = jnp.pad(a, ((0, 0), (0, 0), (w, 0), (0, 0), (0, 0))).reshape(B, d, nb + 1, w, H, E)
        return jnp.concatenate([a[:, :, :-1], a[:, :, 1:]], axis=3)

    qs = to_sub(q).reshape(B, d, nb, w, H, E)
    kb = band(to_sub(k))
    vb = band(to_sub(v))
    s = jnp.einsum('bdnqhe,bdnkhe->bdnhqk', qs, kb).astype(F32) * E ** -0.5
    qi = jnp.arange(w)[:, None]
    kj = jnp.arange(2 * w)[None, :]
    dist = w + qi - kj
    key_sub = jnp.arange(nb)[:, None, None] * w + kj[None] - w
    valid = (dist >= 0) & (dist <= w) & (key_sub >= 0)
    s = jnp.where(valid[:, None], s, -jnp.inf)
    m = jnp.max(s, axis=-1, keepdims=True)
    pexp = jnp.exp(s - m)
    l = jnp.sum(pexp, axis=-1, keepdims=True)
    o = jnp.einsum('bdnhqk,bdnkhe->bdnqhe', pexp, vb.astype(F32)) / jnp.swapaxes(l, 3, 4)
    lse = jnp.swapaxes((m + jnp.log(l))[..., 0], 3, 4)

    def from_sub(a):
        a = a.reshape((B, d, nb * w) + a.shape[4:])[:, :, :ls]
        return jnp.swapaxes(a, 1, 2).reshape((B, T) + a.shape[3:])

    return from_sub(o), from_sub(lse)


def _dilated_mixer(h):
    B, T, _ = h.shape
    qkv = h.reshape(B, T, len(DIL_GROUPS), 3, DIL_HEADS, DIL_HEAD_DIM)
    outs, lses = [], []
    for gi, (window, dilation) in enumerate(DIL_GROUPS):
        o, lse = _dilated_window_attention(qkv[:, :, gi, 0], qkv[:, :, gi, 1], qkv[:, :, gi, 2], window, dilation)
        outs.append(o)
        lses.append(lse)
    wts = jax.nn.softmax(jnp.stack(lses, axis=0), axis=0)
    o = jnp.sum(wts[..., None] * jnp.stack(outs, axis=0), axis=0)
    return o.reshape(B, T, DIL_OUT).astype(h.dtype)


def _dsa_mixer(h, q_norm_g, kv_norm_g, w_q_up, w_idx_q, idx_k_norm_g, idx_k_norm_b, w_kv_up):
    B, T, _ = h.shape
    o1 = DSA_Q_LORA
    o2 = o1 + DSA_KV_LORA
    o3 = o2 + DSA_ROPE_DIM
    o4 = o3 + IDX_DIM
    q_lat, kv_lat, k_rope, idx_k, idx_w = jnp.split(h, [o1, o2, o3, o4], axis=-1)
    cos, sin = _rope_tables(T, DSA_ROPE_DIM)
    cq = _rms_norm(q_lat, q_norm_g)
    q = (cq @ w_q_up).reshape(B, T, DSA_HEADS, DSA_NOPE_DIM + DSA_ROPE_DIM)
    q_nope = q[..., :DSA_NOPE_DIM]
    q_rope = _apply_rope(q[..., DSA_NOPE_DIM:], cos[:, None], sin[:, None])
    c_kv = _rms_norm(kv_lat, kv_norm_g)
    k_rope = _apply_rope(k_rope, cos, sin)
    w_kv = w_kv_up.reshape(DSA_KV_LORA, DSA_HEADS, DSA_NOPE_DIM + DSA_V_DIM)
    w_uk = w_kv[..., :DSA_NOPE_DIM]
    w_uv = w_kv[..., DSA_NOPE_DIM:]
    iq = (cq @ w_idx_q).reshape(B, T, IDX_HEADS, IDX_DIM)
    iq = jnp.concatenate([_apply_rope(iq[..., :IDX_ROPE_DIM], cos[:, None], sin[:, None]),
                          iq[..., IDX_ROPE_DIM:]], axis=-1).astype(F32)
    ik = _layer_norm(idx_k, idx_k_norm_g, idx_k_norm_b)
    ik = jnp.concatenate([_apply_rope(ik[..., :IDX_ROPE_DIM], cos, sin), ik[..., IDX_ROPE_DIM:]],
                         axis=-1).astype(F32)
    iw = idx_w.astype(F32) * (IDX_HEADS ** -0.5 * IDX_DIM ** -0.5)
    topk = min(DSA_TOPK_MAX, T // 4)
    scale = (DSA_NOPE_DIM + DSA_ROPE_DIM) ** -0.5
    key_pos = jnp.arange(T)
    gather = jax.vmap(lambda table, idx: table[idx])
    qb = 128

    def attend_block(args):
        qn, qr, iqb, iwb, qpos = args
        isc = jax.nn.relu(jnp.einsum('bqhd,bsd->bqhs', iqb, ik))
        isc = jnp.einsum('bqhs,bqh->bqs', isc, iwb)
        isc = jnp.where(key_pos[None, None, :] <= qpos[None, :, None], isc, -jnp.inf)
        _, sel = lax.top_k(isc, topk)
        c_sel = gather(c_kv, sel)
        r_sel = gather(k_rope, sel)
        q_abs = jnp.einsum('bqhd,chd->bqhc', qn, w_uk)
        s = (jnp.einsum('bqhc,bqkc->bqhk', q_abs, c_sel)
             + jnp.einsum('bqhr,bqkr->bqhk', qr, r_sel)).astype(F32) * scale
        s = jnp.where((sel <= qpos[None, :, None])[:, :, None, :], s, -jnp.inf)
        p = jax.nn.softmax(s, axis=-1).astype(c_sel.dtype)
        o_lat = jnp.einsum('bqhk,bqkc->bqhc', p, c_sel)
        return jnp.einsum('bqhc,chd->bqhd', o_lat, w_uv)

    nqb = T // qb

    def blocks(a):
        return jnp.moveaxis(a.reshape((B, nqb, qb) + a.shape[2:]), 1, 0)

    qpos = jnp.arange(T).reshape(nqb, qb)
    out = lax.map(attend_block, (blocks(q_nope), blocks(q_rope), blocks(iq), blocks(iw), qpos))
    return jnp.moveaxis(out, 0, 1).reshape(B, T, DSA_HEADS * DSA_V_DIM).astype(h.dtype)


DIL_BAND = 128
DIL_HEADS_PER_STEP = 4
DIL_MASKED = -1e30


def _dil_attn_kernel(q_ref, kp_ref, ko_ref, vp_ref, vo_ref, o_ref, lse_ref):
    nblk = pl.program_id(2)
    w = DIL_BAND
    e = DIL_HEAD_DIM
    scale = e ** -0.5
    qi = lax.broadcasted_iota(I32, (w, w), 0)
    kj = lax.broadcasted_iota(I32, (w, w), 1)
    own_ok = kj <= qi
    prev_ok = (kj >= qi) & (nblk > 0)
    contract_last = (((1,), (1,)), ((), ()))
    for hh in range(DIL_HEADS_PER_STEP):
        cols = slice(hh * e, (hh + 1) * e)
        q = q_ref[0, :, cols]
        s_own = lax.dot_general(q, ko_ref[0, :, cols], contract_last, preferred_element_type=F32) * scale
        s_prev = lax.dot_general(q, kp_ref[0, :, cols], contract_last, preferred_element_type=F32) * scale
        s_own = jnp.where(own_ok, s_own, DIL_MASKED)
        s_prev = jnp.where(prev_ok, s_prev, DIL_MASKED)
        m = jnp.maximum(jnp.max(s_own, axis=1, keepdims=True), jnp.max(s_prev, axis=1, keepdims=True))
        p_own = jnp.exp(s_own - m)
        p_prev = jnp.exp(s_prev - m)
        l = jnp.sum(p_own, axis=1, keepdims=True) + jnp.sum(p_prev, axis=1, keepdims=True)
        o = jnp.dot(p_own.astype(BF16), vo_ref[0, :, cols], preferred_element_type=F32)
        o = o + jnp.dot(p_prev.astype(BF16), vp_ref[0, :, cols], preferred_element_type=F32)
        o_ref[0, :, cols] = o / l
        lse_ref[0, :, cols] = jnp.broadcast_to(m + jnp.log(l), (w, e))


def _dil_group_attention(h, dilation, *, batch):
    gi = 0
    n, wid = h.shape
    seq = n // batch
    d = dilation
    ls = seq // d
    nb = ls // DIL_BAND
    hw = DIL_HEADS_PER_STEP * DIL_HEAD_DIM
    hsteps = DIL_OUT // hw
    wb = wid // hw
    assert wid % hw == 0 and ls % DIL_BAND == 0
    hv = h.reshape(batch, ls, d * wid)
    base = lambda j: (gi * 3 + j) * hsteps
    own = lambda j: pl.BlockSpec((1, DIL_BAND, hw), lambda b, r, s, c: (b, s, r * wb + base(j) + c))
    prev = lambda j: pl.BlockSpec((1, DIL_BAND, hw),
                                  lambda b, r, s, c: (b, jnp.maximum(s - 1, 0), r * wb + base(j) + c))
    out = pl.BlockSpec((1, DIL_BAND, hw), lambda b, r, s, c: (b, s, r * hsteps + c))
    o, lse = pl.pallas_call(
        _dil_attn_kernel,
        grid=(batch, d, nb, hsteps),
        in_specs=[own(0), prev(1), own(1), prev(2), own(2)],
        out_specs=[out, out],
        out_shape=[jax.ShapeDtypeStruct((batch, ls, d * DIL_OUT), F32)] * 2,
        compiler_params=_cparams("parallel", "parallel", "parallel", "parallel"),
    )(hv, hv, hv, hv, hv)
    return o.reshape(n, DIL_OUT), lse.reshape(n, DIL_OUT)


def _dil_combine_kernel(o0, o1, o2, l0, l1, l2, y_ref):
    a, b, c = l0[...], l1[...], l2[...]
    m = jnp.maximum(jnp.maximum(a, b), c)
    ea, eb, ec = jnp.exp(a - m), jnp.exp(b - m), jnp.exp(c - m)
    y_ref[...] = ((ea * o0[...] + eb * o1[...] + ec * o2[...]) / (ea + eb + ec)).astype(y_ref.dtype)


def _dilated_mixer_pallas(h_groups, *, batch):
    n = h_groups[0].shape[0]
    outs, lses = [], []
    for h, (window, dilation) in zip(h_groups, DIL_GROUPS):
        assert window // dilation == DIL_BAND
        o, lse = _dil_group_attention(h, dilation, batch=batch)
        outs.append(o)
        lses.append(lse)
    tt = min(512, n)
    blk = pl.BlockSpec((tt, 512), lambda i, c: (i, c))
    return pl.pallas_call(
        _dil_combine_kernel,
        grid=(n // tt, DIL_OUT // 512),
        in_specs=[blk] * 6,
        out_specs=blk,
        out_shape=jax.ShapeDtypeStruct((n, DIL_OUT), BF16),
        compiler_params=_cparams("parallel", "parallel"),
    )(*outs, *lses)


GDN_HEAD_GROUP = 4
GDN_CHUNK_GROUP = 4
GDN_PREP_COLS = 1024
SUBLANES = 8


def _gdn_prep_kernel(x_ref, halo_ref, w_ref, o_ref, *, tiles_per_seq):
    i = pl.program_id(0)
    c = pl.program_id(1)
    x = x_ref[...]
    tt = x.shape[0]
    halo = jnp.where(i % tiles_per_seq == 0, 0.0, halo_ref[...])
    row8 = lax.broadcasted_iota(I32, halo.shape, 0)
    w = w_ref[...]
    y = x * w[GDN_CONV_TAPS - 1:GDN_CONV_TAPS, :]
    for s in range(1, GDN_CONV_TAPS):
        xs = pltpu.roll(x, s, axis=0)
        first = jnp.where(row8 < s, pltpu.roll(halo, s, axis=0), xs[:SUBLANES])
        xs = jnp.concatenate([first, xs[SUBLANES:]], axis=0)
        y = y + xs * w[GDN_CONV_TAPS - 1 - s:GDN_CONV_TAPS - s, :]
    y = y * jax.nn.sigmoid(y)
    n_qk_tiles = 2 * GDN_KEY_DIM // GDN_PREP_COLS

    @pl.when(c >= n_qk_tiles)
    def _():
        o_ref[...] = y

    @pl.when(c < n_qk_tiles)
    def _():
        qscale = jnp.where(c < n_qk_tiles // 2, GDN_HEAD_DIM ** -0.5, 1.0)
        for g in range(GDN_PREP_COLS // GDN_HEAD_DIM):
            ys = y[:, g * GDN_HEAD_DIM:(g + 1) * GDN_HEAD_DIM]
            inv = lax.rsqrt(jnp.sum(ys * ys, axis=-1, keepdims=True) + RMS_EPS)
            o_ref[:, g * GDN_HEAD_DIM:(g + 1) * GDN_HEAD_DIM] = ys * inv * qscale


def _gdn_prep(h_main, conv_w, *, seq):
    n = h_main.shape[0]
    tt = min(256, seq)
    cw = GDN_PREP_COLS
    return pl.pallas_call(
        functools.partial(_gdn_prep_kernel, tiles_per_seq=seq // tt),
        grid=(n // tt, GDN_CONV_DIM // cw),
        in_specs=[pl.BlockSpec((tt, cw), lambda i, c: (i, c)),
                  pl.BlockSpec((SUBLANES, cw), lambda i, c: (jnp.maximum(i * (tt // SUBLANES) - 1, 0), c)),
                  pl.BlockSpec((GDN_CONV_TAPS, cw), lambda i, c: (0, c))],
        out_specs=pl.BlockSpec((tt, cw), lambda i, c: (i, c)),
        out_shape=jax.ShapeDtypeStruct((n, GDN_CONV_DIM), F32),
        compiler_params=_cparams("parallel", "parallel"),
    )(h_main, h_main, conv_w)


def _gdn_gate_kernel(ba_ref, alog_ref, dtb_ref, beta_ref, g_ref):
    ba = ba_ref[...]
    beta = jax.nn.sigmoid(ba)
    xa = ba + dtb_ref[...]
    softplus = jnp.maximum(xa, 0.0) + jnp.log(1.0 + jnp.exp(-jnp.abs(xa)))
    g = -jnp.exp(alog_ref[...]) * softplus
    hg = GDN_HEAD_GROUP
    for j in range(GDN_V_HEADS // hg):
        beta_ref[j] = beta[:, j * hg:(j + 1) * hg]
        g_ref[j] = g[:, GDN_V_HEADS + j * hg:GDN_V_HEADS + (j + 1) * hg]


def _gdn_gates(h_tail, a_log, dt_bias):
    n = h_tail.shape[0]
    tt = min(512, n)
    ng = GDN_V_HEADS // GDN_HEAD_GROUP
    lane_vec = lambda v: jnp.zeros((1, LANES), F32).at[0, GDN_V_HEADS:2 * GDN_V_HEADS].set(v)
    out = pl.BlockSpec((ng, tt, GDN_HEAD_GROUP), lambda i: (0, i, 0))
    return pl.pallas_call(
        _gdn_gate_kernel,
        grid=(n // tt,),
        in_specs=[pl.BlockSpec((tt, LANES), lambda i: (i, MEM_Q // LANES)),
                  pl.BlockSpec((1, LANES), lambda i: (0, 0)), pl.BlockSpec((1, LANES), lambda i: (0, 0))],
        out_specs=[out, out],
        out_shape=[jax.ShapeDtypeStruct((ng, n, GDN_HEAD_GROUP), F32)] * 2,
        compiler_params=_cparams("parallel"),
    )(h_tail, lane_vec(a_log), lane_vec(dt_bias))


def _dot_tril(ones_b, x):
    acc = None
    for _ in range(3):
        x_b = x.astype(BF16)
        d = jnp.dot(ones_b, x_b, preferred_element_type=F32)
        acc = d if acc is None else acc + d
        x = x - x_b.astype(F32)
    return acc


def _gdn_scan_kernel(q_ref, k_ref, v_ref, z_ref, beta_ref, g_ref, ng_ref, o_ref, state_ref):
    C = GDN_CHUNK
    hd = GDN_HEAD_DIM

    @pl.when(pl.program_id(2) == 0)
    def _():
        state_ref[...] = jnp.zeros_like(state_ref)

    r_i = lax.broadcasted_iota(I32, (C, C), 0)
    c_i = lax.broadcasted_iota(I32, (C, C), 1)
    causal = c_i <= r_i
    strict = c_i < r_i
    tril = causal.astype(F32)
    later = (r_i > c_i).astype(F32)
    eye = (r_i == c_i).astype(F32)
    contract_last = (((1,), (1,)), ((), ()))
    norm_g = ng_ref[...]

    tril_b = tril.astype(BF16)
    chunk_heads = [(c, j) for c in range(GDN_CHUNK_GROUP) for j in range(GDN_HEAD_GROUP)]
    rows = lambda c: slice(c * C, (c + 1) * C)
    hcols = lambda j: slice(j * hd, (j + 1) * hd)

    gc_all = [_dot_tril(tril_b, g_ref[0, rows(c), :]) for c in range(GDN_CHUNK_GROUP)]
    kk, qk = {}, {}
    for c in range(GDN_CHUNK_GROUP):
        for kh in range(GDN_HEAD_GROUP // 2):
            k_b = k_ref[rows(c), hcols(kh)].astype(BF16)
            q_b = q_ref[rows(c), hcols(kh)].astype(BF16)
            kk[c, kh] = lax.dot_general(k_b, k_b, contract_last, preferred_element_type=F32)
            qk[c, kh] = lax.dot_general(q_b, k_b, contract_last, preferred_element_type=F32)
    decay, mpow, tinv = {}, {}, {}
    for c, j in chunk_heads:
        diff = _dot_tril(tril_b, g_ref[0, rows(c), j:j + 1] * later)
        decay[c, j] = jnp.where(causal, jnp.exp(jnp.where(causal, diff, 0.0)), 0.0)
        lower = jnp.where(strict, beta_ref[0, rows(c), j:j + 1] * kk[c, j // 2] * decay[c, j], 0.0)
        mpow[c, j] = -lower
        tinv[c, j] = eye - lower
    for _ in range(5):
        for cj in chunk_heads:
            m_b = mpow[cj].astype(BF16)
            mpow[cj] = jnp.dot(m_b, m_b, preferred_element_type=F32)
        for cj in chunk_heads:
            tinv[cj] = tinv[cj] + jnp.dot(tinv[cj].astype(BF16), mpow[cj].astype(BF16), preferred_element_type=F32)
    u, w, a_intra, q_dec, k_dec_t, g_last = {}, {}, {}, {}, {}, {}
    for c, j in chunk_heads:
        k_h = k_ref[rows(c), hcols(j // 2)]
        beta = beta_ref[0, rows(c), j:j + 1]
        gc = gc_all[c][:, j:j + 1]
        egc = jnp.exp(gc)
        gc_last = gc[C - 1:C, :]
        tinv_b = tinv[c, j].astype(BF16)
        u[c, j] = jnp.dot(tinv_b, (v_ref[rows(c), hcols(j)] * beta).astype(BF16), preferred_element_type=F32)
        w[c, j] = jnp.dot(tinv_b, (k_h * (beta * egc)).astype(BF16), preferred_element_type=F32).astype(BF16)
        a_intra[c, j] = jnp.where(causal, qk[c, j // 2] * decay[c, j], 0.0).astype(BF16)
        q_dec[c, j] = (q_ref[rows(c), hcols(j // 2)] * egc).astype(BF16)
        k_dec_t[c, j] = (k_h * jnp.exp(gc_last - gc)).T.astype(BF16)
        g_last[c, j] = jnp.exp(gc_last)

    heads = range(GDN_HEAD_GROUP)
    state = [state_ref[j] for j in heads]
    for c in range(GDN_CHUNK_GROUP):
        state_b = [state[j].astype(BF16) for j in heads]
        v_new = [u[c, j] - jnp.dot(w[c, j], state_b[j], preferred_element_type=F32) for j in heads]
        v_new_b = [v.astype(BF16) for v in v_new]
        state = [state[j] * g_last[c, j] + jnp.dot(k_dec_t[c, j], v_new_b[j], preferred_element_type=F32)
                 for j in heads]
        for j in heads:
            o = jnp.dot(q_dec[c, j], state_b[j], preferred_element_type=F32) + jnp.dot(
                a_intra[c, j], v_new_b[j], preferred_element_type=F32)
            o = o * lax.rsqrt(jnp.mean(o * o, axis=-1, keepdims=True) + RMS_EPS) * norm_g
            z = z_ref[rows(c), hcols(j)]
            o_ref[rows(c), hcols(j)] = (o * (z * jax.nn.sigmoid(z))).astype(o_ref.dtype)
    for j in heads:
        state_ref[j] = state[j]


def _gdn_scan(qkv, h_main, beta, g, norm_g, *, batch):
    n = qkv.shape[0]
    seq = n // batch
    tr = GDN_CHUNK_GROUP * GDN_CHUNK
    nt = seq // tr
    hg = GDN_HEAD_GROUP
    kw = (hg // 2) * GDN_HEAD_DIM
    vw = hg * GDN_HEAD_DIM
    row = lambda b, h, t: b * nt + t
    return pl.pallas_call(
        _gdn_scan_kernel,
        grid=(batch, GDN_V_HEADS // hg, nt),
        in_specs=[pl.BlockSpec((tr, kw), lambda b, h, t: (row(b, h, t), h)),
                  pl.BlockSpec((tr, kw), lambda b, h, t: (row(b, h, t), GDN_KEY_DIM // kw + h)),
                  pl.BlockSpec((tr, vw), lambda b, h, t: (row(b, h, t), 2 * GDN_KEY_DIM // vw + h)),
                  pl.BlockSpec((tr, vw), lambda b, h, t: (row(b, h, t), GDN_CONV_DIM // vw + h)),
                  pl.BlockSpec((1, tr, hg), lambda b, h, t: (h, row(b, h, t), 0)),
                  pl.BlockSpec((1, tr, hg), lambda b, h, t: (h, row(b, h, t), 0)),
                  pl.BlockSpec((1, GDN_HEAD_DIM), lambda b, h, t: (0, 0))],
        out_specs=pl.BlockSpec((tr, vw), lambda b, h, t: (row(b, h, t), h)),
        out_shape=jax.ShapeDtypeStruct((n, GDN_VAL_DIM), BF16),
        scratch_shapes=[pltpu.VMEM((hg, GDN_HEAD_DIM, GDN_HEAD_DIM), F32)],
        compiler_params=_cparams("parallel", "parallel", "arbitrary"),
    )(qkv, qkv, qkv, h_main, beta, g, norm_g.reshape(1, GDN_HEAD_DIM))


def _gdn_in_weights(w_in):
    n_main = GDN_CONV_DIM + GDN_VAL_DIM
    n_ba = 2 * GDN_V_HEADS
    pad = jnp.zeros((w_in.shape[0], LANES - n_ba), w_in.dtype)
    tail = jnp.concatenate([w_in[:, n_main + n_ba:], w_in[:, n_main:n_main + n_ba], pad], axis=1)
    return w_in[:, :n_main], tail


def _gdn_mixer_pallas(h_main, h_tail, mix_params, *, batch):
    conv_w, a_log, dt_bias, norm_g = mix_params
    seq = h_main.shape[0] // batch
    qkv = _gdn_prep(h_main, conv_w, seq=seq)
    beta, g = _gdn_gates(h_tail, a_log, dt_bias)
    return _gdn_scan(qkv, h_main, beta, g, norm_g, batch=batch)


DSA_TQ = 128
DSA_TK = 512
DSA_HEAD_GROUP = 8
DSA_KEY_W = DSA_KV_LORA + LANES
MASK_BIAS = -2e30
M_INIT = -1e30
I32_MIN = -2 ** 31


def _rope_partner(x, half):
    ax = x.ndim - 1
    n = x.shape[ax]
    lane = lax.broadcasted_iota(I32, x.shape, ax) % (2 * half)
    return jnp.where(lane < half, -pltpu.roll(x, n - half, axis=ax), pltpu.roll(x, half, axis=ax))


def _dsa_prep_kernel(ql_ref, kv_ref, ik_ref, misc_ref, cos_ref, sin_ref, cosi_ref, sini_ref,
                     qg_ref, kvg_ref, ikg_ref, ikb_ref,
                     cq_ref, key_ref, iko_ref, iw_ref):
    half = DSA_ROPE_DIM // 2
    ql = ql_ref[...]
    cq_ref[...] = (ql * lax.rsqrt(jnp.mean(ql * ql, axis=-1, keepdims=True) + RMS_EPS) * qg_ref[...]).astype(BF16)
    kv = kv_ref[...]
    key_ref[:, :DSA_KV_LORA] = (
        kv * lax.rsqrt(jnp.mean(kv * kv, axis=-1, keepdims=True) + RMS_EPS) * kvg_ref[...]).astype(BF16)
    misc = misc_ref[...]
    lane = lax.broadcasted_iota(I32, misc.shape, 1)
    kr = jnp.where(lane < DSA_ROPE_DIM, misc * cos_ref[...] + _rope_partner(misc, half) * sin_ref[...], 0.0)
    key_ref[:, DSA_KV_LORA:] = kr.astype(BF16)
    iw = misc * (IDX_HEADS ** -0.5 * IDX_DIM ** -0.5)
    for h in range(IDX_HEADS):
        iw_ref[h] = iw[:, DSA_ROPE_DIM + h:DSA_ROPE_DIM + h + 1]
    ik = _ln_rows(ik_ref[...], ikg_ref[...], ikb_ref[...])
    iko_ref[...] = (ik * cosi_ref[...] + _rope_partner(ik, half) * sini_ref[...]).astype(BF16)


def _dsa_prep(h, tabs, q_norm_g, kv_norm_g, ik_g, ik_b, *, seq):
    n = h.shape[0]
    tq = min(256, seq)
    nt = seq // tq
    cos4, sin4, cosi, sini = tabs
    tab = pl.BlockSpec((tq, LANES), lambda i: (i % nt, 0))
    vec = lambda w: pl.BlockSpec((1, w), lambda i: (0, 0))
    blk = lambda w, j: pl.BlockSpec((tq, w), lambda i: (i, j))
    return pl.pallas_call(
        _dsa_prep_kernel,
        grid=(n // tq,),
        in_specs=[blk(DSA_Q_LORA, 0), blk(DSA_KV_LORA, 3), blk(LANES, 16), blk(LANES, 17), tab, tab, tab, tab,
                  vec(DSA_Q_LORA), vec(DSA_KV_LORA), vec(IDX_DIM), vec(IDX_DIM)],
        out_specs=[blk(DSA_Q_LORA, 0), blk(DSA_KEY_W, 0), blk(IDX_DIM, 0),
                   pl.BlockSpec((IDX_HEADS, tq, 1), lambda i: (0, i, 0))],
        out_shape=[jax.ShapeDtypeStruct((n, DSA_Q_LORA), BF16), jax.ShapeDtypeStruct((n, DSA_KEY_W), BF16),
                   jax.ShapeDtypeStruct((n, IDX_DIM), BF16), jax.ShapeDtypeStruct((IDX_HEADS, n, 1), F32)],
        compiler_params=_cparams("parallel"),
    )(h, h, h, h, cos4, sin4, cosi, sini, q_norm_g.reshape(1, -1), kv_norm_g.reshape(1, -1),
      ik_g.reshape(1, -1), ik_b.reshape(1, -1))


def _dsa_qpost_kernel(qn_ref, qri_ref, wuk_ref, cos_ref, sin_ref, cosi_ref, sini_ref, q_ref, iq_ref):
    half = DSA_ROPE_DIM // 2
    tq = qn_ref.shape[0]
    n_rope = DSA_HEADS * DSA_ROPE_DIM
    scale = (DSA_NOPE_DIM + DSA_ROPE_DIM) ** -0.5
    qr = qri_ref[:, :n_rope]
    cos = jnp.tile(cos_ref[...], (1, n_rope // LANES))
    sin = jnp.tile(sin_ref[...], (1, n_rope // LANES))
    qr = (qr * cos + _rope_partner(qr, half) * sin) * scale
    zeros = jnp.zeros((tq, LANES - DSA_ROPE_DIM), F32)
    cosi, sini = cosi_ref[...], sini_ref[...]
    for h in range(DSA_HEADS):
        q_ref[h, :, :DSA_KV_LORA] = (jnp.dot(
            qn_ref[:, h * DSA_NOPE_DIM:(h + 1) * DSA_NOPE_DIM], wuk_ref[h],
            preferred_element_type=F32) * scale).astype(BF16)
        q_ref[h, :, DSA_KV_LORA:] = jnp.concatenate(
            [qr[:, h * DSA_ROPE_DIM:(h + 1) * DSA_ROPE_DIM], zeros], axis=1).astype(BF16)
        iq = qri_ref[:, n_rope + h * IDX_DIM:n_rope + (h + 1) * IDX_DIM]
        iq_ref[h] = (iq * cosi + _rope_partner(iq, half) * sini).astype(BF16)


def _dsa_qpost(qn, qri, wuk_t, tabs, *, seq):
    n = qn.shape[0]
    tq = min(128, seq)
    nt = seq // tq
    cos4, sin4, cosi, sini = tabs
    tab = pl.BlockSpec((tq, LANES), lambda i: (i % nt, 0))
    row = lambda w: pl.BlockSpec((tq, w), lambda i: (i, 0))
    hm = lambda w: pl.BlockSpec((DSA_HEADS, tq, w), lambda i: (0, i, 0))
    return pl.pallas_call(
        _dsa_qpost_kernel,
        grid=(n // tq,),
        in_specs=[row(qn.shape[1]), row(qri.shape[1]),
                  pl.BlockSpec(wuk_t.shape, lambda i: (0, 0, 0)), tab, tab, tab, tab],
        out_specs=[hm(DSA_KEY_W), hm(IDX_DIM)],
        out_shape=[jax.ShapeDtypeStruct((DSA_HEADS, n, DSA_KEY_W), BF16),
                   jax.ShapeDtypeStruct((IDX_HEADS, n, IDX_DIM), BF16)],
        compiler_params=_cparams("parallel"),
    )(qn, qri, wuk_t, cos4, sin4, cosi, sini)


def _dsa_index_kernel(iq_ref, iw_ref, ik_ref, bias_ref, key_ref, *, topk):
    qi = pl.program_id(1)
    tq = iq_ref.shape[1]
    tk = DSA_TK
    hg = DSA_HEAD_GROUP
    n_kt = ((qi + 1) * tq + tk - 1) // tk
    qpos = qi * tq + lax.broadcasted_iota(I32, (tq, tk), 0)
    lane = lax.broadcasted_iota(I32, (tq, tk), 1)

    def score_tile(kt, c):
        k0 = pl.multiple_of(kt * tk, tk)
        keys = ik_ref[pl.ds(k0, tk), :]
        acc = jnp.zeros((tq, tk), F32)
        for g in range(IDX_HEADS // hg):
            lhs = iq_ref[g * hg:(g + 1) * hg].reshape(hg * tq, IDX_DIM)
            s = lax.dot_general(lhs, keys, (((1,), (1,)), ((), ())), preferred_element_type=F32)
            s = jnp.maximum(s, 0.0).reshape(hg, tq, tk) * iw_ref[g * hg:(g + 1) * hg]
            acc = acc + jnp.sum(s, axis=0)
        acc = jnp.where(k0 + lane <= qpos, acc, -jnp.inf)
        bits = pltpu.bitcast(acc, I32)
        key_ref[:, pl.ds(k0, tk)] = jnp.where(bits < 0, bits ^ 0x7FFFFFFF, bits)
        return c

    lax.fori_loop(0, n_kt, score_tile, 0)

    res = jnp.zeros((tq, 1), I32)
    for bit in range(31, -1, -1):
        bit_c = jnp.int32(I32_MIN if bit == 31 else (1 << bit))
        cand = (res | bit_c) ^ jnp.int32(I32_MIN)

        def count_tile(kt, cnt, cand=cand):
            blk = key_ref[:, pl.ds(pl.multiple_of(kt * tk, tk), tk)]
            ge = (blk >= cand).astype(I32)
            part = ge[:, 0:LANES]
            for j in range(1, tk // LANES):
                part = part + ge[:, j * LANES:(j + 1) * LANES]
            return cnt + part

        cnt = lax.fori_loop(0, n_kt, count_tile, jnp.zeros((tq, LANES), I32))
        total = jnp.sum(cnt, axis=1, keepdims=True)
        res = jnp.where(total >= topk, res | bit_c, res)
    thr = res ^ jnp.int32(I32_MIN)

    bias_ref[...] = jnp.full(bias_ref.shape, MASK_BIAS, bias_ref.dtype)

    def bias_tile(kt, c):
        k0 = pl.multiple_of(kt * tk, tk)
        sel = (key_ref[:, pl.ds(k0, tk)] >= thr) & (k0 + lane <= qpos)
        bias_ref[:, pl.ds(k0, tk)] = jnp.where(sel, 0.0, MASK_BIAS).astype(bias_ref.dtype)
        return c

    lax.fori_loop(0, n_kt, bias_tile, 0)


def _dsa_index(iq, iw, ik, *, batch, topk):
    n = iq.shape[1]
    seq = n // batch
    tq = min(DSA_TQ, seq)
    nq = seq // tq
    return pl.pallas_call(
        functools.partial(_dsa_index_kernel, topk=topk),
        grid=(batch, nq),
        in_specs=[pl.BlockSpec((IDX_HEADS, tq, IDX_DIM), lambda b, i: (0, b * nq + i, 0)),
                  pl.BlockSpec((IDX_HEADS, tq, 1), lambda b, i: (0, b * nq + i, 0)),
                  pl.BlockSpec((seq, IDX_DIM), lambda b, i: (b, 0))],
        out_specs=pl.BlockSpec((tq, seq), lambda b, i: (b * nq + i, 0)),
        out_shape=jax.ShapeDtypeStruct((n, seq), BF16),
        scratch_shapes=[pltpu.VMEM((tq, seq), I32)],
        compiler_params=_cparams("parallel", "parallel"),
    )(iq, iw, ik)


def _dsa_attn_kernel(q_ref, key_ref, bias_ref, wuv_ref, o_ref, m_ref, l_ref, acc_ref):
    qi = pl.program_id(1)
    kt = pl.program_id(2)
    tq = q_ref.shape[1]
    tk = key_ref.shape[0]
    hg = DSA_HEAD_GROUP
    rows = hg * tq
    last = ((qi + 1) * tq - 1) // tk

    @pl.when(kt == 0)
    def _():
        m_ref[...] = jnp.full(m_ref.shape, M_INIT, F32)
        l_ref[...] = jnp.zeros_like(l_ref)
        acc_ref[...] = jnp.zeros_like(acc_ref)

    @pl.when(kt <= last)
    def _():
        bias = bias_ref[...].astype(F32)
        keys = key_ref[...]
        vals = key_ref[:, :DSA_KV_LORA]
        for g in range(DSA_HEADS // hg):
            r = slice(g * rows, (g + 1) * rows)
            lhs = q_ref[g * hg:(g + 1) * hg].reshape(rows, DSA_KEY_W)
            s = lax.dot_general(lhs, keys, (((1,), (1,)), ((), ())), preferred_element_type=F32)
            s = (s.reshape(hg, tq, tk) + bias).reshape(rows, tk)
            m_prev = m_ref[r]
            m_cur = jnp.maximum(m_prev, jnp.max(s, axis=1, keepdims=True))
            alpha = jnp.exp(m_prev - m_cur)
            p = jnp.exp(s - m_cur[:, :1])
            l_ref[r] = alpha * l_ref[r] + jnp.sum(p, axis=1, keepdims=True)
            acc_ref[r] = acc_ref[r] * alpha[:, :1] + jnp.dot(p.astype(BF16), vals, preferred_element_type=F32)
            m_ref[r] = m_cur

    @pl.when(kt == pl.num_programs(2) - 1)
    def _():
        for h in range(DSA_HEADS):
            r = slice(h * tq, (h + 1) * tq)
            o_lat = (acc_ref[r] / l_ref[r][:, :1]).astype(BF16)
            o_ref[:, h * DSA_V_DIM:(h + 1) * DSA_V_DIM] = jnp.dot(
                o_lat, wuv_ref[h], preferred_element_type=F32).astype(o_ref.dtype)


def _dsa_attention(q, keys, bias, wuv, *, batch):
    n = keys.shape[0]
    seq = n // batch
    tq = min(DSA_TQ, seq)
    tk = min(DSA_TK, seq)
    nq, nk = seq // tq, seq // tk
    last = lambda i: ((i + 1) * tq - 1) // tk
    return pl.pallas_call(
        _dsa_attn_kernel,
        grid=(batch, nq, nk),
        in_specs=[pl.BlockSpec((DSA_HEADS, tq, DSA_KEY_W), lambda b, i, k: (0, b * nq + i, 0)),
                  pl.BlockSpec((tk, DSA_KEY_W), lambda b, i, k: (b * nk + jnp.minimum(k, last(i)), 0)),
                  pl.BlockSpec((tq, tk), lambda b, i, k: (b * nq + i, jnp.minimum(k, last(i)))),
                  pl.BlockSpec(wuv.shape, lambda b, i, k: (0, 0, 0))],
        out_specs=pl.BlockSpec((tq, DSA_HEADS * DSA_V_DIM), lambda b, i, k: (b * nq + i, 0)),
        out_shape=jax.ShapeDtypeStruct((n, DSA_HEADS * DSA_V_DIM), BF16),
        scratch_shapes=[pltpu.VMEM((DSA_HEADS * tq, LANES), F32), pltpu.VMEM((DSA_HEADS * tq, LANES), F32),
                        pltpu.VMEM((DSA_HEADS * tq, DSA_KV_LORA), F32)],
        compiler_params=_cparams("parallel", "parallel", "arbitrary"),
    )(q, keys, bias, wuv)


def _dsa_rope_tables(seq):
    inv_freq = ROPE_THETA ** (-jnp.arange(0, DSA_ROPE_DIM, 2, dtype=F32) / DSA_ROPE_DIM)
    ang = jnp.arange(seq, dtype=F32)[:, None] * inv_freq[None, :]
    c, s = jnp.cos(ang), jnp.sin(ang)
    one, zero = jnp.ones_like(c), jnp.zeros_like(c)
    return (jnp.concatenate([c, c, c, c], axis=1), jnp.concatenate([s, s, s, s], axis=1),
            jnp.concatenate([c, c, one, one], axis=1), jnp.concatenate([s, s, zero, zero], axis=1))


def _dsa_in_weight(w_in):
    o1 = DSA_Q_LORA
    o2 = o1 + DSA_KV_LORA
    o3 = o2 + DSA_ROPE_DIM
    o4 = o3 + IDX_DIM
    o5 = o4 + IDX_HEADS
    pad = jnp.zeros((w_in.shape[0], LANES - DSA_ROPE_DIM - IDX_HEADS), w_in.dtype)
    return jnp.concatenate([w_in[:, :o1], w_in[:, o5:], w_in[:, o1:o2], w_in[:, o3:o4], w_in[:, o2:o3],
                            w_in[:, o4:o5], pad], axis=1)


def _dsa_mixer_pallas(h, mix_params, *, batch):
    q_norm_g, kv_norm_g, w_q_up, w_idx_q, ik_g, ik_b, w_kv_up = mix_params
    n = h.shape[0]
    seq = n // batch
    tabs = _dsa_rope_tables(seq)
    cq, keys, ik, iw = _dsa_prep(h, tabs, q_norm_g, kv_norm_g, ik_g, ik_b, seq=seq)
    wq = w_q_up.reshape(DSA_Q_LORA, DSA_HEADS, DSA_NOPE_DIM + DSA_ROPE_DIM).astype(BF16)
    w_nope = wq[:, :, :DSA_NOPE_DIM].reshape(DSA_Q_LORA, -1)
    w_ri = jnp.concatenate([wq[:, :, DSA_NOPE_DIM:].reshape(DSA_Q_LORA, -1), w_idx_q.astype(BF16)], axis=1)
    qn = _matmul([cq], [w_nope], tm=1024, tn=512, out_dtype=BF16)
    qri = _matmul([cq], [w_ri], tm=1024, tn=512, out_dtype=F32)
    w_kv = w_kv_up.reshape(DSA_KV_LORA, DSA_HEADS, DSA_NOPE_DIM + DSA_V_DIM).astype(BF16)
    wuk_t = jnp.transpose(w_kv[:, :, :DSA_NOPE_DIM], (1, 2, 0))
    wuv = jnp.transpose(w_kv[:, :, DSA_NOPE_DIM:], (1, 0, 2))
    q, iq = _dsa_qpost(qn, qri, wuk_t, tabs, seq=seq)
    bias = _dsa_index(iq, iw, ik, batch=batch, topk=min(DSA_TOPK_MAX, seq // 4))
    return _dsa_attention(q, keys, bias, wuv, batch=batch)


def _trunk(x, mem, mem_ln_g, mem_ln_b, w_mem_kv, layers):
    B, T, D = x.shape
    n = B * T
    nm = mem.shape[1]
    mem_n = _ln_plain(mem.reshape(B * nm, D), mem_ln_g, mem_ln_b)
    mem_kv = _matmul([mem_n], [w_mem_kv.astype(BF16)], tm=512, tn=512, out_dtype=BF16).reshape(B, nm, 2 * MEM_Q)
    x_f = x.reshape(n, D)
    x_b = x_f.astype(BF16)
    for i, (w_in, mix_params, w_out, ln1_g, ln1_b, ffn_params, ln2_g, ln2_b) in enumerate(layers):
        kind = i % N_MIXERS
        n_in = w_in.shape[1]
        n_mix = n_in - MEM_Q
        if kind == 2:
            w_in_b = _dsa_in_weight(w_in).astype(BF16)
            h = _matmul([x_b], [w_in_b], tm=1024, tn=_pick_tn(w_in_b.shape[1]), out_dtype=F32)
            y_mix = _dsa_mixer_pallas(h, mix_params, batch=B)
            y_mem = _mem_attention(h, mem_kv, batch=B, col_block=2)
        elif kind == 0:
            w_main, w_tail = _gdn_in_weights(w_in)
            h_main = _matmul([x_b], [w_main.astype(BF16)], tm=1024, tn=512, out_dtype=F32)
            h_tail = _matmul([x_b], [w_tail.astype(BF16)], tm=1024, tn=w_tail.shape[1], out_dtype=F32)
            y_mix = _gdn_mixer_pallas(h_main, h_tail, mix_params, batch=B)
            y_mem = _mem_attention(h_tail, mem_kv, batch=B)
        else:
            w_in_b = w_in.astype(BF16)
            gw = 3 * DIL_OUT
            h_groups = [_matmul([x_b], [w_in_b[:, g * gw:(g + 1) * gw]], tm=1024, tn=512, out_dtype=BF16)
                        for g in range(len(DIL_GROUPS))]
            q_mem = _matmul([x_b], [w_in_b[:, n_mix:]], tm=1024, tn=MEM_Q, out_dtype=BF16)
            y_mix = _dilated_mixer_pallas(h_groups, batch=B)
            y_mem = _mem_attention(q_mem, mem_kv, batch=B)
        w_out_b = w_out.astype(BF16)
        n_out = y_mix.shape[1]
        y = _matmul([y_mix, y_mem], [w_out_b[:n_out], w_out_b[n_out:]], tm=1024, tn=512, out_dtype=F32)
        x_f, x_lin = _ln_residual(x_f, y, ln1_g, ln1_b)
        x_f, x_b = _moe_block(x_f, x_lin, *ffn_params, ln2_g, ln2_b)
    return x_f.reshape(B, T, D)


def _pick_tn(n):
    for tn in (512, 384, 256, 128):
        if n % tn == 0:
            return tn
    return n


def kernel(x, mem, mem_ln_g, mem_ln_b, w_mem_kv, w_in_0, conv_w_0, a_log_0, dt_bias_0, gdn_norm_g_0, w_out_0, ln1_g_0, ln1_b_0, router_w_0, router_b_0, moe_w1_0, moe_b1_0, moe_w2_0, moe_b2_0, ln2_g_0, ln2_b_0, w_in_1, w_out_1, ln1_g_1, ln1_b_1, router_w_1, router_b_1, moe_w1_1, moe_b1_1, moe_w2_1, moe_b2_1, ln2_g_1, ln2_b_1, w_in_2, q_norm_g_2, kv_norm_g_2, w_q_up_2, w_idx_q_2, idx_k_norm_g_2, idx_k_norm_b_2, w_kv_up_2, w_out_2, ln1_g_2, ln1_b_2, router_w_2, router_b_2, moe_w1_2, moe_b1_2, moe_w2_2, moe_b2_2, ln2_g_2, ln2_b_2, w_in_3, conv_w_3, a_log_3, dt_bias_3, gdn_norm_g_3, w_out_3, ln1_g_3, ln1_b_3, router_w_3, router_b_3, moe_w1_3, moe_b1_3, moe_w2_3, moe_b2_3, ln2_g_3, ln2_b_3):
    layers = (
        (w_in_0, (conv_w_0, a_log_0, dt_bias_0, gdn_norm_g_0), w_out_0, ln1_g_0, ln1_b_0,
         (router_w_0, router_b_0, moe_w1_0, moe_b1_0, moe_w2_0, moe_b2_0), ln2_g_0, ln2_b_0),
        (w_in_1, (), w_out_1, ln1_g_1, ln1_b_1,
         (router_w_1, router_b_1, moe_w1_1, moe_b1_1, moe_w2_1, moe_b2_1), ln2_g_1, ln2_b_1),
        (w_in_2, (q_norm_g_2, kv_norm_g_2, w_q_up_2, w_idx_q_2, idx_k_norm_g_2, idx_k_norm_b_2, w_kv_up_2), w_out_2,
         ln1_g_2, ln1_b_2, (router_w_2, router_b_2, moe_w1_2, moe_b1_2, moe_w2_2, moe_b2_2), ln2_g_2, ln2_b_2),
        (w_in_3, (conv_w_3, a_log_3, dt_bias_3, gdn_norm_g_3), w_out_3, ln1_g_3, ln1_b_3,
         (router_w_3, router_b_3, moe_w1_3, moe_b1_3, moe_w2_3, moe_b2_3), ln2_g_3, ln2_b_3),
    )
    return _trunk(x, mem, mem_ln_g, mem_ln_b, w_mem_kv, layers)
```

```python
import functools
import math

import jax
import jax.numpy as jnp
from jax import lax
from jax.experimental import pallas as pl
from jax.experimental.pallas import tpu as pltpu

F32 = jnp.float32
BF16 = jnp.bfloat16
I32 = jnp.int32
HIGHEST = lax.Precision.HIGHEST

V7X_VMEM_LIMIT_BYTES = 56 * 1024 * 1024

DEPTH = 4
N_MIXERS = 3
MEM_HEADS = 4
MEM_HEAD_DIM = 128
MEM_Q = MEM_HEADS * MEM_HEAD_DIM

GDN_K_HEADS = 16
GDN_V_HEADS = 32
GDN_HEAD_DIM = 128
GDN_KEY_DIM = GDN_K_HEADS * GDN_HEAD_DIM
GDN_VAL_DIM = GDN_V_HEADS * GDN_HEAD_DIM
GDN_CONV_DIM = 2 * GDN_KEY_DIM + GDN_VAL_DIM
GDN_CONV_TAPS = 4
GDN_CHUNK = 64

DIL_GROUPS = ((128, 1), (512, 4), (2048, 16))
DIL_HEADS = 16
DIL_HEAD_DIM = 128
DIL_OUT = DIL_HEADS * DIL_HEAD_DIM

DSA_HEADS = 32
DSA_NOPE_DIM = 128
DSA_ROPE_DIM = 64
DSA_V_DIM = 128
DSA_Q_LORA = 1024
DSA_KV_LORA = 512
IDX_HEADS = 32
IDX_DIM = 128
IDX_ROPE_DIM = 64
DSA_TOPK_MAX = 256
ROPE_THETA = 10000.0

N_EXPERTS = 32
TOP_K = 4
MOE_FF = 384
SWIGLU_ALPHA = 1.702
SWIGLU_LIMIT = 7.0

DEEPNORM_ALPHA = (2 * DEPTH) ** 0.25
LN_EPS = 1e-5
RMS_EPS = 1e-6


def _cparams(*sem):
    return pltpu.CompilerParams(dimension_semantics=sem, vmem_limit_bytes=V7X_VMEM_LIMIT_BYTES)


def _mm_kernel(*refs, n_pairs):
    o_ref = refs[-1]
    acc = None
    for p in range(n_pairs):
        d = jnp.dot(refs[p][...], refs[n_pairs + p][...], preferred_element_type=F32)
        acc = d if acc is None else acc + d
    o_ref[...] = acc.astype(o_ref.dtype)


def _matmul(a_list, w_list, *, tm, tn, out_dtype):
    m = a_list[0].shape[0]
    n = w_list[0].shape[1]
    tm = min(tm, m)
    tn = min(tn, n)
    assert m % tm == 0 and n % tn == 0, (m, n, tm, tn)
    in_specs = [pl.BlockSpec((tm, a.shape[1]), lambda i, j: (i, 0)) for a in a_list]
    in_specs += [pl.BlockSpec((w.shape[0], tn), lambda i, j: (0, j)) for w in w_list]
    return pl.pallas_call(
        functools.partial(_mm_kernel, n_pairs=len(a_list)),
        grid=(m // tm, n // tn),
        in_specs=in_specs,
        out_specs=pl.BlockSpec((tm, tn), lambda i, j: (i, j)),
        out_shape=jax.ShapeDtypeStruct((m, n), out_dtype),
        compiler_params=_cparams("parallel", "arbitrary"),
    )(*a_list, *w_list)


def _ln_rows(v, g, b):
    mu = jnp.mean(v, axis=-1, keepdims=True)
    c = v - mu
    var = jnp.mean(c * c, axis=-1, keepdims=True)
    return c * lax.rsqrt(var + LN_EPS) * g + b


def _ln_res_kernel(x_ref, y_ref, g_ref, b_ref, of_ref, ol_ref):
    out = _ln_rows(DEEPNORM_ALPHA * x_ref[...] + y_ref[...], g_ref[...], b_ref[...])
    of_ref[...] = out
    spr = out.shape[1] // LANES
    _to_row_linear(ol_ref, out, spr, spr)


def _ln_residual(x, y, g, b, *, tr=128):
    n, d = x.shape
    tr = min(tr, n)
    spr = d // LANES
    row = pl.BlockSpec((tr, d), lambda i: (i, 0))
    vec = pl.BlockSpec((1, d), lambda i: (0, 0))
    return pl.pallas_call(
        _ln_res_kernel,
        grid=(n // tr,),
        in_specs=[row, row, vec, vec],
        out_specs=[row, pl.BlockSpec((tr * spr, LANES), lambda i: (i, 0))],
        out_shape=[jax.ShapeDtypeStruct((n, d), F32), jax.ShapeDtypeStruct((n * spr, LANES), F32)],
        compiler_params=_cparams("parallel"),
    )(x, y, g.reshape(1, d), b.reshape(1, d))


def _ln_plain_kernel(x_ref, g_ref, b_ref, ob_ref):
    ob_ref[...] = _ln_rows(x_ref[...], g_ref[...], b_ref[...]).astype(BF16)


def _ln_plain(x, g, b, *, tr=128):
    n, d = x.shape
    tr = min(tr, n)
    row = pl.BlockSpec((tr, d), lambda i: (i, 0))
    vec = pl.BlockSpec((1, d), lambda i: (0, 0))
    return pl.pallas_call(
        _ln_plain_kernel,
        grid=(n // tr,),
        in_specs=[row, vec, vec],
        out_specs=row,
        out_shape=jax.ShapeDtypeStruct((n, d), BF16),
        compiler_params=_cparams("parallel"),
    )(x, g.reshape(1, d), b.reshape(1, d))


def _mem_attn_kernel(q_ref, kv_ref, o_ref):
    scale = MEM_HEAD_DIM ** -0.5
    for h in range(MEM_HEADS):
        lo = h * MEM_HEAD_DIM
        q = q_ref[:, lo:lo + MEM_HEAD_DIM].astype(BF16)
        k = kv_ref[0, :, lo:lo + MEM_HEAD_DIM]
        v = kv_ref[0, :, MEM_Q + lo:MEM_Q + lo + MEM_HEAD_DIM]
        s = lax.dot_general(q, k, (((1,), (1,)), ((), ())), preferred_element_type=F32) * scale
        m = jnp.max(s, axis=-1, keepdims=True)
        p = jnp.exp(s - m)
        l = jnp.sum(p, axis=-1, keepdims=True)
        p = (p / l).astype(BF16)
        o_ref[:, lo:lo + MEM_HEAD_DIM] = jnp.dot(p, v, preferred_element_type=F32).astype(o_ref.dtype)


def _mem_attention(q, mem_kv, *, batch, col_block=0, tq=512):
    n = q.shape[0]
    t = n // batch
    tq = min(tq, t)
    nm = mem_kv.shape[1]
    return pl.pallas_call(
        _mem_attn_kernel,
        grid=(batch, t // tq),
        in_specs=[pl.BlockSpec((tq, MEM_Q), lambda b, i: (b * (t // tq) + i, col_block)),
                  pl.BlockSpec((1, nm, 2 * MEM_Q), lambda b, i: (b, 0, 0))],
        out_specs=pl.BlockSpec((tq, MEM_Q), lambda b, i: (b * (t // tq) + i, 0)),
        out_shape=jax.ShapeDtypeStruct((n, MEM_Q), BF16),
        compiler_params=_cparams("parallel", "parallel"),
    )(q, mem_kv)


MOE_TILE = 256
ROUTER_TILE = 512
COMBINE_TILE = 128


def _router_kernel(x_ref, wt_ref, b_ref, idx_ref, gate_ref, rank_ref, cnt_ref, carry_ref):
    tr = x_ref.shape[0]

    @pl.when(pl.program_id(0) == 0)
    def _():
        carry_ref[...] = jnp.zeros_like(carry_ref)

    logits = lax.dot_general(wt_ref[...], x_ref[...], (((1,), (1,)), ((), ())),
                             precision=HIGHEST, preferred_element_type=F32) + b_ref[...]
    e_iota = lax.broadcasted_iota(I32, (N_EXPERTS, tr), 0)
    work = logits
    vals, onehots = [], []
    for k in range(TOP_K):
        m = jnp.max(work, axis=0, keepdims=True)
        idx = jnp.min(jnp.where(work == m, e_iota, N_EXPERTS), axis=0, keepdims=True)
        oh = e_iota == idx
        vals.append(m)
        onehots.append(oh)
        idx_ref[k:k + 1, :] = idx
        work = jnp.where(oh, -jnp.inf, work)
    exps = [jnp.exp(v - vals[0]) for v in vals]
    denom = exps[0] + exps[1] + exps[2] + exps[3]
    for k in range(TOP_K):
        gate_ref[k:k + 1, :] = exps[k] / denom
    mask = (onehots[0] | onehots[1] | onehots[2] | onehots[3])
    r_i = lax.broadcasted_iota(I32, (tr, tr), 0)
    c_i = lax.broadcasted_iota(I32, (tr, tr), 1)
    before = (r_i < c_i).astype(BF16)
    excl = jnp.dot(mask.astype(BF16), before, preferred_element_type=F32)
    rank = carry_ref[...] + excl
    for k in range(TOP_K):
        rank_ref[k:k + 1, :] = jnp.sum(jnp.where(onehots[k], rank, 0.0), axis=0, keepdims=True).astype(I32)
    carry_ref[...] += jnp.sum(mask.astype(F32), axis=1, keepdims=True)
    cnt_ref[...] = carry_ref[...]


def _router(x, router_w, router_b):
    n, d = x.shape
    tr = min(ROUTER_TILE, n)
    slab = pl.BlockSpec((TOP_K, tr), lambda i: (0, i))
    return pl.pallas_call(
        _router_kernel,
        grid=(n // tr,),
        in_specs=[pl.BlockSpec((tr, d), lambda i: (i, 0)),
                  pl.BlockSpec((N_EXPERTS, d), lambda i: (0, 0)),
                  pl.BlockSpec((N_EXPERTS, 1), lambda i: (0, 0))],
        out_specs=[slab, slab, slab, pl.BlockSpec((N_EXPERTS, 1), lambda i: (0, 0))],
        out_shape=[jax.ShapeDtypeStruct((TOP_K, n), I32), jax.ShapeDtypeStruct((TOP_K, n), F32),
                   jax.ShapeDtypeStruct((TOP_K, n), I32), jax.ShapeDtypeStruct((N_EXPERTS, 1), F32)],
        scratch_shapes=[pltpu.VMEM((N_EXPERTS, 1), F32)],
        compiler_params=_cparams("arbitrary"),
    )(x, router_w.T, router_b.reshape(N_EXPERTS, 1))


LANES = 128
ROW_PAD = 8


def _gather_rows(idx_ref, base, src_hbm, dst_ref, sem, n_rows, spr, src_pitch, dst_pitch):
    unroll = 8
    assert n_rows % unroll == 0

    def body(blk, c):
        for u in range(unroll):
            r = blk * unroll + u
            src = pl.multiple_of(idx_ref[base + r] * src_pitch, SUBLANES)
            dst = pl.multiple_of(r * dst_pitch, SUBLANES)
            pltpu.make_async_copy(src_hbm.at[pl.ds(src, spr)], dst_ref.at[pl.ds(dst, spr)], sem).start()
        return c
    lax.fori_loop(0, n_rows // unroll, body, 0)


def _from_row_linear(ref, first, n_rows, spr, pitch):
    return jnp.concatenate([ref[pl.ds(first * pitch + c, n_rows, stride=pitch), :] for c in range(spr)], axis=1)


def _to_row_linear(ref, val, spr, pitch):
    n_rows = val.shape[0]
    for c in range(spr):
        ref[pl.ds(c, n_rows, stride=pitch), :] = val[:, c * LANES:(c + 1) * LANES]
    for c in range(spr, pitch):
        ref[pl.ds(c, n_rows, stride=pitch), :] = jnp.zeros((n_rows, LANES), val.dtype)


def _expert_kernel(tile_e_ref, n_tiles_ref, rowtok_ref, x_hbm, w1g_ref, w1l_ref, b1g_ref, b1l_ref, w2_ref, b2_ref,
                   o_ref, xbuf, sem):
    i = pl.program_id(0)
    spr = w1g_ref.shape[1] // LANES
    pitch = spr + ROW_PAD
    tm = xbuf.shape[1] // pitch
    slot = i % 2
    n_tiles = n_tiles_ref[0]

    @pl.when(i == 0)
    def _():
        _gather_rows(rowtok_ref, 0, x_hbm, xbuf.at[0], sem.at[0], tm, spr, spr, pitch)

    for nxt in (0, 1):
        @pl.when((i + 1 < n_tiles) & (slot != nxt))
        def _(nxt=nxt):
            _gather_rows(rowtok_ref, (i + 1) * tm, x_hbm, xbuf.at[nxt], sem.at[nxt], tm, spr, spr, pitch)

    @pl.when(i < n_tiles)
    def _():
        done = xbuf.at[slot, pl.ds(0, tm * spr)]
        pltpu.make_async_copy(done, done, sem.at[slot]).wait()
        xs = _from_row_linear(xbuf.at[slot], 0, tm, spr, pitch).astype(BF16)
        glu = jnp.dot(xs, w1g_ref[0], preferred_element_type=F32) + b1g_ref[0]
        lin = jnp.dot(xs, w1l_ref[0], preferred_element_type=F32) + b1l_ref[0]
        glu = jnp.minimum(glu, SWIGLU_LIMIT)
        lin = jnp.clip(lin, -SWIGLU_LIMIT, SWIGLU_LIMIT)
        act = glu * jax.nn.sigmoid(SWIGLU_ALPHA * glu) * (lin + 1.0)
        y = jnp.dot(act.astype(BF16), w2_ref[0], preferred_element_type=F32) + b2_ref[0]
        _to_row_linear(o_ref, y, spr, pitch)

    @pl.when(i >= n_tiles)
    def _():
        o_ref[...] = jnp.zeros_like(o_ref)


def _w1_prep_kernel(w_ref, o_ref):
    two_ff = w_ref.shape[2]
    src = lax.broadcasted_iota(I32, (two_ff, two_ff), 0)
    dst = lax.broadcasted_iota(I32, (two_ff, two_ff), 1)
    want = jnp.where(dst < two_ff // 2, 2 * dst, 2 * (dst - two_ff // 2) + 1)
    perm = (src == want).astype(BF16)
    o_ref[0] = jnp.dot(w_ref[0].astype(BF16), perm, preferred_element_type=F32).astype(BF16)


def _w1_prep(w1, *, tr=2048):
    e, d, two_ff = w1.shape
    tr = min(tr, d)
    blk = pl.BlockSpec((1, tr, two_ff), lambda i, j: (i, j, 0))
    return pl.pallas_call(
        _w1_prep_kernel,
        grid=(e, d // tr),
        in_specs=[blk],
        out_specs=blk,
        out_shape=jax.ShapeDtypeStruct(w1.shape, BF16),
        compiler_params=_cparams("parallel", "parallel"),
    )(w1)


def _expert_ffn(x_lin, tile_e, n_tiles, rowtok, w1p, b1g, b1l, w2, b2, *, max_tiles):
    tm = MOE_TILE
    d, ff = w1p.shape[1], w1p.shape[2] // 2
    spr = d // LANES
    pitch = spr + ROW_PAD
    wmap = lambda i, te, nt, rt: (te[i], 0, 0)
    w1g = w1l = w1p
    return pl.pallas_call(
        _expert_kernel,
        grid_spec=pltpu.PrefetchScalarGridSpec(
            num_scalar_prefetch=3,
            grid=(max_tiles,),
            in_specs=[pl.BlockSpec(memory_space=pl.ANY),
                      pl.BlockSpec((1, d, ff), wmap), pl.BlockSpec((1, d, ff), lambda i, te, nt, rt: (te[i], 0, 1)),
                      pl.BlockSpec((1, 1, ff), wmap), pl.BlockSpec((1, 1, ff), wmap),
                      pl.BlockSpec((1, ff, d), wmap), pl.BlockSpec((1, 1, d), wmap)],
            out_specs=pl.BlockSpec((tm * pitch, LANES), lambda i, te, nt, rt: (i, 0)),
            scratch_shapes=[pltpu.VMEM((2, tm * pitch, LANES), F32), pltpu.SemaphoreType.DMA((2,))]),
        out_shape=jax.ShapeDtypeStruct((max_tiles * tm * pitch, LANES), F32),
        compiler_params=_cparams("arbitrary"),
    )(tile_e, n_tiles, rowtok, x_lin, w1g, w1l, b1g, b1l, w2, b2)


def _combine_kernel(pos_ref, ys_hbm, x_ref, gate_ref, g_ref, b_ref, of_ref, ob_ref, ybuf, sem):
    i = pl.program_id(0)
    nsteps = pl.num_programs(0)
    tt, d = x_ref.shape
    spr = d // LANES
    pitch = spr + ROW_PAD
    n = nsteps * tt
    slot = i % 2

    def fetch(step, s):
        for k in range(TOP_K):
            _gather_rows(pos_ref, k * n + step * tt, ys_hbm, ybuf.at[s, pl.ds(k * tt * pitch, tt * pitch)],
                         sem.at[s], tt, spr, pitch, pitch)

    @pl.when(i == 0)
    def _():
        fetch(0, 0)

    for nxt in (0, 1):
        @pl.when((i + 1 < nsteps) & (slot != nxt))
        def _(nxt=nxt):
            fetch(i + 1, nxt)

    done = ybuf.at[slot, pl.ds(0, TOP_K * tt * spr)]
    pltpu.make_async_copy(done, done, sem.at[slot]).wait()
    gate = gate_ref[...]
    moe = None
    for k in range(TOP_K):
        term = gate[:, k:k + 1] * _from_row_linear(ybuf.at[slot], k * tt, tt, spr, pitch)
        moe = term if moe is None else moe + term
    out = _ln_rows(DEEPNORM_ALPHA * x_ref[...] + moe, g_ref[...], b_ref[...])
    of_ref[...] = out
    ob_ref[...] = out.astype(BF16)


def _combine_ln(pos_flat, ys, x, gate_t, g, b):
    n, d = x.shape
    tt = min(COMBINE_TILE, n)
    pitch = d // LANES + ROW_PAD
    row = pl.BlockSpec((tt, d), lambda i, p: (i, 0))
    vec = pl.BlockSpec((1, d), lambda i, p: (0, 0))
    return pl.pallas_call(
        _combine_kernel,
        grid_spec=pltpu.PrefetchScalarGridSpec(
            num_scalar_prefetch=1,
            grid=(n // tt,),
            in_specs=[pl.BlockSpec(memory_space=pl.ANY), row,
                      pl.BlockSpec((tt, TOP_K), lambda i, p: (i, 0)), vec, vec],
            out_specs=[row, row],
            scratch_shapes=[pltpu.VMEM((2, TOP_K * tt * pitch, LANES), F32), pltpu.SemaphoreType.DMA((2,))]),
        out_shape=[jax.ShapeDtypeStruct((n, d), F32), jax.ShapeDtypeStruct((n, d), BF16)],
        compiler_params=_cparams("arbitrary"),
    )(pos_flat, ys, x, gate_t, g.reshape(1, d), b.reshape(1, d))


def _moe_block(x_f32, x_lin, router_w, router_b, w1, b1, w2, b2, ln_g, ln_b):
    n, d = x_f32.shape
    tm = MOE_TILE
    top_i, gate, rank, counts = _router(x_f32, router_w, router_b)
    counts = counts[:, 0].astype(I32)
    padded = (counts + tm - 1) // tm * tm
    ends = jnp.cumsum(padded)
    starts = ends - padded
    max_tiles = (n * TOP_K) // tm + N_EXPERTS
    e_ids = jnp.arange(N_EXPERTS, dtype=I32)
    start_of = jnp.sum(jnp.where(top_i[:, :, None] == e_ids, starts, 0), axis=-1)
    pos = start_of + rank
    tile_start = jnp.arange(max_tiles, dtype=I32) * tm
    tile_e = jnp.minimum(jnp.sum(tile_start[:, None] >= ends[None, :], axis=1), N_EXPERTS - 1).astype(I32)
    n_tiles = (ends[-1] // tm).astype(I32).reshape(1)
    tok = jnp.broadcast_to(jnp.arange(n, dtype=I32)[None, :], (TOP_K, n))
    rowtok = jnp.zeros((max_tiles * tm,), I32).at[pos.reshape(-1)].set(tok.reshape(-1))
    b1g = b1[:, None, 0::2]
    b1l = b1[:, None, 1::2]
    ys = _expert_ffn(x_lin, tile_e, n_tiles, rowtok, _w1_prep(w1), b1g, b1l, w2.astype(BF16), b2[:, None, :],
                     max_tiles=max_tiles)
    return _combine_ln(pos.reshape(-1), ys, x_f32, gate.T, ln_g, ln_b)


def _rms_norm(x, g):
    xf = x.astype(F32)
    return (xf * lax.rsqrt(jnp.mean(jnp.square(xf), axis=-1, keepdims=True) + RMS_EPS) * g).astype(x.dtype)


def _l2_normalize(x):
    xf = x.astype(F32)
    return xf * lax.rsqrt(jnp.sum(jnp.square(xf), axis=-1, keepdims=True) + RMS_EPS)


def _layer_norm(x, g, b):
    xf = x.astype(F32)
    mu = jnp.mean(xf, axis=-1, keepdims=True)
    var = jnp.mean(jnp.square(xf - mu), axis=-1, keepdims=True)
    return ((xf - mu) * lax.rsqrt(var + LN_EPS) * g + b).astype(x.dtype)


def _rope_tables(T, dim):
    inv_freq = ROPE_THETA ** (-jnp.arange(0, dim, 2, dtype=F32) / dim)
    ang = jnp.arange(T, dtype=F32)[:, None] * inv_freq[None, :]
    return jnp.cos(ang), jnp.sin(ang)


def _apply_rope(x, cos, sin):
    x1, x2 = jnp.split(x.astype(F32), 2, axis=-1)
    return jnp.concatenate([x1 * cos - x2 * sin, x2 * cos + x1 * sin], axis=-1).astype(x.dtype)


def _causal_depthwise_conv(x, w):
    taps, c = w.shape
    return lax.conv_general_dilated(x, w[:, None, :].astype(x.dtype), window_strides=(1,),
                                    padding=((taps - 1, 0),), dimension_numbers=('NWC', 'WIO', 'NWC'),
                                    feature_group_count=c)


def _gated_delta_rule_chunked(q, k, v, g, beta):
    B, T, H, DK = q.shape
    DV = v.shape[-1]
    C = GDN_CHUNK
    N = T // C

    def to_chunks(a):
        a = a.astype(F32).reshape((B, N, C, H) + a.shape[3:])
        return jnp.moveaxis(a, 3, 1)

    q, k, v, g, beta = (to_chunks(a) for a in (q, k, v, g, beta))
    gc = jnp.cumsum(g, axis=-1)
    causal = jnp.tril(jnp.ones((C, C), dtype=bool))
    strict = jnp.tril(jnp.ones((C, C), dtype=bool), -1)
    diff = gc[..., :, None] - gc[..., None, :]
    decay = jnp.where(causal, jnp.exp(jnp.where(causal, diff, 0.0)), 0.0)
    kb = k * beta[..., None]
    lower = jnp.where(strict, jnp.einsum('bhnid,bhnjd->bhnij', kb, k) * decay, 0.0)
    eye = jnp.broadcast_to(jnp.eye(C, dtype=F32), lower.shape)
    tinv = lax.linalg.triangular_solve(eye + lower, eye, left_side=True, lower=True, unit_diagonal=True)
    u = tinv @ (v * beta[..., None])
    w = tinv @ (kb * jnp.exp(gc)[..., None])
    a_intra = jnp.where(causal, jnp.einsum('bhnid,bhnjd->bhnij', q, k) * decay, 0.0)
    q_dec = q * jnp.exp(gc)[..., None]
    k_dec = k * jnp.exp(gc[..., -1:] - gc)[..., None]
    g_last = jnp.exp(gc[..., -1])

    def step(state, xs):
        qd, kd, u_i, w_i, a_i, gl = xs
        v_new = u_i - w_i @ state
        o = qd @ state + a_i @ v_new
        state = state * gl[..., None, None] + jnp.einsum('bhcd,bhce->bhde', kd, v_new)
        return state, o

    xs = tuple(jnp.moveaxis(a, 2, 0) for a in (q_dec, k_dec, u, w, a_intra, g_last))
    _, o = lax.scan(step, jnp.zeros((B, H, DK, DV), F32), xs)
    return jnp.transpose(o, (1, 0, 3, 2, 4)).reshape(B, T, H, DV)


def _gdn_mixer(h, conv_w, a_log, dt_bias, norm_g):
    B, T, _ = h.shape
    qkv, z, b, a = jnp.split(h, [GDN_CONV_DIM, GDN_CONV_DIM + GDN_VAL_DIM,
                                 GDN_CONV_DIM + GDN_VAL_DIM + GDN_V_HEADS], axis=-1)
    qkv = jax.nn.silu(_causal_depthwise_conv(qkv, conv_w))
    q, k, v = jnp.split(qkv, [GDN_KEY_DIM, 2 * GDN_KEY_DIM], axis=-1)
    rep = GDN_V_HEADS // GDN_K_HEADS
    q = jnp.repeat(q.reshape(B, T, GDN_K_HEADS, GDN_HEAD_DIM), rep, axis=2)
    k = jnp.repeat(k.reshape(B, T, GDN_K_HEADS, GDN_HEAD_DIM), rep, axis=2)
    v = v.reshape(B, T, GDN_V_HEADS, GDN_HEAD_DIM)
    q = _l2_normalize(q) * GDN_HEAD_DIM ** -0.5
    k = _l2_normalize(k)
    beta = jax.nn.sigmoid(b.astype(F32))
    g = -jnp.exp(a_log.astype(F32)) * jax.nn.softplus(a.astype(F32) + dt_bias.astype(F32))
    o = _gated_delta_rule_chunked(q, k, v, g, beta)
    o = _rms_norm(o, norm_g) * jax.nn.silu(z.reshape(B, T, GDN_V_HEADS, GDN_HEAD_DIM).astype(F32))
    return o.reshape(B, T, GDN_VAL_DIM).astype(h.dtype)


def _dilated_window_attention(q, k, v, window, dilation):
    B, T, H, E = q.shape
    d = dilation
    w = window // dilation
    ls = T // d
    nb = -(-ls // w)
    pad = nb * w - ls

    def to_sub(a):
        a = jnp.swapaxes(a.reshape(B, ls, d, H, E), 1, 2)
        return jnp.pad(a, ((0, 0), (0, 0), (0, pad), (0, 0), (0, 0)))

    def band(a):
        a = jnp.pad(a, ((0, 0), (0, 0), (w, 0), (0, 0), (0, 0))).reshape(B, d, nb + 1, w, H, E)
        return jnp.concatenate([a[:, :, :-1], a[:, :, 1:]], axis=3)

    qs = to_sub(q).reshape(B, d, nb, w, H, E)
    kb = band(to_sub(k))
    vb = band(to_sub(v))
    s = jnp.einsum('bdnqhe,bdnkhe->bdnhqk', qs, kb).astype(F32) * E ** -0.5
    qi = jnp.arange(w)[:, None]
    kj = jnp.arange(2 * w)[None, :]
    dist = w + qi - kj
    key_sub = jnp.arange(nb)[:, None, None] * w + kj[None] - w
    valid = (dist >= 0) & (dist <= w) & (key_sub >= 0)
    s = jnp.where(valid[:, None], s, -jnp.inf)
    m = jnp.max(s, axis=-1, keepdims=True)
    pexp = jnp.exp(s - m)
    l = jnp.sum(pexp, axis=-1, keepdims=True)
    o = jnp.einsum('bdnhqk,bdnkhe->bdnqhe', pexp, vb.astype(F32)) / jnp.swapaxes(l, 3, 4)
    lse = jnp.swapaxes((m + jnp.log(l))[..., 0], 3, 4)

    def from_sub(a):
        a = a.reshape((B, d, nb * w) + a.shape[4:])[:, :, :ls]
        return jnp.swapaxes(a, 1, 2).reshape((B, T) + a.shape[3:])

    return from_sub(o), from_sub(lse)


def _dilated_mixer(h):
    B, T, _ = h.shape
    qkv = h.reshape(B, T, len(DIL_GROUPS), 3, DIL_HEADS, DIL_HEAD_DIM)
    outs, lses = [], []
    for gi, (window, dilation) in enumerate(DIL_GROUPS):
        o, lse = _dilated_window_attention(qkv[:, :, gi, 0], qkv[:, :, gi, 1], qkv[:, :, gi, 2], window, dilation)
        outs.append(o)
        lses.append(lse)
    wts = jax.nn.softmax(jnp.stack(lses, axis=0), axis=0)
    o = jnp.sum(wts[..., None] * jnp.stack(outs, axis=0), axis=0)
    return o.reshape(B, T, DIL_OUT).astype(h.dtype)


def _dsa_mixer(h, q_norm_g, kv_norm_g, w_q_up, w_idx_q, idx_k_norm_g, idx_k_norm_b, w_kv_up):
    B, T, _ = h.shape
    o1 = DSA_Q_LORA
    o2 = o1 + DSA_KV_LORA
    o3 = o2 + DSA_ROPE_DIM
    o4 = o3 + IDX_DIM
    q_lat, kv_lat, k_rope, idx_k, idx_w = jnp.split(h, [o1, o2, o3, o4], axis=-1)
    cos, sin = _rope_tables(T, DSA_ROPE_DIM)
    cq = _rms_norm(q_lat, q_norm_g)
    q = (cq @ w_q_up).reshape(B, T, DSA_HEADS, DSA_NOPE_DIM + DSA_ROPE_DIM)
    q_nope = q[..., :DSA_NOPE_DIM]
    q_rope = _apply_rope(q[..., DSA_NOPE_DIM:], cos[:, None], sin[:, None])
    c_kv = _rms_norm(kv_lat, kv_norm_g)
    k_rope = _apply_rope(k_rope, cos, sin)
    w_kv = w_kv_up.reshape(DSA_KV_LORA, DSA_HEADS, DSA_NOPE_DIM + DSA_V_DIM)
    w_uk = w_kv[..., :DSA_NOPE_DIM]
    w_uv = w_kv[..., DSA_NOPE_DIM:]
    iq = (cq @ w_idx_q).reshape(B, T, IDX_HEADS, IDX_DIM)
    iq = jnp.concatenate([_apply_rope(iq[..., :IDX_ROPE_DIM], cos[:, None], sin[:, None]),
                          iq[..., IDX_ROPE_DIM:]], axis=-1).astype(F32)
    ik = _layer_norm(idx_k, idx_k_norm_g, idx_k_norm_b)
    ik = jnp.concatenate([_apply_rope(ik[..., :IDX_ROPE_DIM], cos, sin), ik[..., IDX_ROPE_DIM:]],
                         axis=-1).astype(F32)
    iw = idx_w.astype(F32) * (IDX_HEADS ** -0.5 * IDX_DIM ** -0.5)
    topk = min(DSA_TOPK_MAX, T // 4)
    scale = (DSA_NOPE_DIM + DSA_ROPE_DIM) ** -0.5
    key_pos = jnp.arange(T)
    gather = jax.vmap(lambda table, idx: table[idx])
    qb = 128

    def attend_block(args):
        qn, qr, iqb, iwb, qpos = args
        isc = jax.nn.relu(jnp.einsum('bqhd,bsd->bqhs', iqb, ik))
        isc = jnp.einsum('bqhs,bqh->bqs', isc, iwb)
        isc = jnp.where(key_pos[None, None, :] <= qpos[None, :, None], isc, -jnp.inf)
        _, sel = lax.top_k(isc, topk)
        c_sel = gather(c_kv, sel)
        r_sel = gather(k_rope, sel)
        q_abs = jnp.einsum('bqhd,chd->bqhc', qn, w_uk)
        s = (jnp.einsum('bqhc,bqkc->bqhk', q_abs, c_sel)
             + jnp.einsum('bqhr,bqkr->bqhk', qr, r_sel)).astype(F32) * scale
        s = jnp.where((sel <= qpos[None, :, None])[:, :, None, :], s, -jnp.inf)
        p = jax.nn.softmax(s, axis=-1).astype(c_sel.dtype)
        o_lat = jnp.einsum('bqhk,bqkc->bqhc', p, c_sel)
        return jnp.einsum('bqhc,chd->bqhd', o_lat, w_uv)

    nqb = T // qb

    def blocks(a):
        return jnp.moveaxis(a.reshape((B, nqb, qb) + a.shape[2:]), 1, 0)

    qpos = jnp.arange(T).reshape(nqb, qb)
    out = lax.map(attend_block, (blocks(q_nope), blocks(q_rope), blocks(iq), blocks(iw), qpos))
    return jnp.moveaxis(out, 0, 1).reshape(B, T, DSA_HEADS * DSA_V_DIM).astype(h.dtype)


DIL_BAND = 128
DIL_HEADS_PER_STEP = 4
DIL_MASKED = -1e30


def _dil_attn_kernel(q_ref, kp_ref, ko_ref, vp_ref, vo_ref, o_ref, lse_ref):
    nblk = pl.program_id(2)
    w = DIL_BAND
    e = DIL_HEAD_DIM
    scale = e ** -0.5
    qi = lax.broadcasted_iota(I32, (w, w), 0)
    kj = lax.broadcasted_iota(I32, (w, w), 1)
    own_ok = kj <= qi
    prev_ok = (kj >= qi) & (nblk > 0)
    contract_last = (((1,), (1,)), ((), ()))
    heads = range(DIL_HEADS_PER_STEP)
    cols = [slice(hh * e, (hh + 1) * e) for hh in heads]
    s_own = [lax.dot_general(q_ref[0, :, c], ko_ref[0, :, c], contract_last, preferred_element_type=F32) for c in cols]
    s_prev = [lax.dot_general(q_ref[0, :, c], kp_ref[0, :, c], contract_last, preferred_element_type=F32) for c in cols]
    s_own = [jnp.where(own_ok, s * scale, DIL_MASKED) for s in s_own]
    s_prev = [jnp.where(prev_ok, s * scale, DIL_MASKED) for s in s_prev]
    m = [jnp.maximum(jnp.max(a, axis=1, keepdims=True), jnp.max(b, axis=1, keepdims=True))
         for a, b in zip(s_own, s_prev)]
    p_own = [jnp.exp(s - mm) for s, mm in zip(s_own, m)]
    p_prev = [jnp.exp(s - mm) for s, mm in zip(s_prev, m)]
    l = [jnp.sum(a, axis=1, keepdims=True) + jnp.sum(b, axis=1, keepdims=True) for a, b in zip(p_own, p_prev)]
    o = [jnp.dot(p.astype(BF16), vo_ref[0, :, c], preferred_element_type=F32) for p, c in zip(p_own, cols)]
    o = [acc + jnp.dot(p.astype(BF16), vp_ref[0, :, c], preferred_element_type=F32)
         for acc, p, c in zip(o, p_prev, cols)]
    for hh in heads:
        o_ref[0, :, cols[hh]] = o[hh] / l[hh]
        lse_ref[0, :, cols[hh]] = jnp.broadcast_to(m[hh] + jnp.log(l[hh]), (w, e))


def _dil_group_attention(h, dilation, *, batch):
    gi = 0
    n, wid = h.shape
    seq = n // batch
    d = dilation
    ls = seq // d
    nb = ls // DIL_BAND
    hw = DIL_HEADS_PER_STEP * DIL_HEAD_DIM
    hsteps = DIL_OUT // hw
    wb = wid // hw
    assert wid % hw == 0 and ls % DIL_BAND == 0
    hv = h.reshape(batch, ls, d * wid)
    base = lambda j: (gi * 3 + j) * hsteps
    own = lambda j: pl.BlockSpec((1, DIL_BAND, hw), lambda b, r, s, c: (b, s, r * wb + base(j) + c))
    prev = lambda j: pl.BlockSpec((1, DIL_BAND, hw),
                                  lambda b, r, s, c: (b, jnp.maximum(s - 1, 0), r * wb + base(j) + c))
    out = pl.BlockSpec((1, DIL_BAND, hw), lambda b, r, s, c: (b, s, r * hsteps + c))
    o, lse = pl.pallas_call(
        _dil_attn_kernel,
        grid=(batch, d, nb, hsteps),
        in_specs=[own(0), prev(1), own(1), prev(2), own(2)],
        out_specs=[out, out],
        out_shape=[jax.ShapeDtypeStruct((batch, ls, d * DIL_OUT), F32)] * 2,
        compiler_params=_cparams("parallel", "parallel", "parallel", "parallel"),
    )(hv, hv, hv, hv, hv)
    return o.reshape(n, DIL_OUT), lse.reshape(n, DIL_OUT)


def _dil_combine_kernel(o0, o1, o2, l0, l1, l2, y_ref):
    a, b, c = l0[...], l1[...], l2[...]
    m = jnp.maximum(jnp.maximum(a, b), c)
    ea, eb, ec = jnp.exp(a - m), jnp.exp(b - m), jnp.exp(c - m)
    y_ref[...] = ((ea * o0[...] + eb * o1[...] + ec * o2[...]) / (ea + eb + ec)).astype(y_ref.dtype)


def _dilated_mixer_pallas(h_groups, *, batch):
    n = h_groups[0].shape[0]
    outs, lses = [], []
    for h, (window, dilation) in zip(h_groups, DIL_GROUPS):
        assert window // dilation == DIL_BAND
        o, lse = _dil_group_attention(h, dilation, batch=batch)
        outs.append(o)
        lses.append(lse)
    tt = min(512, n)
    blk = pl.BlockSpec((tt, 512), lambda i, c: (i, c))
    return pl.pallas_call(
        _dil_combine_kernel,
        grid=(n // tt, DIL_OUT // 512),
        in_specs=[blk] * 6,
        out_specs=blk,
        out_shape=jax.ShapeDtypeStruct((n, DIL_OUT), BF16),
        compiler_params=_cparams("parallel", "parallel"),
    )(*outs, *lses)


GDN_HEAD_GROUP = 4
GDN_CHUNK_GROUP = 4
GDN_PREP_COLS = 1024
SUBLANES = 8


def _gdn_prep_kernel(x_ref, halo_ref, w_ref, o_ref, *, tiles_per_seq):
    i = pl.program_id(0)
    c = pl.program_id(1)
    x = x_ref[...]
    tt = x.shape[0]
    halo = jnp.where(i % tiles_per_seq == 0, 0.0, halo_ref[...])
    row8 = lax.broadcasted_iota(I32, halo.shape, 0)
    w = w_ref[...]
    y = x * w[GDN_CONV_TAPS - 1:GDN_CONV_TAPS, :]
    for s in range(1, GDN_CONV_TAPS):
        xs = pltpu.roll(x, s, axis=0)
        first = jnp.where(row8 < s, pltpu.roll(halo, s, axis=0), xs[:SUBLANES])
        xs = jnp.concatenate([first, xs[SUBLANES:]], axis=0)
        y = y + xs * w[GDN_CONV_TAPS - 1 - s:GDN_CONV_TAPS - s, :]
    y = y * jax.nn.sigmoid(y)
    n_qk_tiles = 2 * GDN_KEY_DIM // GDN_PREP_COLS

    @pl.when(c >= n_qk_tiles)
    def _():
        o_ref[...] = y

    @pl.when(c < n_qk_tiles)
    def _():
        qscale = jnp.where(c < n_qk_tiles // 2, GDN_HEAD_DIM ** -0.5, 1.0)
        for g in range(GDN_PREP_COLS // GDN_HEAD_DIM):
            ys = y[:, g * GDN_HEAD_DIM:(g + 1) * GDN_HEAD_DIM]
            inv = lax.rsqrt(jnp.sum(ys * ys, axis=-1, keepdims=True) + RMS_EPS)
            o_ref[:, g * GDN_HEAD_DIM:(g + 1) * GDN_HEAD_DIM] = ys * inv * qscale


def _gdn_prep(h_main, conv_w, *, seq):
    n = h_main.shape[0]
    tt = min(256, seq)
    cw = GDN_PREP_COLS
    return pl.pallas_call(
        functools.partial(_gdn_prep_kernel, tiles_per_seq=seq // tt),
        grid=(n // tt, GDN_CONV_DIM // cw),
        in_specs=[pl.BlockSpec((tt, cw), lambda i, c: (i, c)),
                  pl.BlockSpec((SUBLANES, cw), lambda i, c: (jnp.maximum(i * (tt // SUBLANES) - 1, 0), c)),
                  pl.BlockSpec((GDN_CONV_TAPS, cw), lambda i, c: (0, c))],
        out_specs=pl.BlockSpec((tt, cw), lambda i, c: (i, c)),
        out_shape=jax.ShapeDtypeStruct((n, GDN_CONV_DIM), F32),
        compiler_params=_cparams("parallel", "parallel"),
    )(h_main, h_main, conv_w)


def _gdn_gate_kernel(ba_ref, alog_ref, dtb_ref, beta_ref, g_ref):
    ba = ba_ref[...]
    beta = jax.nn.sigmoid(ba)
    xa = ba + dtb_ref[...]
    softplus = jnp.maximum(xa, 0.0) + jnp.log(1.0 + jnp.exp(-jnp.abs(xa)))
    g = -jnp.exp(alog_ref[...]) * softplus
    hg = GDN_HEAD_GROUP
    for j in range(GDN_V_HEADS // hg):
        beta_ref[j] = beta[:, j * hg:(j + 1) * hg]
        g_ref[j] = g[:, GDN_V_HEADS + j * hg:GDN_V_HEADS + (j + 1) * hg]


def _gdn_gates(h_tail, a_log, dt_bias):
    n = h_tail.shape[0]
    tt = min(512, n)
    ng = GDN_V_HEADS // GDN_HEAD_GROUP
    lane_vec = lambda v: jnp.zeros((1, LANES), F32).at[0, GDN_V_HEADS:2 * GDN_V_HEADS].set(v)
    out = pl.BlockSpec((ng, tt, GDN_HEAD_GROUP), lambda i: (0, i, 0))
    return pl.pallas_call(
        _gdn_gate_kernel,
        grid=(n // tt,),
        in_specs=[pl.BlockSpec((tt, LANES), lambda i: (i, MEM_Q // LANES)),
                  pl.BlockSpec((1, LANES), lambda i: (0, 0)), pl.BlockSpec((1, LANES), lambda i: (0, 0))],
        out_specs=[out, out],
        out_shape=[jax.ShapeDtypeStruct((ng, n, GDN_HEAD_GROUP), F32)] * 2,
        compiler_params=_cparams("parallel"),
    )(h_tail, lane_vec(a_log), lane_vec(dt_bias))


def _dot_tril(ones_b, x):
    acc = None
    for _ in range(3):
        x_b = x.astype(BF16)
        d = jnp.dot(ones_b, x_b, preferred_element_type=F32)
        acc = d if acc is None else acc + d
        x = x - x_b.astype(F32)
    return acc


def _gdn_scan_kernel(q_ref, k_ref, v_ref, z_ref, beta_ref, g_ref, ng_ref, o_ref, state_ref):
    C = GDN_CHUNK
    hd = GDN_HEAD_DIM

    @pl.when(pl.program_id(2) == 0)
    def _():
        state_ref[...] = jnp.zeros_like(state_ref)

    r_i = lax.broadcasted_iota(I32, (C, C), 0)
    c_i = lax.broadcasted_iota(I32, (C, C), 1)
    causal = c_i <= r_i
    strict = c_i < r_i
    tril = causal.astype(F32)
    later = (r_i > c_i).astype(F32)
    eye = (r_i == c_i).astype(F32)
    contract_last = (((1,), (1,)), ((), ()))
    norm_g = ng_ref[...]

    tril_b = tril.astype(BF16)
    chunk_heads = [(c, j) for c in range(GDN_CHUNK_GROUP) for j in range(GDN_HEAD_GROUP)]
    rows = lambda c: slice(c * C, (c + 1) * C)
    hcols = lambda j: slice(j * hd, (j + 1) * hd)

    gc_all = [_dot_tril(tril_b, g_ref[0, rows(c), :]) for c in range(GDN_CHUNK_GROUP)]
    kk, qk = {}, {}
    for c in range(GDN_CHUNK_GROUP):
        for kh in range(GDN_HEAD_GROUP // 2):
            k_b = k_ref[rows(c), hcols(kh)].astype(BF16)
            q_b = q_ref[rows(c), hcols(kh)].astype(BF16)
            kk[c, kh] = lax.dot_general(k_b, k_b, contract_last, preferred_element_type=F32)
            qk[c, kh] = lax.dot_general(q_b, k_b, contract_last, preferred_element_type=F32)
    decay, mpow, tinv = {}, {}, {}
    for c, j in chunk_heads:
        diff = _dot_tril(tril_b, g_ref[0, rows(c), j:j + 1] * later)
        decay[c, j] = jnp.where(causal, jnp.exp(jnp.where(causal, diff, 0.0)), 0.0)
        lower = jnp.where(strict, beta_ref[0, rows(c), j:j + 1] * kk[c, j // 2] * decay[c, j], 0.0)
        mpow[c, j] = -lower
        tinv[c, j] = eye - lower
    for _ in range(5):
        for cj in chunk_heads:
            m_b = mpow[cj].astype(BF16)
            mpow[cj] = jnp.dot(m_b, m_b, preferred_element_type=F32)
        for cj in chunk_heads:
            tinv[cj] = tinv[cj] + jnp.dot(tinv[cj].astype(BF16), mpow[cj].astype(BF16), preferred_element_type=F32)
    u, w, a_intra, q_dec, k_dec_t, g_last = {}, {}, {}, {}, {}, {}
    for c, j in chunk_heads:
        k_h = k_ref[rows(c), hcols(j // 2)]
        beta = beta_ref[0, rows(c), j:j + 1]
        gc = gc_all[c][:, j:j + 1]
        egc = jnp.exp(gc)
        gc_last = gc[C - 1:C, :]
        tinv_b = tinv[c, j].astype(BF16)
        u[c, j] = jnp.dot(tinv_b, (v_ref[rows(c), hcols(j)] * beta).astype(BF16), preferred_element_type=F32)
        w[c, j] = jnp.dot(tinv_b, (k_h * (beta * egc)).astype(BF16), preferred_element_type=F32).astype(BF16)
        a_intra[c, j] = jnp.where(causal, qk[c, j // 2] * decay[c, j], 0.0).astype(BF16)
        q_dec[c, j] = (q_ref[rows(c), hcols(j // 2)] * egc).astype(BF16)
        k_dec_t[c, j] = (k_h * jnp.exp(gc_last - gc)).T.astype(BF16)
        g_last[c, j] = jnp.exp(gc_last)

    heads = range(GDN_HEAD_GROUP)
    state = [state_ref[j] for j in heads]
    for c in range(GDN_CHUNK_GROUP):
        state_b = [state[j].astype(BF16) for j in heads]
        v_new = [u[c, j] - jnp.dot(w[c, j], state_b[j], preferred_element_type=F32) for j in heads]
        v_new_b = [v.astype(BF16) for v in v_new]
        state = [state[j] * g_last[c, j] + jnp.dot(k_dec_t[c, j], v_new_b[j], preferred_element_type=F32)
                 for j in heads]
        for j in heads:
            o = jnp.dot(q_dec[c, j], state_b[j], preferred_element_type=F32) + jnp.dot(
                a_intra[c, j], v_new_b[j], preferred_element_type=F32)
            o = o * lax.rsqrt(jnp.mean(o * o, axis=-1, keepdims=True) + RMS_EPS) * norm_g
            z = z_ref[rows(c), hcols(j)]
            o_ref[rows(c), hcols(j)] = (o * (z * jax.nn.sigmoid(z))).astype(o_ref.dtype)
    for j in heads:
        state_ref[j] = state[j]


def _gdn_scan(qkv, h_main, beta, g, norm_g, *, batch):
    n = qkv.shape[0]
    seq = n // batch
    tr = GDN_CHUNK_GROUP * GDN_CHUNK
    nt = seq // tr
    hg = GDN_HEAD_GROUP
    kw = (hg // 2) * GDN_HEAD_DIM
    vw = hg * GDN_HEAD_DIM
    row = lambda b, h, t: b * nt + t
    return pl.pallas_call(
        _gdn_scan_kernel,
        grid=(batch, GDN_V_HEADS // hg, nt),
        in_specs=[pl.BlockSpec((tr, kw), lambda b, h, t: (row(b, h, t), h)),
                  pl.BlockSpec((tr, kw), lambda b, h, t: (row(b, h, t), GDN_KEY_DIM // kw + h)),
                  pl.BlockSpec((tr, vw), lambda b, h, t: (row(b, h, t), 2 * GDN_KEY_DIM // vw + h)),
                  pl.BlockSpec((tr, vw), lambda b, h, t: (row(b, h, t), GDN_CONV_DIM // vw + h)),
                  pl.BlockSpec((1, tr, hg), lambda b, h, t: (h, row(b, h, t), 0)),
                  pl.BlockSpec((1, tr, hg), lambda b, h, t: (h, row(b, h, t), 0)),
                  pl.BlockSpec((1, GDN_HEAD_DIM), lambda b, h, t: (0, 0))],
        out_specs=pl.BlockSpec((tr, vw), lambda b, h, t: (row(b, h, t), h)),
        out_shape=jax.ShapeDtypeStruct((n, GDN_VAL_DIM), BF16),
        scratch_shapes=[pltpu.VMEM((hg, GDN_HEAD_DIM, GDN_HEAD_DIM), F32)],
        compiler_params=_cparams("parallel", "parallel", "arbitrary"),
    )(qkv, qkv, qkv, h_main, beta, g, norm_g.reshape(1, GDN_HEAD_DIM))


def _gdn_in_weights(w_in):
    n_main = GDN_CONV_DIM + GDN_VAL_DIM
    n_ba = 2 * GDN_V_HEADS
    pad = jnp.zeros((w_in.shape[0], LANES - n_ba), w_in.dtype)
    tail = jnp.concatenate([w_in[:, n_main + n_ba:], w_in[:, n_main:n_main + n_ba], pad], axis=1)
    return w_in[:, :n_main], tail


def _gdn_mixer_pallas(h_main, h_tail, mix_params, *, batch):
    conv_w, a_log, dt_bias, norm_g = mix_params
    seq = h_main.shape[0] // batch
    qkv = _gdn_prep(h_main, conv_w, seq=seq)
    beta, g = _gdn_gates(h_tail, a_log, dt_bias)
    return _gdn_scan(qkv, h_main, beta, g, norm_g, batch=batch)


DSA_TQ = 128
DSA_TK = 512
DSA_HEAD_GROUP = 8
DSA_KEY_W = DSA_KV_LORA + LANES
MASK_BIAS = -2e30
M_INIT = -1e30
I32_MIN = -2 ** 31


def _rope_partner(x, half):
    ax = x.ndim - 1
    n = x.shape[ax]
    lane = lax.broadcasted_iota(I32, x.shape, ax) % (2 * half)
    return jnp.where(lane < half, -pltpu.roll(x, n - half, axis=ax), pltpu.roll(x, half, axis=ax))


def _dsa_prep_kernel(ql_ref, kv_ref, ik_ref, misc_ref, cos_ref, sin_ref, cosi_ref, sini_ref,
                     qg_ref, kvg_ref, ikg_ref, ikb_ref,
                     cq_ref, key_ref, iko_ref, iw_ref):
    half = DSA_ROPE_DIM // 2
    ql = ql_ref[...]
    cq_ref[...] = (ql * lax.rsqrt(jnp.mean(ql * ql, axis=-1, keepdims=True) + RMS_EPS) * qg_ref[...]).astype(BF16)
    kv = kv_ref[...]
    key_ref[:, :DSA_KV_LORA] = (
        kv * lax.rsqrt(jnp.mean(kv * kv, axis=-1, keepdims=True) + RMS_EPS) * kvg_ref[...]).astype(BF16)
    misc = misc_ref[...]
    lane = lax.broadcasted_iota(I32, misc.shape, 1)
    kr = jnp.where(lane < DSA_ROPE_DIM, misc * cos_ref[...] + _rope_partner(misc, half) * sin_ref[...], 0.0)
    key_ref[:, DSA_KV_LORA:] = kr.astype(BF16)
    iw = misc * (IDX_HEADS ** -0.5 * IDX_DIM ** -0.5)
    for h in range(IDX_HEADS):
        iw_ref[h] = iw[:, DSA_ROPE_DIM + h:DSA_ROPE_DIM + h + 1]
    ik = _ln_rows(ik_ref[...], ikg_ref[...], ikb_ref[...])
    iko_ref[...] = (ik * cosi_ref[...] + _rope_partner(ik, half) * sini_ref[...]).astype(BF16)


def _dsa_prep(h, tabs, q_norm_g, kv_norm_g, ik_g, ik_b, *, seq):
    n = h.shape[0]
    tq = min(256, seq)
    nt = seq // tq
    cos4, sin4, cosi, sini = tabs
    tab = pl.BlockSpec((tq, LANES), lambda i: (i % nt, 0))
    vec = lambda w: pl.BlockSpec((1, w), lambda i: (0, 0))
    blk = lambda w, j: pl.BlockSpec((tq, w), lambda i: (i, j))
    return pl.pallas_call(
        _dsa_prep_kernel,
        grid=(n // tq,),
        in_specs=[blk(DSA_Q_LORA, 0), blk(DSA_KV_LORA, 3), blk(LANES, 16), blk(LANES, 17), tab, tab, tab, tab,
                  vec(DSA_Q_LORA), vec(DSA_KV_LORA), vec(IDX_DIM), vec(IDX_DIM)],
        out_specs=[blk(DSA_Q_LORA, 0), blk(DSA_KEY_W, 0), blk(IDX_DIM, 0),
                   pl.BlockSpec((IDX_HEADS, tq, 1), lambda i: (0, i, 0))],
        out_shape=[jax.ShapeDtypeStruct((n, DSA_Q_LORA), BF16), jax.ShapeDtypeStruct((n, DSA_KEY_W), BF16),
                   jax.ShapeDtypeStruct((n, IDX_DIM), BF16), jax.ShapeDtypeStruct((IDX_HEADS, n, 1), F32)],
        compiler_params=_cparams("parallel"),
    )(h, h, h, h, cos4, sin4, cosi, sini, q_norm_g.reshape(1, -1), kv_norm_g.reshape(1, -1),
      ik_g.reshape(1, -1), ik_b.reshape(1, -1))


def _dsa_qpost_kernel(qn_ref, qri_ref, wuk_ref, cos_ref, sin_ref, cosi_ref, sini_ref, q_ref, iq_ref):
    half = DSA_ROPE_DIM // 2
    tq = qn_ref.shape[0]
    n_rope = DSA_HEADS * DSA_ROPE_DIM
    scale = (DSA_NOPE_DIM + DSA_ROPE_DIM) ** -0.5
    qr = qri_ref[:, :n_rope]
    cos = jnp.tile(cos_ref[...], (1, n_rope // LANES))
    sin = jnp.tile(sin_ref[...], (1, n_rope // LANES))
    qr = (qr * cos + _rope_partner(qr, half) * sin) * scale
    zeros = jnp.zeros((tq, LANES - DSA_ROPE_DIM), F32)
    cosi, sini = cosi_ref[...], sini_ref[...]
    for h in range(DSA_HEADS):
        q_ref[h, :, :DSA_KV_LORA] = (jnp.dot(
            qn_ref[:, h * DSA_NOPE_DIM:(h + 1) * DSA_NOPE_DIM], wuk_ref[h],
            preferred_element_type=F32) * scale).astype(BF16)
        q_ref[h, :, DSA_KV_LORA:] = jnp.concatenate(
            [qr[:, h * DSA_ROPE_DIM:(h + 1) * DSA_ROPE_DIM], zeros], axis=1).astype(BF16)
        iq = qri_ref[:, n_rope + h * IDX_DIM:n_rope + (h + 1) * IDX_DIM]
        iq_ref[h] = (iq * cosi + _rope_partner(iq, half) * sini).astype(BF16)


def _dsa_qpost(qn, qri, wuk_t, tabs, *, seq):
    n = qn.shape[0]
    tq = min(128, seq)
    nt = seq // tq
    cos4, sin4, cosi, sini = tabs
    tab = pl.BlockSpec((tq, LANES), lambda i: (i % nt, 0))
    row = lambda w: pl.BlockSpec((tq, w), lambda i: (i, 0))
    hm = lambda w: pl.BlockSpec((DSA_HEADS, tq, w), lambda i: (0, i, 0))
    return pl.pallas_call(
        _dsa_qpost_kernel,
        grid=(n // tq,),
        in_specs=[row(qn.shape[1]), row(qri.shape[1]),
                  pl.BlockSpec(wuk_t.shape, lambda i: (0, 0, 0)), tab, tab, tab, tab],
        out_specs=[hm(DSA_KEY_W), hm(IDX_DIM)],
        out_shape=[jax.ShapeDtypeStruct((DSA_HEADS, n, DSA_KEY_W), BF16),
                   jax.ShapeDtypeStruct((IDX_HEADS, n, IDX_DIM), BF16)],
        compiler_params=_cparams("parallel"),
    )(qn, qri, wuk_t, cos4, sin4, cosi, sini)


def _dsa_index_kernel(iq_ref, iw_ref, ik_ref, bias_ref, key_ref, *, topk):
    qi = pl.program_id(1)
    tq = iq_ref.shape[1]
    tk = DSA_TK
    hg = DSA_HEAD_GROUP
    n_kt = ((qi + 1) * tq + tk - 1) // tk
    qpos = qi * tq + lax.broadcasted_iota(I32, (tq, tk), 0)
    lane = lax.broadcasted_iota(I32, (tq, tk), 1)

    def score_tile(kt, c):
        k0 = pl.multiple_of(kt * tk, tk)
        keys = ik_ref[pl.ds(k0, tk), :]
        acc = jnp.zeros((tq, tk), F32)
        for g in range(IDX_HEADS // hg):
            lhs = iq_ref[g * hg:(g + 1) * hg].reshape(hg * tq, IDX_DIM)
            s = lax.dot_general(lhs, keys, (((1,), (1,)), ((), ())), preferred_element_type=F32)
            s = jnp.maximum(s, 0.0).reshape(hg, tq, tk) * iw_ref[g * hg:(g + 1) * hg]
            acc = acc + jnp.sum(s, axis=0)
        acc = jnp.where(k0 + lane <= qpos, acc, -jnp.inf)
        bits = pltpu.bitcast(acc, I32)
        key_ref[:, pl.ds(k0, tk)] = jnp.where(bits < 0, bits ^ 0x7FFFFFFF, bits)
        return c

    lax.fori_loop(0, n_kt, score_tile, 0)

    res = jnp.zeros((tq, 1), I32)
    for bit in range(31, -1, -1):
        bit_c = jnp.int32(I32_MIN if bit == 31 else (1 << bit))
        cand = (res | bit_c) ^ jnp.int32(I32_MIN)

        def count_tile(kt, cnt, cand=cand):
            blk = key_ref[:, pl.ds(pl.multiple_of(kt * tk, tk), tk)]
            ge = (blk >= cand).astype(I32)
            part = ge[:, 0:LANES]
            for j in range(1, tk // LANES):
                part = part + ge[:, j * LANES:(j + 1) * LANES]
            return cnt + part

        cnt = lax.fori_loop(0, n_kt, count_tile, jnp.zeros((tq, LANES), I32))
        total = jnp.sum(cnt, axis=1, keepdims=True)
        res = jnp.where(total >= topk, res | bit_c, res)
    thr = res ^ jnp.int32(I32_MIN)

    bias_ref[...] = jnp.full(bias_ref.shape, MASK_BIAS, bias_ref.dtype)

    def bias_tile(kt, c):
        k0 = pl.multiple_of(kt * tk, tk)
        sel = (key_ref[:, pl.ds(k0, tk)] >= thr) & (k0 + lane <= qpos)
        bias_ref[:, pl.ds(k0, tk)] = jnp.where(sel, 0.0, MASK_BIAS).astype(bias_ref.dtype)
        return c

    lax.fori_loop(0, n_kt, bias_tile, 0)


def _dsa_index(iq, iw, ik, *, batch, topk):
    n = iq.shape[1]
    seq = n // batch
    tq = min(DSA_TQ, seq)
    nq = seq // tq
    return pl.pallas_call(
        functools.partial(_dsa_index_kernel, topk=topk),
        grid=(batch, nq),
        in_specs=[pl.BlockSpec((IDX_HEADS, tq, IDX_DIM), lambda b, i: (0, b * nq + i, 0)),
                  pl.BlockSpec((IDX_HEADS, tq, 1), lambda b, i: (0, b * nq + i, 0)),
                  pl.BlockSpec((seq, IDX_DIM), lambda b, i: (b, 0))],
        out_specs=pl.BlockSpec((tq, seq), lambda b, i: (b * nq + i, 0)),
        out_shape=jax.ShapeDtypeStruct((n, seq), BF16),
        scratch_shapes=[pltpu.VMEM((tq, seq), I32)],
        compiler_params=_cparams("parallel", "parallel"),
    )(iq, iw, ik)


def _dsa_attn_kernel(q_ref, key_ref, bias_ref, wuv_ref, o_ref, m_ref, l_ref, acc_ref):
    qi = pl.program_id(1)
    kt = pl.program_id(2)
    tq = q_ref.shape[1]
    tk = key_ref.shape[0]
    hg = DSA_HEAD_GROUP
    rows = hg * tq
    last = ((qi + 1) * tq - 1) // tk

    @pl.when(kt == 0)
    def _():
        m_ref[...] = jnp.full(m_ref.shape, M_INIT, F32)
        l_ref[...] = jnp.zeros_like(l_ref)
        acc_ref[...] = jnp.zeros_like(acc_ref)

    @pl.when(kt <= last)
    def _():
        bias = bias_ref[...].astype(F32)
        keys = key_ref[...]
        vals = key_ref[:, :DSA_KV_LORA]
        for g in range(DSA_HEADS // hg):
            r = slice(g * rows, (g + 1) * rows)
            lhs = q_ref[g * hg:(g + 1) * hg].reshape(rows, DSA_KEY_W)
            s = lax.dot_general(lhs, keys, (((1,), (1,)), ((), ())), preferred_element_type=F32)
            s = (s.reshape(hg, tq, tk) + bias).reshape(rows, tk)
            m_prev = m_ref[r]
            m_cur = jnp.maximum(m_prev, jnp.max(s, axis=1, keepdims=True))
            alpha = jnp.exp(m_prev - m_cur)
            p = jnp.exp(s - m_cur[:, :1])
            l_ref[r] = alpha * l_ref[r] + jnp.sum(p, axis=1, keepdims=True)
            acc_ref[r] = acc_ref[r] * alpha[:, :1] + jnp.dot(p.astype(BF16), vals, preferred_element_type=F32)
            m_ref[r] = m_cur

    @pl.when(kt == pl.num_programs(2) - 1)
    def _():
        for h in range(DSA_HEADS):
            r = slice(h * tq, (h + 1) * tq)
            o_lat = (acc_ref[r] / l_ref[r][:, :1]).astype(BF16)
            o_ref[:, h * DSA_V_DIM:(h + 1) * DSA_V_DIM] = jnp.dot(
                o_lat, wuv_ref[h], preferred_element_type=F32).astype(o_ref.dtype)


def _dsa_attention(q, keys, bias, wuv, *, batch):
    n = keys.shape[0]
    seq = n // batch
    tq = min(DSA_TQ, seq)
    tk = min(DSA_TK, seq)
    nq, nk = seq // tq, seq // tk
    last = lambda i: ((i + 1) * tq - 1) // tk
    return pl.pallas_call(
        _dsa_attn_kernel,
        grid=(batch, nq, nk),
        in_specs=[pl.BlockSpec((DSA_HEADS, tq, DSA_KEY_W), lambda b, i, k: (0, b * nq + i, 0)),
                  pl.BlockSpec((tk, DSA_KEY_W), lambda b, i, k: (b * nk + jnp.minimum(k, last(i)), 0)),
                  pl.BlockSpec((tq, tk), lambda b, i, k: (b * nq + i, jnp.minimum(k, last(i)))),
                  pl.BlockSpec(wuv.shape, lambda b, i, k: (0, 0, 0))],
        out_specs=pl.BlockSpec((tq, DSA_HEADS * DSA_V_DIM), lambda b, i, k: (b * nq + i, 0)),
        out_shape=jax.ShapeDtypeStruct((n, DSA_HEADS * DSA_V_DIM), BF16),
        scratch_shapes=[pltpu.VMEM((DSA_HEADS * tq, LANES), F32), pltpu.VMEM((DSA_HEADS * tq, LANES), F32),
                        pltpu.VMEM((DSA_HEADS * tq, DSA_KV_LORA), F32)],
        compiler_params=_cparams("parallel", "parallel", "arbitrary"),
    )(q, keys, bias, wuv)


def _dsa_rope_tables(seq):
    inv_freq = ROPE_THETA ** (-jnp.arange(0, DSA_ROPE_DIM, 2, dtype=F32) / DSA_ROPE_DIM)
    ang = jnp.arange(seq, dtype=F32)[:, None] * inv_freq[None, :]
    c, s = jnp.cos(ang), jnp.sin(ang)
    one, zero = jnp.ones_like(c), jnp.zeros_like(c)
    return (jnp.concatenate([c, c, c, c], axis=1), jnp.concatenate([s, s, s, s], axis=1),
            jnp.concatenate([c, c, one, one], axis=1), jnp.concatenate([s, s, zero, zero], axis=1))


def _dsa_in_weight(w_in):
    o1 = DSA_Q_LORA
    o2 = o1 + DSA_KV_LORA
    o3 = o2 + DSA_ROPE_DIM
    o4 = o3 + IDX_DIM
    o5 = o4 + IDX_HEADS
    pad = jnp.zeros((w_in.shape[0], LANES - DSA_ROPE_DIM - IDX_HEADS), w_in.dtype)
    return jnp.concatenate([w_in[:, :o1], w_in[:, o5:], w_in[:, o1:o2], w_in[:, o3:o4], w_in[:, o2:o3],
                            w_in[:, o4:o5], pad], axis=1)


def _dsa_mixer_pallas(h, mix_params, *, batch):
    q_norm_g, kv_norm_g, w_q_up, w_idx_q, ik_g, ik_b, w_kv_up = mix_params
    n = h.shape[0]
    seq = n // batch
    tabs = _dsa_rope_tables(seq)
    cq, keys, ik, iw = _dsa_prep(h, tabs, q_norm_g, kv_norm_g, ik_g, ik_b, seq=seq)
    wq = w_q_up.reshape(DSA_Q_LORA, DSA_HEADS, DSA_NOPE_DIM + DSA_ROPE_DIM).astype(BF16)
    w_nope = wq[:, :, :DSA_NOPE_DIM].reshape(DSA_Q_LORA, -1)
    w_ri = jnp.concatenate([wq[:, :, DSA_NOPE_DIM:].reshape(DSA_Q_LORA, -1), w_idx_q.astype(BF16)], axis=1)
    qn = _matmul([cq], [w_nope], tm=1024, tn=512, out_dtype=BF16)
    qri = _matmul([cq], [w_ri], tm=1024, tn=512, out_dtype=F32)
    w_kv = w_kv_up.reshape(DSA_KV_LORA, DSA_HEADS, DSA_NOPE_DIM + DSA_V_DIM).astype(BF16)
    wuk_t = jnp.transpose(w_kv[:, :, :DSA_NOPE_DIM], (1, 2, 0))
    wuv = jnp.transpose(w_kv[:, :, DSA_NOPE_DIM:], (1, 0, 2))
    q, iq = _dsa_qpost(qn, qri, wuk_t, tabs, seq=seq)
    bias = _dsa_index(iq, iw, ik, batch=batch, topk=min(DSA_TOPK_MAX, seq // 4))
    return _dsa_attention(q, keys, bias, wuv, batch=batch)


def _trunk(x, mem, mem_ln_g, mem_ln_b, w_mem_kv, layers):
    B, T, D = x.shape
    n = B * T
    nm = mem.shape[1]
    mem_n = _ln_plain(mem.reshape(B * nm, D), mem_ln_g, mem_ln_b)
    mem_kv = _matmul([mem_n], [w_mem_kv.astype(BF16)], tm=512, tn=512, out_dtype=BF16).reshape(B, nm, 2 * MEM_Q)
    x_f = x.reshape(n, D)
    x_b = x_f.astype(BF16)
    for i, (w_in, mix_params, w_out, ln1_g, ln1_b, ffn_params, ln2_g, ln2_b) in enumerate(layers):
        kind = i % N_MIXERS
        n_in = w_in.shape[1]
        n_mix = n_in - MEM_Q
        if kind == 2:
            w_in_b = _dsa_in_weight(w_in).astype(BF16)
            h = _matmul([x_b], [w_in_b], tm=1024, tn=_pick_tn(w_in_b.shape[1]), out_dtype=F32)
            y_mix = _dsa_mixer_pallas(h, mix_params, batch=B)
            y_mem = _mem_attention(h, mem_kv, batch=B, col_block=2)
        elif kind == 0:
            w_main, w_tail = _gdn_in_weights(w_in)
            h_main = _matmul([x_b], [w_main.astype(BF16)], tm=1024, tn=512, out_dtype=F32)
            h_tail = _matmul([x_b], [w_tail.astype(BF16)], tm=1024, tn=w_tail.shape[1], out_dtype=F32)
            y_mix = _gdn_mixer_pallas(h_main, h_tail, mix_params, batch=B)
            y_mem = _mem_attention(h_tail, mem_kv, batch=B)
        else:
            w_in_b = w_in.astype(BF16)
            gw = 3 * DIL_OUT
            h_groups = [_matmul([x_b], [w_in_b[:, g * gw:(g + 1) * gw]], tm=1024, tn=512, out_dtype=BF16)
                        for g in range(len(DIL_GROUPS))]
            q_mem = _matmul([x_b], [w_in_b[:, n_mix:]], tm=1024, tn=MEM_Q, out_dtype=BF16)
            y_mix = _dilated_mixer_pallas(h_groups, batch=B)
            y_mem = _mem_attention(q_mem, mem_kv, batch=B)
        w_out_b = w_out.astype(BF16)
        n_out = y_mix.shape[1]
        y = _matmul([y_mix, y_mem], [w_out_b[:n_out], w_out_b[n_out:]], tm=1024, tn=512, out_dtype=F32)
        x_f, x_lin = _ln_residual(x_f, y, ln1_g, ln1_b)
        x_f, x_b = _moe_block(x_f, x_lin, *ffn_params, ln2_g, ln2_b)
    return x_f.reshape(B, T, D)


def _pick_tn(n):
    for tn in (512, 384, 256, 128):
        if n % tn == 0:
            return tn
    return n


def kernel(x, mem, mem_ln_g, mem_ln_b, w_mem_kv, w_in_0, conv_w_0, a_log_0, dt_bias_0, gdn_norm_g_0, w_out_0, ln1_g_0, ln1_b_0, router_w_0, router_b_0, moe_w1_0, moe_b1_0, moe_w2_0, moe_b2_0, ln2_g_0, ln2_b_0, w_in_1, w_out_1, ln1_g_1, ln1_b_1, router_w_1, router_b_1, moe_w1_1, moe_b1_1, moe_w2_1, moe_b2_1, ln2_g_1, ln2_b_1, w_in_2, q_norm_g_2, kv_norm_g_2, w_q_up_2, w_idx_q_2, idx_k_norm_g_2, idx_k_norm_b_2, w_kv_up_2, w_out_2, ln1_g_2, ln1_b_2, router_w_2, router_b_2, moe_w1_2, moe_b1_2, moe_w2_2, moe_b2_2, ln2_g_2, ln2_b_2, w_in_3, conv_w_3, a_log_3, dt_bias_3, gdn_norm_g_3, w_out_3, ln1_g_3, ln1_b_3, router_w_3, router_b_3, moe_w1_3, moe_b1_3, moe_w2_3, moe_b2_3, ln2_g_3, ln2_b_3):
    layers = (
        (w_in_0, (conv_w_0, a_log_0, dt_bias_0, gdn_norm_g_0), w_out_0, ln1_g_0, ln1_b_0,
         (router_w_0, router_b_0, moe_w1_0, moe_b1_0, moe_w2_0, moe_b2_0), ln2_g_0, ln2_b_0),
        (w_in_1, (), w_out_1, ln1_g_1, ln1_b_1,
         (router_w_1, router_b_1, moe_w1_1, moe_b1_1, moe_w2_1, moe_b2_1), ln2_g_1, ln2_b_1),
        (w_in_2, (q_norm_g_2, kv_norm_g_2, w_q_up_2, w_idx_q_2, idx_k_norm_g_2, idx_k_norm_b_2, w_kv_up_2), w_out_2,
         ln1_g_2, ln1_b_2, (router_w_2, router_b_2, moe_w1_2, moe_b1_2, moe_w2_2, moe_b2_2), ln2_g_2, ln2_b_2),
        (w_in_3, (conv_w_3, a_log_3, dt_bias_3, gdn_norm_g_3), w_out_3, ln1_g_3, ln1_b_3,
         (router_w_3, router_b_3, moe_w1_3, moe_b1_3, moe_w2_3, moe_b2_3), ln2_g_3, ln2_b_3),
    )
    return _trunk(x, mem, mem_ln_g, mem_ln_b, w_mem_kv, layers)
```

```python
import functools
import math

import jax
import jax.numpy as jnp
from jax import lax
from jax.experimental import pallas as pl
from jax.experimental.pallas import tpu as pltpu

F32 = jnp.float32
BF16 = jnp.bfloat16
I32 = jnp.int32
HIGHEST = lax.Precision.HIGHEST

V7X_VMEM_LIMIT_BYTES = 56 * 1024 * 1024

DEPTH = 4
N_MIXERS = 3
MEM_HEADS = 4
MEM_HEAD_DIM = 128
MEM_Q = MEM_HEADS * MEM_HEAD_DIM

GDN_K_HEADS = 16
GDN_V_HEADS = 32
GDN_HEAD_DIM = 128
GDN_KEY_DIM = GDN_K_HEADS * GDN_HEAD_DIM
GDN_VAL_DIM = GDN_V_HEADS * GDN_HEAD_DIM
GDN_CONV_DIM = 2 * GDN_KEY_DIM + GDN_VAL_DIM
GDN_CONV_TAPS = 4
GDN_CHUNK = 64

DIL_GROUPS = ((128, 1), (512, 4), (2048, 16))
DIL_HEADS = 16
DIL_HEAD_DIM = 128
DIL_OUT = DIL_HEADS * DIL_HEAD_DIM

DSA_HEADS = 32
DSA_NOPE_DIM = 128
DSA_ROPE_DIM = 64
DSA_V_DIM = 128
DSA_Q_LORA = 1024
DSA_KV_LORA = 512
IDX_HEADS = 32
IDX_DIM = 128
IDX_ROPE_DIM = 64
DSA_TOPK_MAX = 256
ROPE_THETA = 10000.0

N_EXPERTS = 32
TOP_K = 4
MOE_FF = 384
SWIGLU_ALPHA = 1.702
SWIGLU_LIMIT = 7.0

DEEPNORM_ALPHA = (2 * DEPTH) ** 0.25
LN_EPS = 1e-5
RMS_EPS = 1e-6


def _cparams(*sem):
    return pltpu.CompilerParams(dimension_semantics=sem, vmem_limit_bytes=V7X_VMEM_LIMIT_BYTES)


def _mm_kernel(*refs, n_pairs):
    o_ref = refs[-1]
    acc = None
    for p in range(n_pairs):
        d = jnp.dot(refs[p][...], refs[n_pairs + p][...], preferred_element_type=F32)
        acc = d if acc is None else acc + d
    o_ref[...] = acc.astype(o_ref.dtype)


def _matmul(a_list, w_list, *, tm, tn, out_dtype):
    m = a_list[0].shape[0]
    n = w_list[0].shape[1]
    tm = min(tm, m)
    tn = min(tn, n)
    assert m % tm == 0 and n % tn == 0, (m, n, tm, tn)
    in_specs = [pl.BlockSpec((tm, a.shape[1]), lambda i, j: (i, 0)) for a in a_list]
    in_specs += [pl.BlockSpec((w.shape[0], tn), lambda i, j: (0, j)) for w in w_list]
    return pl.pallas_call(
        functools.partial(_mm_kernel, n_pairs=len(a_list)),
        grid=(m // tm, n // tn),
        in_specs=in_specs,
        out_specs=pl.BlockSpec((tm, tn), lambda i, j: (i, j)),
        out_shape=jax.ShapeDtypeStruct((m, n), out_dtype),
        compiler_params=_cparams("parallel", "arbitrary"),
    )(*a_list, *w_list)


def _ln_rows(v, g, b):
    mu = jnp.mean(v, axis=-1, keepdims=True)
    c = v - mu
    var = jnp.mean(c * c, axis=-1, keepdims=True)
    return c * lax.rsqrt(var + LN_EPS) * g + b


def _ln_res_kernel(x_ref, y_ref, g_ref, b_ref, of_ref, ol_ref):
    out = _ln_rows(DEEPNORM_ALPHA * x_ref[...] + y_ref[...], g_ref[...], b_ref[...])
    of_ref[...] = out
    spr = out.shape[1] // LANES
    _to_row_linear(ol_ref, out, spr, spr)


def _ln_residual(x, y, g, b, *, tr=128):
    n, d = x.shape
    tr = min(tr, n)
    spr = d // LANES
    row = pl.BlockSpec((tr, d), lambda i: (i, 0))
    vec = pl.BlockSpec((1, d), lambda i: (0, 0))
    return pl.pallas_call(
        _ln_res_kernel,
        grid=(n // tr,),
        in_specs=[row, row, vec, vec],
        out_specs=[row, pl.BlockSpec((tr * spr, LANES), lambda i: (i, 0))],
        out_shape=[jax.ShapeDtypeStruct((n, d), F32), jax.ShapeDtypeStruct((n * spr, LANES), F32)],
        compiler_params=_cparams("parallel"),
    )(x, y, g.reshape(1, d), b.reshape(1, d))


def _ln_plain_kernel(x_ref, g_ref, b_ref, ob_ref):
    ob_ref[...] = _ln_rows(x_ref[...], g_ref[...], b_ref[...]).astype(BF16)


def _ln_plain(x, g, b, *, tr=128):
    n, d = x.shape
    tr = min(tr, n)
    row = pl.BlockSpec((tr, d), lambda i: (i, 0))
    vec = pl.BlockSpec((1, d), lambda i: (0, 0))
    return pl.pallas_call(
        _ln_plain_kernel,
        grid=(n // tr,),
        in_specs=[row, vec, vec],
        out_specs=row,
        out_shape=jax.ShapeDtypeStruct((n, d), BF16),
        compiler_params=_cparams("parallel"),
    )(x, g.reshape(1, d), b.reshape(1, d))


def _mem_attn_kernel(q_ref, kv_ref, o_ref):
    scale = MEM_HEAD_DIM ** -0.5
    for h in range(MEM_HEADS):
        lo = h * MEM_HEAD_DIM
        q = q_ref[:, lo:lo + MEM_HEAD_DIM].astype(BF16)
        k = kv_ref[0, :, lo:lo + MEM_HEAD_DIM]
        v = kv_ref[0, :, MEM_Q + lo:MEM_Q + lo + MEM_HEAD_DIM]
        s = lax.dot_general(q, k, (((1,), (1,)), ((), ())), preferred_element_type=F32) * scale
        m = jnp.max(s, axis=-1, keepdims=True)
        p = jnp.exp(s - m)
        l = jnp.sum(p, axis=-1, keepdims=True)
        p = (p / l).astype(BF16)
        o_ref[:, lo:lo + MEM_HEAD_DIM] = jnp.dot(p, v, preferred_element_type=F32).astype(o_ref.dtype)


def _mem_attention(q, mem_kv, *, batch, col_block=0, tq=512):
    n = q.shape[0]
    t = n // batch
    tq = min(tq, t)
    nm = mem_kv.shape[1]
    return pl.pallas_call(
        _mem_attn_kernel,
        grid=(batch, t // tq),
        in_specs=[pl.BlockSpec((tq, MEM_Q), lambda b, i: (b * (t // tq) + i, col_block)),
                  pl.BlockSpec((1, nm, 2 * MEM_Q), lambda b, i: (b, 0, 0))],
        out_specs=pl.BlockSpec((tq, MEM_Q), lambda b, i: (b * (t // tq) + i, 0)),
        out_shape=jax.ShapeDtypeStruct((n, MEM_Q), BF16),
        compiler_params=_cparams("parallel", "parallel"),
    )(q, mem_kv)


MOE_TILE = 256
ROUTER_TILE = 512
COMBINE_TILE = 128


def _router_kernel(x_ref, wt_ref, b_ref, idx_ref, gate_ref, rank_ref, cnt_ref, carry_ref):
    tr = x_ref.shape[0]

    @pl.when(pl.program_id(0) == 0)
    def _():
        carry_ref[...] = jnp.zeros_like(carry_ref)

    logits = lax.dot_general(wt_ref[...], x_ref[...], (((1,), (1,)), ((), ())),
                             precision=HIGHEST, preferred_element_type=F32) + b_ref[...]
    e_iota = lax.broadcasted_iota(I32, (N_EXPERTS, tr), 0)
    work = logits
    vals, onehots = [], []
    for k in range(TOP_K):
        m = jnp.max(work, axis=0, keepdims=True)
        idx = jnp.min(jnp.where(work == m, e_iota, N_EXPERTS), axis=0, keepdims=True)
        oh = e_iota == idx
        vals.append(m)
        onehots.append(oh)
        idx_ref[k:k + 1, :] = idx
        work = jnp.where(oh, -jnp.inf, work)
    exps = [jnp.exp(v - vals[0]) for v in vals]
    denom = exps[0] + exps[1] + exps[2] + exps[3]
    for k in range(TOP_K):
        gate_ref[k:k + 1, :] = exps[k] / denom
    mask = (onehots[0] | onehots[1] | onehots[2] | onehots[3])
    r_i = lax.broadcasted_iota(I32, (tr, tr), 0)
    c_i = lax.broadcasted_iota(I32, (tr, tr), 1)
    before = (r_i < c_i).astype(BF16)
    excl = jnp.dot(mask.astype(BF16), before, preferred_element_type=F32)
    rank = carry_ref[...] + excl
    for k in range(TOP_K):
        rank_ref[k:k + 1, :] = jnp.sum(jnp.where(onehots[k], rank, 0.0), axis=0, keepdims=True).astype(I32)
    carry_ref[...] += jnp.sum(mask.astype(F32), axis=1, keepdims=True)
    cnt_ref[...] = carry_ref[...]


def _router(x, router_w, router_b):
    n, d = x.shape
    tr = min(ROUTER_TILE, n)
    slab = pl.BlockSpec((TOP_K, tr), lambda i: (0, i))
    return pl.pallas_call(
        _router_kernel,
        grid=(n // tr,),
        in_specs=[pl.BlockSpec((tr, d), lambda i: (i, 0)),
                  pl.BlockSpec((N_EXPERTS, d), lambda i: (0, 0)),
                  pl.BlockSpec((N_EXPERTS, 1), lambda i: (0, 0))],
        out_specs=[slab, slab, slab, pl.BlockSpec((N_EXPERTS, 1), lambda i: (0, 0))],
        out_shape=[jax.ShapeDtypeStruct((TOP_K, n), I32), jax.ShapeDtypeStruct((TOP_K, n), F32),
                   jax.ShapeDtypeStruct((TOP_K, n), I32), jax.ShapeDtypeStruct((N_EXPERTS, 1), F32)],
        scratch_shapes=[pltpu.VMEM((N_EXPERTS, 1), F32)],
        compiler_params=_cparams("arbitrary"),
    )(x, router_w.T, router_b.reshape(N_EXPERTS, 1))


LANES = 128
ROW_PAD = 8


def _gather_rows(idx_ref, base, src_hbm, dst_ref, sem, n_rows, spr, src_pitch, dst_pitch):
    unroll = 8
    assert n_rows % unroll == 0

    def body(blk, c):
        for u in range(unroll):
            r = blk * unroll + u
            src = pl.multiple_of(idx_ref[base + r] * src_pitch, SUBLANES)
            dst = pl.multiple_of(r * dst_pitch, SUBLANES)
            pltpu.make_async_copy(src_hbm.at[pl.ds(src, spr)], dst_ref.at[pl.ds(dst, spr)], sem).start()
        return c
    lax.fori_loop(0, n_rows // unroll, body, 0)


def _from_row_linear(ref, first, n_rows, spr, pitch):
    return jnp.concatenate([ref[pl.ds(first * pitch + c, n_rows, stride=pitch), :] for c in range(spr)], axis=1)


def _to_row_linear(ref, val, spr, pitch):
    n_rows = val.shape[0]
    for c in range(spr):
        ref[pl.ds(c, n_rows, stride=pitch), :] = val[:, c * LANES:(c + 1) * LANES]
    for c in range(spr, pitch):
        ref[pl.ds(c, n_rows, stride=pitch), :] = jnp.zeros((n_rows, LANES), val.dtype)


def _expert_kernel(tile_e_ref, n_tiles_ref, rowtok_ref, x_hbm, w1g_ref, w1l_ref, b1g_ref, b1l_ref, w2_ref, b2_ref,
                   o_ref, xbuf, sem):
    i = pl.program_id(0)
    spr = w1g_ref.shape[1] // LANES
    pitch = spr + ROW_PAD
    tm = xbuf.shape[1] // pitch
    slot = i % 2
    n_tiles = n_tiles_ref[0]

    @pl.when(i == 0)
    def _():
        _gather_rows(rowtok_ref, 0, x_hbm, xbuf.at[0], sem.at[0], tm, spr, spr, pitch)

    for nxt in (0, 1):
        @pl.when((i + 1 < n_tiles) & (slot != nxt))
        def _(nxt=nxt):
            _gather_rows(rowtok_ref, (i + 1) * tm, x_hbm, xbuf.at[nxt], sem.at[nxt], tm, spr, spr, pitch)

    @pl.when(i < n_tiles)
    def _():
        done = xbuf.at[slot, pl.ds(0, tm * spr)]
        pltpu.make_async_copy(done, done, sem.at[slot]).wait()
        xs = _from_row_linear(xbuf.at[slot], 0, tm, spr, pitch).astype(BF16)
        glu = jnp.dot(xs, w1g_ref[0], preferred_element_type=F32) + b1g_ref[0]
        lin = jnp.dot(xs, w1l_ref[0], preferred_element_type=F32) + b1l_ref[0]
        glu = jnp.minimum(glu, SWIGLU_LIMIT)
        lin = jnp.clip(lin, -SWIGLU_LIMIT, SWIGLU_LIMIT)
        act = glu * jax.nn.sigmoid(SWIGLU_ALPHA * glu) * (lin + 1.0)
        y = jnp.dot(act.astype(BF16), w2_ref[0], preferred_element_type=F32) + b2_ref[0]
        _to_row_linear(o_ref, y, spr, pitch)

    @pl.when(i >= n_tiles)
    def _():
        o_ref[...] = jnp.zeros_like(o_ref)


def _w1_prep_kernel(w_ref, o_ref):
    two_ff = w_ref.shape[2]
    src = lax.broadcasted_iota(I32, (two_ff, two_ff), 0)
    dst = lax.broadcasted_iota(I32, (two_ff, two_ff), 1)
    want = jnp.where(dst < two_ff // 2, 2 * dst, 2 * (dst - two_ff // 2) + 1)
    perm = (src == want).astype(BF16)
    o_ref[0] = jnp.dot(w_ref[0].astype(BF16), perm, preferred_element_type=F32).astype(BF16)


def _w1_prep(w1, *, tr=2048):
    e, d, two_ff = w1.shape
    tr = min(tr, d)
    blk = pl.BlockSpec((1, tr, two_ff), lambda i, j: (i, j, 0))
    return pl.pallas_call(
        _w1_prep_kernel,
        grid=(e, d // tr),
        in_specs=[blk],
        out_specs=blk,
        out_shape=jax.ShapeDtypeStruct(w1.shape, BF16),
        compiler_params=_cparams("parallel", "parallel"),
    )(w1)


def _expert_ffn(x_lin, tile_e, n_tiles, rowtok, w1p, b1g, b1l, w2, b2, *, max_tiles):
    tm = MOE_TILE
    d, ff = w1p.shape[1], w1p.shape[2] // 2
    spr = d // LANES
    pitch = spr + ROW_PAD
    wmap = lambda i, te, nt, rt: (te[i], 0, 0)
    w1g = w1l = w1p
    return pl.pallas_call(
        _expert_kernel,
        grid_spec=pltpu.PrefetchScalarGridSpec(
            num_scalar_prefetch=3,
            grid=(max_tiles,),
            in_specs=[pl.BlockSpec(memory_space=pl.ANY),
                      pl.BlockSpec((1, d, ff), wmap), pl.BlockSpec((1, d, ff), lambda i, te, nt, rt: (te[i], 0, 1)),
                      pl.BlockSpec((1, 1, ff), wmap), pl.BlockSpec((1, 1, ff), wmap),
                      pl.BlockSpec((1, ff, d), wmap), pl.BlockSpec((1, 1, d), wmap)],
            out_specs=pl.BlockSpec((tm * pitch, LANES), lambda i, te, nt, rt: (i, 0)),
            scratch_shapes=[pltpu.VMEM((2, tm * pitch, LANES), F32), pltpu.SemaphoreType.DMA((2,))]),
        out_shape=jax.ShapeDtypeStruct((max_tiles * tm * pitch, LANES), F32),
        compiler_params=_cparams("arbitrary"),
    )(tile_e, n_tiles, rowtok, x_lin, w1g, w1l, b1g, b1l, w2, b2)


def _combine_kernel(pos_ref, ys_hbm, x_ref, gate_ref, g_ref, b_ref, of_ref, ob_ref, ybuf, sem):
    i = pl.program_id(0)
    nsteps = pl.num_programs(0)
    tt, d = x_ref.shape
    spr = d // LANES
    pitch = spr + ROW_PAD
    n = nsteps * tt
    slot = i % 2

    def fetch(step, s):
        for k in range(TOP_K):
            _gather_rows(pos_ref, k * n + step * tt, ys_hbm, ybuf.at[s, pl.ds(k * tt * pitch, tt * pitch)],
                         sem.at[s], tt, spr, pitch, pitch)

    @pl.when(i == 0)
    def _():
        fetch(0, 0)

    for nxt in (0, 1):
        @pl.when((i + 1 < nsteps) & (slot != nxt))
        def _(nxt=nxt):
            fetch(i + 1, nxt)

    done = ybuf.at[slot, pl.ds(0, TOP_K * tt * spr)]
    pltpu.make_async_copy(done, done, sem.at[slot]).wait()
    gate = gate_ref[...]
    moe = None
    for k in range(TOP_K):
        term = gate[:, k:k + 1] * _from_row_linear(ybuf.at[slot], k * tt, tt, spr, pitch)
        moe = term if moe is None else moe + term
    out = _ln_rows(DEEPNORM_ALPHA * x_ref[...] + moe, g_ref[...], b_ref[...])
    of_ref[...] = out
    ob_ref[...] = out.astype(BF16)


def _combine_ln(pos_flat, ys, x, gate_t, g, b):
    n, d = x.shape
    tt = min(COMBINE_TILE, n)
    pitch = d // LANES + ROW_PAD
    row = pl.BlockSpec((tt, d), lambda i, p: (i, 0))
    vec = pl.BlockSpec((1, d), lambda i, p: (0, 0))
    return pl.pallas_call(
        _combine_kernel,
        grid_spec=pltpu.PrefetchScalarGridSpec(
            num_scalar_prefetch=1,
            grid=(n // tt,),
            in_specs=[pl.BlockSpec(memory_space=pl.ANY), row,
                      pl.BlockSpec((tt, TOP_K), lambda i, p: (i, 0)), vec, vec],
            out_specs=[row, row],
            scratch_shapes=[pltpu.VMEM((2, TOP_K * tt * pitch, LANES), F32), pltpu.SemaphoreType.DMA((2,))]),
        out_shape=[jax.ShapeDtypeStruct((n, d), F32), jax.ShapeDtypeStruct((n, d), BF16)],
        compiler_params=_cparams("arbitrary"),
    )(pos_flat, ys, x, gate_t, g.reshape(1, d), b.reshape(1, d))


def _moe_block(x_f32, x_lin, router_w, router_b, w1, b1, w2, b2, ln_g, ln_b):
    n, d = x_f32.shape
    tm = MOE_TILE
    top_i, gate, rank, counts = _router(x_f32, router_w, router_b)
    counts = counts[:, 0].astype(I32)
    padded = (counts + tm - 1) // tm * tm
    ends = jnp.cumsum(padded)
    starts = ends - padded
    max_tiles = (n * TOP_K) // tm + N_EXPERTS
    e_ids = jnp.arange(N_EXPERTS, dtype=I32)
    start_of = jnp.sum(jnp.where(top_i[:, :, None] == e_ids, starts, 0), axis=-1)
    pos = start_of + rank
    tile_start = jnp.arange(max_tiles, dtype=I32) * tm
    tile_e = jnp.minimum(jnp.sum(tile_start[:, None] >= ends[None, :], axis=1), N_EXPERTS - 1).astype(I32)
    n_tiles = (ends[-1] // tm).astype(I32).reshape(1)
    tok = jnp.broadcast_to(jnp.arange(n, dtype=I32)[None, :], (TOP_K, n))
    rowtok = jnp.zeros((max_tiles * tm,), I32).at[pos.reshape(-1)].set(tok.reshape(-1))
    b1g = b1[:, None, 0::2]
    b1l = b1[:, None, 1::2]
    ys = _expert_ffn(x_lin, tile_e, n_tiles, rowtok, _w1_prep(w1), b1g, b1l, w2.astype(BF16), b2[:, None, :],
                     max_tiles=max_tiles)
    return _combine_ln(pos.reshape(-1), ys, x_f32, gate.T, ln_g, ln_b)


DIL_BAND = 128
DIL_HEADS_PER_STEP = 4
DIL_MASKED = -1e30


def _dil_attn_kernel(q_ref, kp_ref, ko_ref, vp_ref, vo_ref, o_ref, lse_ref):
    nblk = pl.program_id(2)
    w = DIL_BAND
    e = DIL_HEAD_DIM
    scale = e ** -0.5
    qi = lax.broadcasted_iota(I32, (w, w), 0)
    kj = lax.broadcasted_iota(I32, (w, w), 1)
    own_ok = kj <= qi
    prev_ok = (kj >= qi) & (nblk > 0)
    contract_last = (((1,), (1,)), ((), ()))
    heads = range(DIL_HEADS_PER_STEP)
    cols = [slice(hh * e, (hh + 1) * e) for hh in heads]
    s_own = [lax.dot_general(q_ref[0, :, c], ko_ref[0, :, c], contract_last, preferred_element_type=F32) for c in cols]
    s_prev = [lax.dot_general(q_ref[0, :, c], kp_ref[0, :, c], contract_last, preferred_element_type=F32) for c in cols]
    s_own = [jnp.where(own_ok, s * scale, DIL_MASKED) for s in s_own]
    s_prev = [jnp.where(prev_ok, s * scale, DIL_MASKED) for s in s_prev]
    m = [jnp.maximum(jnp.max(a, axis=1, keepdims=True), jnp.max(b, axis=1, keepdims=True))
         for a, b in zip(s_own, s_prev)]
    p_own = [jnp.exp(s - mm) for s, mm in zip(s_own, m)]
    p_prev = [jnp.exp(s - mm) for s, mm in zip(s_prev, m)]
    l = [jnp.sum(a, axis=1, keepdims=True) + jnp.sum(b, axis=1, keepdims=True) for a, b in zip(p_own, p_prev)]
    o = [jnp.dot(p.astype(BF16), vo_ref[0, :, c], preferred_element_type=F32) for p, c in zip(p_own, cols)]
    o = [acc + jnp.dot(p.astype(BF16), vp_ref[0, :, c], preferred_element_type=F32)
         for acc, p, c in zip(o, p_prev, cols)]
    for hh in heads:
        o_ref[0, :, cols[hh]] = o[hh] / l[hh]
        lse_ref[0, :, cols[hh]] = jnp.broadcast_to(m[hh] + jnp.log(l[hh]), (w, e))


def _dil_group_attention(h, dilation, *, batch):
    gi = 0
    n, wid = h.shape
    seq = n // batch
    d = dilation
    ls = seq // d
    nb = ls // DIL_BAND
    hw = DIL_HEADS_PER_STEP * DIL_HEAD_DIM
    hsteps = DIL_OUT // hw
    wb = wid // hw
    assert wid % hw == 0 and ls % DIL_BAND == 0
    hv = h.reshape(batch, ls, d * wid)
    base = lambda j: (gi * 3 + j) * hsteps
    own = lambda j: pl.BlockSpec((1, DIL_BAND, hw), lambda b, r, s, c: (b, s, r * wb + base(j) + c))
    prev = lambda j: pl.BlockSpec((1, DIL_BAND, hw),
                                  lambda b, r, s, c: (b, jnp.maximum(s - 1, 0), r * wb + base(j) + c))
    out = pl.BlockSpec((1, DIL_BAND, hw), lambda b, r, s, c: (b, s, r * hsteps + c))
    o, lse = pl.pallas_call(
        _dil_attn_kernel,
        grid=(batch, d, nb, hsteps),
        in_specs=[own(0), prev(1), own(1), prev(2), own(2)],
        out_specs=[out, out],
        out_shape=[jax.ShapeDtypeStruct((batch, ls, d * DIL_OUT), F32)] * 2,
        compiler_params=_cparams("parallel", "parallel", "parallel", "parallel"),
    )(hv, hv, hv, hv, hv)
    return o.reshape(n, DIL_OUT), lse.reshape(n, DIL_OUT)


def _dil_combine_kernel(o0, o1, o2, l0, l1, l2, y_ref):
    a, b, c = l0[...], l1[...], l2[...]
    m = jnp.maximum(jnp.maximum(a, b), c)
    ea, eb, ec = jnp.exp(a - m), jnp.exp(b - m), jnp.exp(c - m)
    y_ref[...] = ((ea * o0[...] + eb * o1[...] + ec * o2[...]) / (ea + eb + ec)).astype(y_ref.dtype)


def _dilated_mixer_pallas(h_groups, *, batch):
    n = h_groups[0].shape[0]
    outs, lses = [], []
    for h, (window, dilation) in zip(h_groups, DIL_GROUPS):
        assert window // dilation == DIL_BAND
        o, lse = _dil_group_attention(h, dilation, batch=batch)
        outs.append(o)
        lses.append(lse)
    tt = min(512, n)
    blk = pl.BlockSpec((tt, 512), lambda i, c: (i, c))
    return pl.pallas_call(
        _dil_combine_kernel,
        grid=(n // tt, DIL_OUT // 512),
        in_specs=[blk] * 6,
        out_specs=blk,
        out_shape=jax.ShapeDtypeStruct((n, DIL_OUT), BF16),
        compiler_params=_cparams("parallel", "parallel"),
    )(*outs, *lses)


GDN_HEAD_GROUP = 4
GDN_CHUNK_GROUP = 8
GDN_PREP_COLS = 1024
SUBLANES = 8


def _gdn_prep_kernel(x_ref, halo_ref, w_ref, o_ref, *, tiles_per_seq):
    i = pl.program_id(0)
    c = pl.program_id(1)
    x = x_ref[...]
    tt = x.shape[0]
    halo = jnp.where(i % tiles_per_seq == 0, 0.0, halo_ref[...])
    row8 = lax.broadcasted_iota(I32, halo.shape, 0)
    w = w_ref[...]
    y = x * w[GDN_CONV_TAPS - 1:GDN_CONV_TAPS, :]
    for s in range(1, GDN_CONV_TAPS):
        xs = pltpu.roll(x, s, axis=0)
        first = jnp.where(row8 < s, pltpu.roll(halo, s, axis=0), xs[:SUBLANES])
        xs = jnp.concatenate([first, xs[SUBLANES:]], axis=0)
        y = y + xs * w[GDN_CONV_TAPS - 1 - s:GDN_CONV_TAPS - s, :]
    y = y * jax.nn.sigmoid(y)
    n_qk_tiles = 2 * GDN_KEY_DIM // GDN_PREP_COLS

    @pl.when(c >= n_qk_tiles)
    def _():
        o_ref[...] = y

    @pl.when(c < n_qk_tiles)
    def _():
        qscale = jnp.where(c < n_qk_tiles // 2, GDN_HEAD_DIM ** -0.5, 1.0)
        for g in range(GDN_PREP_COLS // GDN_HEAD_DIM):
            ys = y[:, g * GDN_HEAD_DIM:(g + 1) * GDN_HEAD_DIM]
            inv = lax.rsqrt(jnp.sum(ys * ys, axis=-1, keepdims=True) + RMS_EPS)
            o_ref[:, g * GDN_HEAD_DIM:(g + 1) * GDN_HEAD_DIM] = ys * inv * qscale


def _gdn_prep(h_main, conv_w, *, seq):
    n = h_main.shape[0]
    tt = min(256, seq)
    cw = GDN_PREP_COLS
    return pl.pallas_call(
        functools.partial(_gdn_prep_kernel, tiles_per_seq=seq // tt),
        grid=(n // tt, GDN_CONV_DIM // cw),
        in_specs=[pl.BlockSpec((tt, cw), lambda i, c: (i, c)),
                  pl.BlockSpec((SUBLANES, cw), lambda i, c: (jnp.maximum(i * (tt // SUBLANES) - 1, 0), c)),
                  pl.BlockSpec((GDN_CONV_TAPS, cw), lambda i, c: (0, c))],
        out_specs=pl.BlockSpec((tt, cw), lambda i, c: (i, c)),
        out_shape=jax.ShapeDtypeStruct((n, GDN_CONV_DIM), F32),
        compiler_params=_cparams("parallel", "parallel"),
    )(h_main, h_main, conv_w)


def _gdn_gate_kernel(ba_ref, alog_ref, dtb_ref, beta_ref, g_ref):
    ba = ba_ref[...]
    beta = jax.nn.sigmoid(ba)
    xa = ba + dtb_ref[...]
    softplus = jnp.maximum(xa, 0.0) + jnp.log(1.0 + jnp.exp(-jnp.abs(xa)))
    g = -jnp.exp(alog_ref[...]) * softplus
    hg = GDN_HEAD_GROUP
    for j in range(GDN_V_HEADS // hg):
        beta_ref[j] = beta[:, j * hg:(j + 1) * hg]
        g_ref[j] = g[:, GDN_V_HEADS + j * hg:GDN_V_HEADS + (j + 1) * hg]


def _gdn_gates(h_tail, a_log, dt_bias):
    n = h_tail.shape[0]
    tt = min(512, n)
    ng = GDN_V_HEADS // GDN_HEAD_GROUP
    lane_vec = lambda v: jnp.zeros((1, LANES), F32).at[0, GDN_V_HEADS:2 * GDN_V_HEADS].set(v)
    out = pl.BlockSpec((ng, tt, GDN_HEAD_GROUP), lambda i: (0, i, 0))
    return pl.pallas_call(
        _gdn_gate_kernel,
        grid=(n // tt,),
        in_specs=[pl.BlockSpec((tt, LANES), lambda i: (i, MEM_Q // LANES)),
                  pl.BlockSpec((1, LANES), lambda i: (0, 0)), pl.BlockSpec((1, LANES), lambda i: (0, 0))],
        out_specs=[out, out],
        out_shape=[jax.ShapeDtypeStruct((ng, n, GDN_HEAD_GROUP), F32)] * 2,
        compiler_params=_cparams("parallel"),
    )(h_tail, lane_vec(a_log), lane_vec(dt_bias))


def _dot_tril(ones_b, x):
    acc = None
    for _ in range(3):
        x_b = x.astype(BF16)
        d = jnp.dot(ones_b, x_b, preferred_element_type=F32)
        acc = d if acc is None else acc + d
        x = x - x_b.astype(F32)
    return acc


def _gdn_scan_kernel(q_ref, k_ref, v_ref, z_ref, beta_ref, g_ref, ng_ref, o_ref, state_ref):
    C = GDN_CHUNK
    hd = GDN_HEAD_DIM

    @pl.when(pl.program_id(2) == 0)
    def _():
        state_ref[...] = jnp.zeros_like(state_ref)

    r_i = lax.broadcasted_iota(I32, (C, C), 0)
    c_i = lax.broadcasted_iota(I32, (C, C), 1)
    causal = c_i <= r_i
    strict = c_i < r_i
    tril = causal.astype(F32)
    later = (r_i > c_i).astype(F32)
    eye = (r_i == c_i).astype(F32)
    contract_last = (((1,), (1,)), ((), ()))
    norm_g = ng_ref[...]

    tril_b = tril.astype(BF16)
    chunk_heads = [(c, j) for c in range(GDN_CHUNK_GROUP) for j in range(GDN_HEAD_GROUP)]
    rows = lambda c: slice(c * C, (c + 1) * C)
    hcols = lambda j: slice(j * hd, (j + 1) * hd)

    gc_all = [_dot_tril(tril_b, g_ref[0, rows(c), :]) for c in range(GDN_CHUNK_GROUP)]
    kk, qk = {}, {}
    for c in range(GDN_CHUNK_GROUP):
        for kh in range(GDN_HEAD_GROUP // 2):
            k_b = k_ref[rows(c), hcols(kh)].astype(BF16)
            q_b = q_ref[rows(c), hcols(kh)].astype(BF16)
            kk[c, kh] = lax.dot_general(k_b, k_b, contract_last, preferred_element_type=F32)
            qk[c, kh] = lax.dot_general(q_b, k_b, contract_last, preferred_element_type=F32)
    decay, mpow, tinv = {}, {}, {}
    for c, j in chunk_heads:
        diff = _dot_tril(tril_b, g_ref[0, rows(c), j:j + 1] * later)
        decay[c, j] = jnp.where(causal, jnp.exp(jnp.where(causal, diff, 0.0)), 0.0)
        lower = jnp.where(strict, beta_ref[0, rows(c), j:j + 1] * kk[c, j // 2] * decay[c, j], 0.0)
        mpow[c, j] = -lower
        tinv[c, j] = eye - lower
    for _ in range(5):
        for cj in chunk_heads:
            m_b = mpow[cj].astype(BF16)
            mpow[cj] = jnp.dot(m_b, m_b, preferred_element_type=F32)
        for cj in chunk_heads:
            tinv[cj] = tinv[cj] + jnp.dot(tinv[cj].astype(BF16), mpow[cj].astype(BF16), preferred_element_type=F32)
    u, w, a_intra, q_dec, k_dec_t, g_last = {}, {}, {}, {}, {}, {}
    for c, j in chunk_heads:
        k_h = k_ref[rows(c), hcols(j // 2)]
        beta = beta_ref[0, rows(c), j:j + 1]
        gc = gc_all[c][:, j:j + 1]
        egc = jnp.exp(gc)
        gc_last = gc[C - 1:C, :]
        tinv_b = tinv[c, j].astype(BF16)
        u[c, j] = jnp.dot(tinv_b, (v_ref[rows(c), hcols(j)] * beta).astype(BF16), preferred_element_type=F32)
        w[c, j] = jnp.dot(tinv_b, (k_h * (beta * egc)).astype(BF16), preferred_element_type=F32).astype(BF16)
        a_intra[c, j] = jnp.where(causal, qk[c, j // 2] * decay[c, j], 0.0).astype(BF16)
        q_dec[c, j] = (q_ref[rows(c), hcols(j // 2)] * egc).astype(BF16)
        k_dec_t[c, j] = (k_h * jnp.exp(gc_last - gc)).T.astype(BF16)
        g_last[c, j] = jnp.exp(gc_last)

    heads = range(GDN_HEAD_GROUP)
    state = [state_ref[j] for j in heads]
    for c in range(GDN_CHUNK_GROUP):
        state_b = [state[j].astype(BF16) for j in heads]
        v_new = [u[c, j] - jnp.dot(w[c, j], state_b[j], preferred_element_type=F32) for j in heads]
        v_new_b = [v.astype(BF16) for v in v_new]
        state = [state[j] * g_last[c, j] + jnp.dot(k_dec_t[c, j], v_new_b[j], preferred_element_type=F32)
                 for j in heads]
        for j in heads:
            o = jnp.dot(q_dec[c, j], state_b[j], preferred_element_type=F32) + jnp.dot(
                a_intra[c, j], v_new_b[j], preferred_element_type=F32)
            o = o * lax.rsqrt(jnp.mean(o * o, axis=-1, keepdims=True) + RMS_EPS) * norm_g
            z = z_ref[rows(c), hcols(j)]
            o_ref[rows(c), hcols(j)] = (o * (z * jax.nn.sigmoid(z))).astype(o_ref.dtype)
    for j in heads:
        state_ref[j] = state[j]


def _gdn_scan(qkv, h_main, beta, g, norm_g, *, batch):
    n = qkv.shape[0]
    seq = n // batch
    tr = GDN_CHUNK_GROUP * GDN_CHUNK
    nt = seq // tr
    hg = GDN_HEAD_GROUP
    kw = (hg // 2) * GDN_HEAD_DIM
    vw = hg * GDN_HEAD_DIM
    row = lambda b, h, t: b * nt + t
    return pl.pallas_call(
        _gdn_scan_kernel,
        grid=(batch, GDN_V_HEADS // hg, nt),
        in_specs=[pl.BlockSpec((tr, kw), lambda b, h, t: (row(b, h, t), h)),
                  pl.BlockSpec((tr, kw), lambda b, h, t: (row(b, h, t), GDN_KEY_DIM // kw + h)),
                  pl.BlockSpec((tr, vw), lambda b, h, t: (row(b, h, t), 2 * GDN_KEY_DIM // vw + h)),
                  pl.BlockSpec((tr, vw), lambda b, h, t: (row(b, h, t), GDN_CONV_DIM // vw + h)),
                  pl.BlockSpec((1, tr, hg), lambda b, h, t: (h, row(b, h, t), 0)),
                  pl.BlockSpec((1, tr, hg), lambda b, h, t: (h, row(b, h, t), 0)),
                  pl.BlockSpec((1, GDN_HEAD_DIM), lambda b, h, t: (0, 0))],
        out_specs=pl.BlockSpec((tr, vw), lambda b, h, t: (row(b, h, t), h)),
        out_shape=jax.ShapeDtypeStruct((n, GDN_VAL_DIM), BF16),
        scratch_shapes=[pltpu.VMEM((hg, GDN_HEAD_DIM, GDN_HEAD_DIM), F32)],
        compiler_params=_cparams("parallel", "parallel", "arbitrary"),
    )(qkv, qkv, qkv, h_main, beta, g, norm_g.reshape(1, GDN_HEAD_DIM))


def _gdn_in_weights(w_in):
    n_main = GDN_CONV_DIM + GDN_VAL_DIM
    n_ba = 2 * GDN_V_HEADS
    pad = jnp.zeros((w_in.shape[0], LANES - n_ba), w_in.dtype)
    tail = jnp.concatenate([w_in[:, n_main + n_ba:], w_in[:, n_main:n_main + n_ba], pad], axis=1)
    return w_in[:, :n_main], tail


def _gdn_mixer_pallas(h_main, h_tail, mix_params, *, batch):
    conv_w, a_log, dt_bias, norm_g = mix_params
    seq = h_main.shape[0] // batch
    qkv = _gdn_prep(h_main, conv_w, seq=seq)
    beta, g = _gdn_gates(h_tail, a_log, dt_bias)
    return _gdn_scan(qkv, h_main, beta, g, norm_g, batch=batch)


DSA_TQ = 128
DSA_TK = 512
DSA_HEAD_GROUP = 16
DSA_INDEX_TQ = 128
DSA_INDEX_HEAD_GROUP = 8
DSA_KEY_W = DSA_KV_LORA + LANES
MASK_BIAS = -2e30
M_INIT = -1e30
I32_MIN = -2 ** 31


def _rope_partner(x, half):
    ax = x.ndim - 1
    n = x.shape[ax]
    lane = lax.broadcasted_iota(I32, x.shape, ax) % (2 * half)
    return jnp.where(lane < half, -pltpu.roll(x, n - half, axis=ax), pltpu.roll(x, half, axis=ax))


def _dsa_prep_kernel(ql_ref, kv_ref, ik_ref, misc_ref, cos_ref, sin_ref, cosi_ref, sini_ref,
                     qg_ref, kvg_ref, ikg_ref, ikb_ref,
                     cq_ref, key_ref, iko_ref, iw_ref):
    half = DSA_ROPE_DIM // 2
    ql = ql_ref[...]
    cq_ref[...] = (ql * lax.rsqrt(jnp.mean(ql * ql, axis=-1, keepdims=True) + RMS_EPS) * qg_ref[...]).astype(BF16)
    kv = kv_ref[...]
    key_ref[:, :DSA_KV_LORA] = (
        kv * lax.rsqrt(jnp.mean(kv * kv, axis=-1, keepdims=True) + RMS_EPS) * kvg_ref[...]).astype(BF16)
    misc = misc_ref[...]
    lane = lax.broadcasted_iota(I32, misc.shape, 1)
    kr = jnp.where(lane < DSA_ROPE_DIM, misc * cos_ref[...] + _rope_partner(misc, half) * sin_ref[...], 0.0)
    key_ref[:, DSA_KV_LORA:] = kr.astype(BF16)
    iw = misc * (IDX_HEADS ** -0.5 * IDX_DIM ** -0.5)
    for h in range(IDX_HEADS):
        iw_ref[h] = iw[:, DSA_ROPE_DIM + h:DSA_ROPE_DIM + h + 1]
    ik = _ln_rows(ik_ref[...], ikg_ref[...], ikb_ref[...])
    iko_ref[...] = (ik * cosi_ref[...] + _rope_partner(ik, half) * sini_ref[...]).astype(BF16)


def _dsa_prep(h, tabs, q_norm_g, kv_norm_g, ik_g, ik_b, *, seq):
    n = h.shape[0]
    tq = min(256, seq)
    nt = seq // tq
    cos4, sin4, cosi, sini = tabs
    tab = pl.BlockSpec((tq, LANES), lambda i: (i % nt, 0))
    vec = lambda w: pl.BlockSpec((1, w), lambda i: (0, 0))
    blk = lambda w, j: pl.BlockSpec((tq, w), lambda i: (i, j))
    return pl.pallas_call(
        _dsa_prep_kernel,
        grid=(n // tq,),
        in_specs=[blk(DSA_Q_LORA, 0), blk(DSA_KV_LORA, 3), blk(LANES, 16), blk(LANES, 17), tab, tab, tab, tab,
                  vec(DSA_Q_LORA), vec(DSA_KV_LORA), vec(IDX_DIM), vec(IDX_DIM)],
        out_specs=[blk(DSA_Q_LORA, 0), blk(DSA_KEY_W, 0), blk(IDX_DIM, 0),
                   pl.BlockSpec((IDX_HEADS, tq, 1), lambda i: (0, i, 0))],
        out_shape=[jax.ShapeDtypeStruct((n, DSA_Q_LORA), BF16), jax.ShapeDtypeStruct((n, DSA_KEY_W), BF16),
                   jax.ShapeDtypeStruct((n, IDX_DIM), BF16), jax.ShapeDtypeStruct((IDX_HEADS, n, 1), F32)],
        compiler_params=_cparams("parallel"),
    )(h, h, h, h, cos4, sin4, cosi, sini, q_norm_g.reshape(1, -1), kv_norm_g.reshape(1, -1),
      ik_g.reshape(1, -1), ik_b.reshape(1, -1))


def _dsa_qpost_kernel(qn_ref, qri_ref, wuk_ref, cos_ref, sin_ref, cosi_ref, sini_ref, q_ref, iq_ref):
    half = DSA_ROPE_DIM // 2
    tq = qn_ref.shape[0]
    n_rope = DSA_HEADS * DSA_ROPE_DIM
    scale = (DSA_NOPE_DIM + DSA_ROPE_DIM) ** -0.5
    qr = qri_ref[:, :n_rope]
    cos = jnp.tile(cos_ref[...], (1, n_rope // LANES))
    sin = jnp.tile(sin_ref[...], (1, n_rope // LANES))
    qr = (qr * cos + _rope_partner(qr, half) * sin) * scale
    zeros = jnp.zeros((tq, LANES - DSA_ROPE_DIM), F32)
    cosi, sini = cosi_ref[...], sini_ref[...]
    for h in range(DSA_HEADS):
        q_ref[h, :, :DSA_KV_LORA] = (jnp.dot(
            qn_ref[:, h * DSA_NOPE_DIM:(h + 1) * DSA_NOPE_DIM], wuk_ref[h],
            preferred_element_type=F32) * scale).astype(BF16)
        q_ref[h, :, DSA_KV_LORA:] = jnp.concatenate(
            [qr[:, h * DSA_ROPE_DIM:(h + 1) * DSA_ROPE_DIM], zeros], axis=1).astype(BF16)
        iq = qri_ref[:, n_rope + h * IDX_DIM:n_rope + (h + 1) * IDX_DIM]
        iq_ref[h] = (iq * cosi + _rope_partner(iq, half) * sini).astype(BF16)


def _dsa_qpost(qn, qri, wuk_t, tabs, *, seq):
    n = qn.shape[0]
    tq = min(128, seq)
    nt = seq // tq
    cos4, sin4, cosi, sini = tabs
    tab = pl.BlockSpec((tq, LANES), lambda i: (i % nt, 0))
    row = lambda w: pl.BlockSpec((tq, w), lambda i: (i, 0))
    hm = lambda w: pl.BlockSpec((DSA_HEADS, tq, w), lambda i: (0, i, 0))
    return pl.pallas_call(
        _dsa_qpost_kernel,
        grid=(n // tq,),
        in_specs=[row(qn.shape[1]), row(qri.shape[1]),
                  pl.BlockSpec(wuk_t.shape, lambda i: (0, 0, 0)), tab, tab, tab, tab],
        out_specs=[hm(DSA_KEY_W), hm(IDX_DIM)],
        out_shape=[jax.ShapeDtypeStruct((DSA_HEADS, n, DSA_KEY_W), BF16),
                   jax.ShapeDtypeStruct((IDX_HEADS, n, IDX_DIM), BF16)],
        compiler_params=_cparams("parallel"),
    )(qn, qri, wuk_t, cos4, sin4, cosi, sini)


def _dsa_index_kernel(iq_ref, iw_ref, ik_ref, bias_ref, key_ref, *, topk):
    qi = pl.program_id(1)
    tq = iq_ref.shape[1]
    tk = DSA_TK
    hg = DSA_INDEX_HEAD_GROUP
    n_kt = ((qi + 1) * tq + tk - 1) // tk
    qpos = qi * tq + lax.broadcasted_iota(I32, (tq, tk), 0)
    lane = lax.broadcasted_iota(I32, (tq, tk), 1)

    def score_tile(kt, c):
        k0 = pl.multiple_of(kt * tk, tk)
        keys = ik_ref[pl.ds(k0, tk), :]
        acc = jnp.zeros((tq, tk), F32)
        for g in range(IDX_HEADS // hg):
            lhs = iq_ref[g * hg:(g + 1) * hg].reshape(hg * tq, IDX_DIM)
            s = lax.dot_general(lhs, keys, (((1,), (1,)), ((), ())), preferred_element_type=F32)
            s = jnp.maximum(s, 0.0).reshape(hg, tq, tk) * iw_ref[g * hg:(g + 1) * hg]
            acc = acc + jnp.sum(s, axis=0)
        acc = jnp.where(k0 + lane <= qpos, acc, -jnp.inf)
        bits = pltpu.bitcast(acc, I32)
        key_ref[:, pl.ds(k0, tk)] = jnp.where(bits < 0, bits ^ 0x7FFFFFFF, bits)
        return c

    lax.fori_loop(0, n_kt, score_tile, 0)

    res = jnp.zeros((tq, 1), I32)
    for bit in range(31, -1, -1):
        bit_c = jnp.int32(I32_MIN if bit == 31 else (1 << bit))
        cand = (res | bit_c) ^ jnp.int32(I32_MIN)

        def count_tile(kt, cnt, cand=cand):
            blk = key_ref[:, pl.ds(pl.multiple_of(kt * tk, tk), tk)]
            ge = (blk >= cand).astype(I32)
            part = ge[:, 0:LANES]
            for j in range(1, tk // LANES):
                part = part + ge[:, j * LANES:(j + 1) * LANES]
            return cnt + part

        cnt = lax.fori_loop(0, n_kt, count_tile, jnp.zeros((tq, LANES), I32))
        total = jnp.sum(cnt, axis=1, keepdims=True)
        res = jnp.where(total >= topk, res | bit_c, res)
    thr = res ^ jnp.int32(I32_MIN)

    bias_ref[...] = jnp.full(bias_ref.shape, MASK_BIAS, bias_ref.dtype)

    def bias_tile(kt, c):
        k0 = pl.multiple_of(kt * tk, tk)
        sel = (key_ref[:, pl.ds(k0, tk)] >= thr) & (k0 + lane <= qpos)
        bias_ref[:, pl.ds(k0, tk)] = jnp.where(sel, 0.0, MASK_BIAS).astype(bias_ref.dtype)
        return c

    lax.fori_loop(0, n_kt, bias_tile, 0)


def _dsa_index(iq, iw, ik, *, batch, topk):
    n = iq.shape[1]
    seq = n // batch
    tq = min(DSA_INDEX_TQ, seq)
    nq = seq // tq
    return pl.pallas_call(
        functools.partial(_dsa_index_kernel, topk=topk),
        grid=(batch, nq),
        in_specs=[pl.BlockSpec((IDX_HEADS, tq, IDX_DIM), lambda b, i: (0, b * nq + i, 0)),
                  pl.BlockSpec((IDX_HEADS, tq, 1), lambda b, i: (0, b * nq + i, 0)),
                  pl.BlockSpec((seq, IDX_DIM), lambda b, i: (b, 0))],
        out_specs=pl.BlockSpec((tq, seq), lambda b, i: (b * nq + i, 0)),
        out_shape=jax.ShapeDtypeStruct((n, seq), BF16),
        scratch_shapes=[pltpu.VMEM((tq, seq), I32)],
        compiler_params=_cparams("parallel", "parallel"),
    )(iq, iw, ik)


def _dsa_attn_kernel(q_ref, key_ref, bias_ref, wuv_ref, o_ref, m_ref, l_ref, acc_ref):
    qi = pl.program_id(1)
    kt = pl.program_id(2)
    tq = q_ref.shape[1]
    tk = key_ref.shape[0]
    hg = DSA_HEAD_GROUP
    rows = hg * tq
    last = ((qi + 1) * tq - 1) // tk

    @pl.when(kt == 0)
    def _():
        m_ref[...] = jnp.full(m_ref.shape, M_INIT, F32)
        l_ref[...] = jnp.zeros_like(l_ref)
        acc_ref[...] = jnp.zeros_like(acc_ref)

    @pl.when(kt <= last)
    def _():
        bias = bias_ref[...].astype(F32)
        keys = key_ref[...]
        vals = key_ref[:, :DSA_KV_LORA]
        for g in range(DSA_HEADS // hg):
            r = slice(g * rows, (g + 1) * rows)
            lhs = q_ref[g * hg:(g + 1) * hg].reshape(rows, DSA_KEY_W)
            s = lax.dot_general(lhs, keys, (((1,), (1,)), ((), ())), preferred_element_type=F32)
            s = (s.reshape(hg, tq, tk) + bias).reshape(rows, tk)
            m_prev = m_ref[r]
            m_cur = jnp.maximum(m_prev, jnp.max(s, axis=1, keepdims=True))
            alpha = jnp.exp(m_prev - m_cur)
            p = jnp.exp(s - m_cur[:, :1])
            l_ref[r] = alpha * l_ref[r] + jnp.sum(p, axis=1, keepdims=True)
            acc_ref[r] = acc_ref[r] * alpha[:, :1] + jnp.dot(p.astype(BF16), vals, preferred_element_type=F32)
            m_ref[r] = m_cur

    @pl.when(kt == pl.num_programs(2) - 1)
    def _():
        for h in range(DSA_HEADS):
            r = slice(h * tq, (h + 1) * tq)
            o_lat = (acc_ref[r] / l_ref[r][:, :1]).astype(BF16)
            o_ref[:, h * DSA_V_DIM:(h + 1) * DSA_V_DIM] = jnp.dot(
                o_lat, wuv_ref[h], preferred_element_type=F32).astype(o_ref.dtype)


def _dsa_attention(q, keys, bias, wuv, *, batch):
    n = keys.shape[0]
    seq = n // batch
    tq = min(DSA_TQ, seq)
    tk = min(DSA_TK, seq)
    nq, nk = seq // tq, seq // tk
    last = lambda i: ((i + 1) * tq - 1) // tk
    return pl.pallas_call(
        _dsa_attn_kernel,
        grid=(batch, nq, nk),
        in_specs=[pl.BlockSpec((DSA_HEADS, tq, DSA_KEY_W), lambda b, i, k: (0, b * nq + i, 0)),
                  pl.BlockSpec((tk, DSA_KEY_W), lambda b, i, k: (b * nk + jnp.minimum(k, last(i)), 0)),
                  pl.BlockSpec((tq, tk), lambda b, i, k: (b * nq + i, jnp.minimum(k, last(i)))),
                  pl.BlockSpec(wuv.shape, lambda b, i, k: (0, 0, 0))],
        out_specs=pl.BlockSpec((tq, DSA_HEADS * DSA_V_DIM), lambda b, i, k: (b * nq + i, 0)),
        out_shape=jax.ShapeDtypeStruct((n, DSA_HEADS * DSA_V_DIM), BF16),
        scratch_shapes=[pltpu.VMEM((DSA_HEADS * tq, LANES), F32), pltpu.VMEM((DSA_HEADS * tq, LANES), F32),
                        pltpu.VMEM((DSA_HEADS * tq, DSA_KV_LORA), F32)],
        compiler_params=_cparams("parallel", "parallel", "arbitrary"),
    )(q, keys, bias, wuv)


def _dsa_rope_tables(seq):
    inv_freq = ROPE_THETA ** (-jnp.arange(0, DSA_ROPE_DIM, 2, dtype=F32) / DSA_ROPE_DIM)
    ang = jnp.arange(seq, dtype=F32)[:, None] * inv_freq[None, :]
    c, s = jnp.cos(ang), jnp.sin(ang)
    one, zero = jnp.ones_like(c), jnp.zeros_like(c)
    return (jnp.concatenate([c, c, c, c], axis=1), jnp.concatenate([s, s, s, s], axis=1),
            jnp.concatenate([c, c, one, one], axis=1), jnp.concatenate([s, s, zero, zero], axis=1))


def _dsa_in_weight(w_in):
    o1 = DSA_Q_LORA
    o2 = o1 + DSA_KV_LORA
    o3 = o2 + DSA_ROPE_DIM
    o4 = o3 + IDX_DIM
    o5 = o4 + IDX_HEADS
    pad = jnp.zeros((w_in.shape[0], LANES - DSA_ROPE_DIM - IDX_HEADS), w_in.dtype)
    return jnp.concatenate([w_in[:, :o1], w_in[:, o5:], w_in[:, o1:o2], w_in[:, o3:o4], w_in[:, o2:o3],
                            w_in[:, o4:o5], pad], axis=1)


def _dsa_mixer_pallas(h, mix_params, *, batch):
    q_norm_g, kv_norm_g, w_q_up, w_idx_q, ik_g, ik_b, w_kv_up = mix_params
    n = h.shape[0]
    seq = n // batch
    tabs = _dsa_rope_tables(seq)
    cq, keys, ik, iw = _dsa_prep(h, tabs, q_norm_g, kv_norm_g, ik_g, ik_b, seq=seq)
    wq = w_q_up.reshape(DSA_Q_LORA, DSA_HEADS, DSA_NOPE_DIM + DSA_ROPE_DIM).astype(BF16)
    w_nope = wq[:, :, :DSA_NOPE_DIM].reshape(DSA_Q_LORA, -1)
    w_ri = jnp.concatenate([wq[:, :, DSA_NOPE_DIM:].reshape(DSA_Q_LORA, -1), w_idx_q.astype(BF16)], axis=1)
    qn = _matmul([cq], [w_nope], tm=1024, tn=512, out_dtype=BF16)
    qri = _matmul([cq], [w_ri], tm=1024, tn=512, out_dtype=F32)
    w_kv = w_kv_up.reshape(DSA_KV_LORA, DSA_HEADS, DSA_NOPE_DIM + DSA_V_DIM).astype(BF16)
    wuk_t = jnp.transpose(w_kv[:, :, :DSA_NOPE_DIM], (1, 2, 0))
    wuv = jnp.transpose(w_kv[:, :, DSA_NOPE_DIM:], (1, 0, 2))
    q, iq = _dsa_qpost(qn, qri, wuk_t, tabs, seq=seq)
    bias = _dsa_index(iq, iw, ik, batch=batch, topk=min(DSA_TOPK_MAX, seq // 4))
    return _dsa_attention(q, keys, bias, wuv, batch=batch)


def _trunk(x, mem, mem_ln_g, mem_ln_b, w_mem_kv, layers):
    B, T, D = x.shape
    n = B * T
    nm = mem.shape[1]
    mem_n = _ln_plain(mem.reshape(B * nm, D), mem_ln_g, mem_ln_b)
    mem_kv = _matmul([mem_n], [w_mem_kv.astype(BF16)], tm=512, tn=512, out_dtype=BF16).reshape(B, nm, 2 * MEM_Q)
    x_f = x.reshape(n, D)
    x_b = x_f.astype(BF16)
    for i, (w_in, mix_params, w_out, ln1_g, ln1_b, ffn_params, ln2_g, ln2_b) in enumerate(layers):
        kind = i % N_MIXERS
        n_in = w_in.shape[1]
        n_mix = n_in - MEM_Q
        if kind == 2:
            w_in_b = _dsa_in_weight(w_in).astype(BF16)
            h = _matmul([x_b], [w_in_b], tm=1024, tn=_pick_tn(w_in_b.shape[1]), out_dtype=F32)
            y_mix = _dsa_mixer_pallas(h, mix_params, batch=B)
            y_mem = _mem_attention(h, mem_kv, batch=B, col_block=2)
        elif kind == 0:
            w_main, w_tail = _gdn_in_weights(w_in)
            h_main = _matmul([x_b], [w_main.astype(BF16)], tm=1024, tn=512, out_dtype=F32)
            h_tail = _matmul([x_b], [w_tail.astype(BF16)], tm=1024, tn=w_tail.shape[1], out_dtype=F32)
            y_mix = _gdn_mixer_pallas(h_main, h_tail, mix_params, batch=B)
            y_mem = _mem_attention(h_tail, mem_kv, batch=B)
        else:
            w_in_b = w_in.astype(BF16)
            gw = 3 * DIL_OUT
            h_groups = [_matmul([x_b], [w_in_b[:, g * gw:(g + 1) * gw]], tm=1024, tn=512, out_dtype=BF16)
                        for g in range(len(DIL_GROUPS))]
            q_mem = _matmul([x_b], [w_in_b[:, n_mix:]], tm=1024, tn=MEM_Q, out_dtype=BF16)
            y_mix = _dilated_mixer_pallas(h_groups, batch=B)
            y_mem = _mem_attention(q_mem, mem_kv, batch=B)
        w_out_b = w_out.astype(BF16)
        n_out = y_mix.shape[1]
        y = _matmul([y_mix, y_mem], [w_out_b[:n_out], w_out_b[n_out:]], tm=1024, tn=512, out_dtype=F32)
        x_f, x_lin = _ln_residual(x_f, y, ln1_g, ln1_b)
        x_f, x_b = _moe_block(x_f, x_lin, *ffn_params, ln2_g, ln2_b)
    return x_f.reshape(B, T, D)


def _pick_tn(n):
    for tn in (512, 384, 256, 128):
        if n % tn == 0:
            return tn
    return n


def kernel(x, mem, mem_ln_g, mem_ln_b, w_mem_kv, w_in_0, conv_w_0, a_log_0, dt_bias_0, gdn_norm_g_0, w_out_0, ln1_g_0, ln1_b_0, router_w_0, router_b_0, moe_w1_0, moe_b1_0, moe_w2_0, moe_b2_0, ln2_g_0, ln2_b_0, w_in_1, w_out_1, ln1_g_1, ln1_b_1, router_w_1, router_b_1, moe_w1_1, moe_b1_1, moe_w2_1, moe_b2_1, ln2_g_1, ln2_b_1, w_in_2, q_norm_g_2, kv_norm_g_2, w_q_up_2, w_idx_q_2, idx_k_norm_g_2, idx_k_norm_b_2, w_kv_up_2, w_out_2, ln1_g_2, ln1_b_2, router_w_2, router_b_2, moe_w1_2, moe_b1_2, moe_w2_2, moe_b2_2, ln2_g_2, ln2_b_2, w_in_3, conv_w_3, a_log_3, dt_bias_3, gdn_norm_g_3, w_out_3, ln1_g_3, ln1_b_3, router_w_3, router_b_3, moe_w1_3, moe_b1_3, moe_w2_3, moe_b2_3, ln2_g_3, ln2_b_3):
    layers = (
        (w_in_0, (conv_w_0, a_log_0, dt_bias_0, gdn_norm_g_0), w_out_0, ln1_g_0, ln1_b_0,
         (router_w_0, router_b_0, moe_w1_0, moe_b1_0, moe_w2_0, moe_b2_0), ln2_g_0, ln2_b_0),
        (w_in_1, (), w_out_1, ln1_g_1, ln1_b_1,
         (router_w_1, router_b_1, moe_w1_1, moe_b1_1, moe_w2_1, moe_b2_1), ln2_g_1, ln2_b_1),
        (w_in_2, (q_norm_g_2, kv_norm_g_2, w_q_up_2, w_idx_q_2, idx_k_norm_g_2, idx_k_norm_b_2, w_kv_up_2), w_out_2,
         ln1_g_2, ln1_b_2, (router_w_2, router_b_2, moe_w1_2, moe_b1_2, moe_w2_2, moe_b2_2), ln2_g_2, ln2_b_2),
        (w_in_3, (conv_w_3, a_log_3, dt_bias_3, gdn_norm_g_3), w_out_3, ln1_g_3, ln1_b_3,
         (router_w_3, router_b_3, moe_w1_3, moe_b1_3, moe_w2_3, moe_b2_3), ln2_g_3, ln2_b_3),
    )
    return _trunk(x, mem, mem_ln_g, mem_ln_b, w_mem_kv, layers)
```

```python
import functools
import math

import jax
import jax.numpy as jnp
from jax import lax
from jax.experimental import pallas as pl
from jax.experimental.pallas import tpu as pltpu

F32 = jnp.float32
BF16 = jnp.bfloat16
I32 = jnp.int32
HIGHEST = lax.Precision.HIGHEST

V7X_VMEM_LIMIT_BYTES = 56 * 1024 * 1024

DEPTH = 4
N_MIXERS = 3
MEM_HEADS = 4
MEM_HEAD_DIM = 128
MEM_Q = MEM_HEADS * MEM_HEAD_DIM

GDN_K_HEADS = 16
GDN_V_HEADS = 32
GDN_HEAD_DIM = 128
GDN_KEY_DIM = GDN_K_HEADS * GDN_HEAD_DIM
GDN_VAL_DIM = GDN_V_HEADS * GDN_HEAD_DIM
GDN_CONV_DIM = 2 * GDN_KEY_DIM + GDN_VAL_DIM
GDN_CONV_TAPS = 4
GDN_CHUNK = 64

DIL_GROUPS = ((128, 1), (512, 4), (2048, 16))
DIL_HEADS = 16
DIL_HEAD_DIM = 128
DIL_OUT = DIL_HEADS * DIL_HEAD_DIM

DSA_HEADS = 32
DSA_NOPE_DIM = 128
DSA_ROPE_DIM = 64
DSA_V_DIM = 128
DSA_Q_LORA = 1024
DSA_KV_LORA = 512
IDX_HEADS = 32
IDX_DIM = 128
IDX_ROPE_DIM = 64
DSA_TOPK_MAX = 256
ROPE_THETA = 10000.0

N_EXPERTS = 32
TOP_K = 4
MOE_FF = 384
SWIGLU_ALPHA = 1.702
SWIGLU_LIMIT = 7.0

DEEPNORM_ALPHA = (2 * DEPTH) ** 0.25
LN_EPS = 1e-5
RMS_EPS = 1e-6


def _cparams(*sem):
    return pltpu.CompilerParams(dimension_semantics=sem, vmem_limit_bytes=V7X_VMEM_LIMIT_BYTES)


def _mm_kernel(*refs, n_pairs):
    o_ref = refs[-1]
    acc = None
    for p in range(n_pairs):
        d = jnp.dot(refs[p][...], refs[n_pairs + p][...], preferred_element_type=F32)
        acc = d if acc is None else acc + d
    o_ref[...] = acc.astype(o_ref.dtype)


def _matmul(a_list, w_list, *, tm, tn, out_dtype):
    m = a_list[0].shape[0]
    n = w_list[0].shape[1]
    tm = min(tm, m)
    tn = min(tn, n)
    assert m % tm == 0 and n % tn == 0, (m, n, tm, tn)
    in_specs = [pl.BlockSpec((tm, a.shape[1]), lambda i, j: (i, 0)) for a in a_list]
    in_specs += [pl.BlockSpec((w.shape[0], tn), lambda i, j: (0, j)) for w in w_list]
    return pl.pallas_call(
        functools.partial(_mm_kernel, n_pairs=len(a_list)),
        grid=(m // tm, n // tn),
        in_specs=in_specs,
        out_specs=pl.BlockSpec((tm, tn), lambda i, j: (i, j)),
        out_shape=jax.ShapeDtypeStruct((m, n), out_dtype),
        compiler_params=_cparams("parallel", "arbitrary"),
    )(*a_list, *w_list)


def _ln_rows(v, g, b):
    mu = jnp.mean(v, axis=-1, keepdims=True)
    c = v - mu
    var = jnp.mean(c * c, axis=-1, keepdims=True)
    return c * lax.rsqrt(var + LN_EPS) * g + b


def _ln_res_kernel(x_ref, y_ref, g_ref, b_ref, of_ref, ol_ref):
    out = _ln_rows(DEEPNORM_ALPHA * x_ref[...] + y_ref[...], g_ref[...], b_ref[...])
    of_ref[...] = out
    spr = out.shape[1] // LANES
    _to_row_linear(ol_ref, out, spr, spr)


def _ln_residual(x, y, g, b, *, tr=128):
    n, d = x.shape
    tr = min(tr, n)
    spr = d // LANES
    row = pl.BlockSpec((tr, d), lambda i: (i, 0))
    vec = pl.BlockSpec((1, d), lambda i: (0, 0))
    return pl.pallas_call(
        _ln_res_kernel,
        grid=(n // tr,),
        in_specs=[row, row, vec, vec],
        out_specs=[row, pl.BlockSpec((tr * spr, LANES), lambda i: (i, 0))],
        out_shape=[jax.ShapeDtypeStruct((n, d), F32), jax.ShapeDtypeStruct((n * spr, LANES), F32)],
        compiler_params=_cparams("parallel"),
    )(x, y, g.reshape(1, d), b.reshape(1, d))


def _ln_plain_kernel(x_ref, g_ref, b_ref, ob_ref):
    ob_ref[...] = _ln_rows(x_ref[...], g_ref[...], b_ref[...]).astype(BF16)


def _ln_plain(x, g, b, *, tr=128):
    n, d = x.shape
    tr = min(tr, n)
    row = pl.BlockSpec((tr, d), lambda i: (i, 0))
    vec = pl.BlockSpec((1, d), lambda i: (0, 0))
    return pl.pallas_call(
        _ln_plain_kernel,
        grid=(n // tr,),
        in_specs=[row, vec, vec],
        out_specs=row,
        out_shape=jax.ShapeDtypeStruct((n, d), BF16),
        compiler_params=_cparams("parallel"),
    )(x, g.reshape(1, d), b.reshape(1, d))


def _mem_attn_kernel(q_ref, kv_ref, o_ref):
    scale = MEM_HEAD_DIM ** -0.5
    for h in range(MEM_HEADS):
        lo = h * MEM_HEAD_DIM
        q = q_ref[:, lo:lo + MEM_HEAD_DIM].astype(BF16)
        k = kv_ref[0, :, lo:lo + MEM_HEAD_DIM]
        v = kv_ref[0, :, MEM_Q + lo:MEM_Q + lo + MEM_HEAD_DIM]
        s = lax.dot_general(q, k, (((1,), (1,)), ((), ())), preferred_element_type=F32) * scale
        m = jnp.max(s, axis=-1, keepdims=True)
        p = jnp.exp(s - m)
        l = jnp.sum(p, axis=-1, keepdims=True)
        p = (p / l).astype(BF16)
        o_ref[:, lo:lo + MEM_HEAD_DIM] = jnp.dot(p, v, preferred_element_type=F32).astype(o_ref.dtype)


def _mem_attention(q, mem_kv, *, batch, col_block=0, tq=512):
    n = q.shape[0]
    t = n // batch
    tq = min(tq, t)
    nm = mem_kv.shape[1]
    return pl.pallas_call(
        _mem_attn_kernel,
        grid=(batch, t // tq),
        in_specs=[pl.BlockSpec((tq, MEM_Q), lambda b, i: (b * (t // tq) + i, col_block)),
                  pl.BlockSpec((1, nm, 2 * MEM_Q), lambda b, i: (b, 0, 0))],
        out_specs=pl.BlockSpec((tq, MEM_Q), lambda b, i: (b * (t // tq) + i, 0)),
        out_shape=jax.ShapeDtypeStruct((n, MEM_Q), BF16),
        compiler_params=_cparams("parallel", "parallel"),
    )(q, mem_kv)


MOE_TILE = 256
ROUTER_TILE = 512
COMBINE_TILE = 128


def _router_kernel(x_ref, wt_ref, b_ref, idx_ref, gate_ref, rank_ref, cnt_ref, carry_ref):
    tr = x_ref.shape[0]

    @pl.when(pl.program_id(0) == 0)
    def _():
        carry_ref[...] = jnp.zeros_like(carry_ref)

    logits = lax.dot_general(wt_ref[...], x_ref[...], (((1,), (1,)), ((), ())),
                             precision=HIGHEST, preferred_element_type=F32) + b_ref[...]
    e_iota = lax.broadcasted_iota(I32, (N_EXPERTS, tr), 0)
    work = logits
    vals, onehots = [], []
    for k in range(TOP_K):
        m = jnp.max(work, axis=0, keepdims=True)
        idx = jnp.min(jnp.where(work == m, e_iota, N_EXPERTS), axis=0, keepdims=True)
        oh = e_iota == idx
        vals.append(m)
        onehots.append(oh)
        idx_ref[k:k + 1, :] = idx
        work = jnp.where(oh, -jnp.inf, work)
    exps = [jnp.exp(v - vals[0]) for v in vals]
    denom = exps[0] + exps[1] + exps[2] + exps[3]
    for k in range(TOP_K):
        gate_ref[k:k + 1, :] = exps[k] / denom
    mask = (onehots[0] | onehots[1] | onehots[2] | onehots[3])
    r_i = lax.broadcasted_iota(I32, (tr, tr), 0)
    c_i = lax.broadcasted_iota(I32, (tr, tr), 1)
    before = (r_i < c_i).astype(BF16)
    excl = jnp.dot(mask.astype(BF16), before, preferred_element_type=F32)
    rank = carry_ref[...] + excl
    for k in range(TOP_K):
        rank_ref[k:k + 1, :] = jnp.sum(jnp.where(onehots[k], rank, 0.0), axis=0, keepdims=True).astype(I32)
    carry_ref[...] += jnp.sum(mask.astype(F32), axis=1, keepdims=True)
    cnt_ref[...] = carry_ref[...]


def _router(x, router_w, router_b):
    n, d = x.shape
    tr = min(ROUTER_TILE, n)
    slab = pl.BlockSpec((TOP_K, tr), lambda i: (0, i))
    return pl.pallas_call(
        _router_kernel,
        grid=(n // tr,),
        in_specs=[pl.BlockSpec((tr, d), lambda i: (i, 0)),
                  pl.BlockSpec((N_EXPERTS, d), lambda i: (0, 0)),
                  pl.BlockSpec((N_EXPERTS, 1), lambda i: (0, 0))],
        out_specs=[slab, slab, slab, pl.BlockSpec((N_EXPERTS, 1), lambda i: (0, 0))],
        out_shape=[jax.ShapeDtypeStruct((TOP_K, n), I32), jax.ShapeDtypeStruct((TOP_K, n), F32),
                   jax.ShapeDtypeStruct((TOP_K, n), I32), jax.ShapeDtypeStruct((N_EXPERTS, 1), F32)],
        scratch_shapes=[pltpu.VMEM((N_EXPERTS, 1), F32)],
        compiler_params=_cparams("arbitrary"),
    )(x, router_w.T, router_b.reshape(N_EXPERTS, 1))


LANES = 128
ROW_PAD = 8


def _gather_rows(idx_ref, base, src_hbm, dst_ref, sem, n_rows, spr, src_pitch, dst_pitch):
    unroll = 8
    assert n_rows % unroll == 0

    def body(blk, c):
        for u in range(unroll):
            r = blk * unroll + u
            src = pl.multiple_of(idx_ref[base + r] * src_pitch, SUBLANES)
            dst = pl.multiple_of(r * dst_pitch, SUBLANES)
            pltpu.make_async_copy(src_hbm.at[pl.ds(src, spr)], dst_ref.at[pl.ds(dst, spr)], sem).start()
        return c
    lax.fori_loop(0, n_rows // unroll, body, 0)


def _from_row_linear(ref, first, n_rows, spr, pitch):
    return jnp.concatenate([ref[pl.ds(first * pitch + c, n_rows, stride=pitch), :] for c in range(spr)], axis=1)


def _to_row_linear(ref, val, spr, pitch):
    n_rows = val.shape[0]
    for c in range(spr):
        ref[pl.ds(c, n_rows, stride=pitch), :] = val[:, c * LANES:(c + 1) * LANES]
    for c in range(spr, pitch):
        ref[pl.ds(c, n_rows, stride=pitch), :] = jnp.zeros((n_rows, LANES), val.dtype)


def _expert_kernel(tile_e_ref, n_tiles_ref, rowtok_ref, x_hbm, w1g_ref, w1l_ref, b1g_ref, b1l_ref, w2_ref, b2_ref,
                   o_ref, xbuf, sem):
    i = pl.program_id(0)
    spr = w1g_ref.shape[1] // LANES
    pitch = spr + ROW_PAD
    tm = xbuf.shape[1] // pitch
    slot = i % 2
    n_tiles = n_tiles_ref[0]

    @pl.when(i == 0)
    def _():
        _gather_rows(rowtok_ref, 0, x_hbm, xbuf.at[0], sem.at[0], tm, spr, spr, pitch)

    for nxt in (0, 1):
        @pl.when((i + 1 < n_tiles) & (slot != nxt))
        def _(nxt=nxt):
            _gather_rows(rowtok_ref, (i + 1) * tm, x_hbm, xbuf.at[nxt], sem.at[nxt], tm, spr, spr, pitch)

    @pl.when(i < n_tiles)
    def _():
        done = xbuf.at[slot, pl.ds(0, tm * spr)]
        pltpu.make_async_copy(done, done, sem.at[slot]).wait()
        xs = _from_row_linear(xbuf.at[slot], 0, tm, spr, pitch).astype(BF16)
        glu = jnp.dot(xs, w1g_ref[0], preferred_element_type=F32) + b1g_ref[0]
        lin = jnp.dot(xs, w1l_ref[0], preferred_element_type=F32) + b1l_ref[0]
        glu = jnp.minimum(glu, SWIGLU_LIMIT)
        lin = jnp.clip(lin, -SWIGLU_LIMIT, SWIGLU_LIMIT)
        act = glu * jax.nn.sigmoid(SWIGLU_ALPHA * glu) * (lin + 1.0)
        y = jnp.dot(act.astype(BF16), w2_ref[0], preferred_element_type=F32) + b2_ref[0]
        _to_row_linear(o_ref, y, spr, pitch)

    @pl.when(i >= n_tiles)
    def _():
        o_ref[...] = jnp.zeros_like(o_ref)


def _w1_prep_kernel(w_ref, o_ref):
    two_ff = w_ref.shape[2]
    src = lax.broadcasted_iota(I32, (two_ff, two_ff), 0)
    dst = lax.broadcasted_iota(I32, (two_ff, two_ff), 1)
    want = jnp.where(dst < two_ff // 2, 2 * dst, 2 * (dst - two_ff // 2) + 1)
    perm = (src == want).astype(BF16)
    o_ref[0] = jnp.dot(w_ref[0].astype(BF16), perm, preferred_element_type=F32).astype(BF16)


def _w1_prep(w1, *, tr=2048):
    e, d, two_ff = w1.shape
    tr = min(tr, d)
    blk = pl.BlockSpec((1, tr, two_ff), lambda i, j: (i, j, 0))
    return pl.pallas_call(
        _w1_prep_kernel,
        grid=(e, d // tr),
        in_specs=[blk],
        out_specs=blk,
        out_shape=jax.ShapeDtypeStruct(w1.shape, BF16),
        compiler_params=_cparams("parallel", "parallel"),
    )(w1)


def _expert_ffn(x_lin, tile_e, n_tiles, rowtok, w1p, b1g, b1l, w2, b2, *, max_tiles):
    tm = MOE_TILE
    d, ff = w1p.shape[1], w1p.shape[2] // 2
    spr = d // LANES
    pitch = spr + ROW_PAD
    wmap = lambda i, te, nt, rt: (te[i], 0, 0)
    w1g = w1l = w1p
    return pl.pallas_call(
        _expert_kernel,
        grid_spec=pltpu.PrefetchScalarGridSpec(
            num_scalar_prefetch=3,
            grid=(max_tiles,),
            in_specs=[pl.BlockSpec(memory_space=pl.ANY),
                      pl.BlockSpec((1, d, ff), wmap), pl.BlockSpec((1, d, ff), lambda i, te, nt, rt: (te[i], 0, 1)),
                      pl.BlockSpec((1, 1, ff), wmap), pl.BlockSpec((1, 1, ff), wmap),
                      pl.BlockSpec((1, ff, d), wmap), pl.BlockSpec((1, 1, d), wmap)],
            out_specs=pl.BlockSpec((tm * pitch, LANES), lambda i, te, nt, rt: (i, 0)),
            scratch_shapes=[pltpu.VMEM((2, tm * pitch, LANES), F32), pltpu.SemaphoreType.DMA((2,))]),
        out_shape=jax.ShapeDtypeStruct((max_tiles * tm * pitch, LANES), F32),
        compiler_params=_cparams("arbitrary"),
    )(tile_e, n_tiles, rowtok, x_lin, w1g, w1l, b1g, b1l, w2, b2)


def _combine_kernel(pos_ref, ys_hbm, x_ref, gate_ref, g_ref, b_ref, of_ref, ob_ref, ybuf, sem):
    i = pl.program_id(0)
    nsteps = pl.num_programs(0)
    tt, d = x_ref.shape
    spr = d // LANES
    pitch = spr + ROW_PAD
    n = nsteps * tt
    slot = i % 2

    def fetch(step, s):
        for k in range(TOP_K):
            _gather_rows(pos_ref, k * n + step * tt, ys_hbm, ybuf.at[s, pl.ds(k * tt * pitch, tt * pitch)],
                         sem.at[s], tt, spr, pitch, pitch)

    @pl.when(i == 0)
    def _():
        fetch(0, 0)

    for nxt in (0, 1):
        @pl.when((i + 1 < nsteps) & (slot != nxt))
        def _(nxt=nxt):
            fetch(i + 1, nxt)

    done = ybuf.at[slot, pl.ds(0, TOP_K * tt * spr)]
    pltpu.make_async_copy(done, done, sem.at[slot]).wait()
    gate = gate_ref[...]
    moe = None
    for k in range(TOP_K):
        term = gate[:, k:k + 1] * _from_row_linear(ybuf.at[slot], k * tt, tt, spr, pitch)
        moe = term if moe is None else moe + term
    out = _ln_rows(DEEPNORM_ALPHA * x_ref[...] + moe, g_ref[...], b_ref[...])
    of_ref[...] = out
    ob_ref[...] = out.astype(BF16)


def _combine_ln(pos_flat, ys, x, gate_t, g, b):
    n, d = x.shape
    tt = min(COMBINE_TILE, n)
    pitch = d // LANES + ROW_PAD
    row = pl.BlockSpec((tt, d), lambda i, p: (i, 0))
    vec = pl.BlockSpec((1, d), lambda i, p: (0, 0))
    return pl.pallas_call(
        _combine_kernel,
        grid_spec=pltpu.PrefetchScalarGridSpec(
            num_scalar_prefetch=1,
            grid=(n // tt,),
            in_specs=[pl.BlockSpec(memory_space=pl.ANY), row,
                      pl.BlockSpec((tt, TOP_K), lambda i, p: (i, 0)), vec, vec],
            out_specs=[row, row],
            scratch_shapes=[pltpu.VMEM((2, TOP_K * tt * pitch, LANES), F32), pltpu.SemaphoreType.DMA((2,))]),
        out_shape=[jax.ShapeDtypeStruct((n, d), F32), jax.ShapeDtypeStruct((n, d), BF16)],
        compiler_params=_cparams("arbitrary"),
    )(pos_flat, ys, x, gate_t, g.reshape(1, d), b.reshape(1, d))


def _moe_block(x_f32, x_lin, router_w, router_b, w1, b1, w2, b2, ln_g, ln_b):
    n, d = x_f32.shape
    tm = MOE_TILE
    top_i, gate, rank, counts = _router(x_f32, router_w, router_b)
    counts = counts[:, 0].astype(I32)
    padded = (counts + tm - 1) // tm * tm
    ends = jnp.cumsum(padded)
    starts = ends - padded
    max_tiles = (n * TOP_K) // tm + N_EXPERTS
    e_ids = jnp.arange(N_EXPERTS, dtype=I32)
    start_of = jnp.sum(jnp.where(top_i[:, :, None] == e_ids, starts, 0), axis=-1)
    pos = start_of + rank
    tile_start = jnp.arange(max_tiles, dtype=I32) * tm
    tile_e = jnp.minimum(jnp.sum(tile_start[:, None] >= ends[None, :], axis=1), N_EXPERTS - 1).astype(I32)
    n_tiles = (ends[-1] // tm).astype(I32).reshape(1)
    tok = jnp.broadcast_to(jnp.arange(n, dtype=I32)[None, :], (TOP_K, n))
    rowtok = jnp.zeros((max_tiles * tm,), I32).at[pos.reshape(-1)].set(tok.reshape(-1))
    b1g = b1[:, None, 0::2]
    b1l = b1[:, None, 1::2]
    ys = _expert_ffn(x_lin, tile_e, n_tiles, rowtok, _w1_prep(w1), b1g, b1l, w2.astype(BF16), b2[:, None, :],
                     max_tiles=max_tiles)
    return _combine_ln(pos.reshape(-1), ys, x_f32, gate.T, ln_g, ln_b)


DIL_BAND = 128
DIL_HEADS_PER_STEP = 4
DIL_MASKED = -1e30


def _dil_attn_kernel(q_ref, kp_ref, ko_ref, vp_ref, vo_ref, o_ref, lse_ref):
    nblk = pl.program_id(2)
    w = DIL_BAND
    e = DIL_HEAD_DIM
    scale = e ** -0.5
    qi = lax.broadcasted_iota(I32, (w, w), 0)
    kj = lax.broadcasted_iota(I32, (w, w), 1)
    own_ok = kj <= qi
    prev_ok = (kj >= qi) & (nblk > 0)
    contract_last = (((1,), (1,)), ((), ()))
    heads = range(DIL_HEADS_PER_STEP)
    cols = [slice(hh * e, (hh + 1) * e) for hh in heads]
    s_own = [lax.dot_general(q_ref[0, :, c], ko_ref[0, :, c], contract_last, preferred_element_type=F32) for c in cols]
    s_prev = [lax.dot_general(q_ref[0, :, c], kp_ref[0, :, c], contract_last, preferred_element_type=F32) for c in cols]
    s_own = [jnp.where(own_ok, s * scale, DIL_MASKED) for s in s_own]
    s_prev = [jnp.where(prev_ok, s * scale, DIL_MASKED) for s in s_prev]
    m = [jnp.maximum(jnp.max(a, axis=1, keepdims=True), jnp.max(b, axis=1, keepdims=True))
         for a, b in zip(s_own, s_prev)]
    p_own = [jnp.exp(s - mm) for s, mm in zip(s_own, m)]
    p_prev = [jnp.exp(s - mm) for s, mm in zip(s_prev, m)]
    l = [jnp.sum(a, axis=1, keepdims=True) + jnp.sum(b, axis=1, keepdims=True) for a, b in zip(p_own, p_prev)]
    o = [jnp.dot(p.astype(BF16), vo_ref[0, :, c], preferred_element_type=F32) for p, c in zip(p_own, cols)]
    o = [acc + jnp.dot(p.astype(BF16), vp_ref[0, :, c], preferred_element_type=F32)
         for acc, p, c in zip(o, p_prev, cols)]
    for hh in heads:
        o_ref[0, :, cols[hh]] = o[hh] / l[hh]
        lse_ref[0, :, cols[hh]] = jnp.broadcast_to(m[hh] + jnp.log(l[hh]), (w, e))


def _dil_group_attention(h, dilation, *, batch):
    gi = 0
    n, wid = h.shape
    seq = n // batch
    d = dilation
    ls = seq // d
    nb = ls // DIL_BAND
    hw = DIL_HEADS_PER_STEP * DIL_HEAD_DIM
    hsteps = DIL_OUT // hw
    wb = wid // hw
    assert wid % hw == 0 and ls % DIL_BAND == 0
    hv = h.reshape(batch, ls, d * wid)
    base = lambda j: (gi * 3 + j) * hsteps
    own = lambda j: pl.BlockSpec((1, DIL_BAND, hw), lambda b, r, s, c: (b, s, r * wb + base(j) + c))
    prev = lambda j: pl.BlockSpec((1, DIL_BAND, hw),
                                  lambda b, r, s, c: (b, jnp.maximum(s - 1, 0), r * wb + base(j) + c))
    out = pl.BlockSpec((1, DIL_BAND, hw), lambda b, r, s, c: (b, s, r * hsteps + c))
    o, lse = pl.pallas_call(
        _dil_attn_kernel,
        grid=(batch, d, nb, hsteps),
        in_specs=[own(0), prev(1), own(1), prev(2), own(2)],
        out_specs=[out, out],
        out_shape=[jax.ShapeDtypeStruct((batch, ls, d * DIL_OUT), F32)] * 2,
        compiler_params=_cparams("parallel", "parallel", "parallel", "parallel"),
    )(hv, hv, hv, hv, hv)
    return o.reshape(n, DIL_OUT), lse.reshape(n, DIL_OUT)


def _dil_combine_kernel(o0, o1, o2, l0, l1, l2, y_ref):
    a, b, c = l0[...], l1[...], l2[...]
    m = jnp.maximum(jnp.maximum(a, b), c)
    ea, eb, ec = jnp.exp(a - m), jnp.exp(b - m), jnp.exp(c - m)
    y_ref[...] = ((ea * o0[...] + eb * o1[...] + ec * o2[...]) / (ea + eb + ec)).astype(y_ref.dtype)


def _dilated_mixer_pallas(h_groups, *, batch):
    n = h_groups[0].shape[0]
    outs, lses = [], []
    for h, (window, dilation) in zip(h_groups, DIL_GROUPS):
        assert window // dilation == DIL_BAND
        o, lse = _dil_group_attention(h, dilation, batch=batch)
        outs.append(o)
        lses.append(lse)
    tt = min(512, n)
    blk = pl.BlockSpec((tt, 512), lambda i, c: (i, c))
    return pl.pallas_call(
        _dil_combine_kernel,
        grid=(n // tt, DIL_OUT // 512),
        in_specs=[blk] * 6,
        out_specs=blk,
        out_shape=jax.ShapeDtypeStruct((n, DIL_OUT), BF16),
        compiler_params=_cparams("parallel", "parallel"),
    )(*outs, *lses)


GDN_HEAD_GROUP = 4
GDN_CHUNK_GROUP = 8
GDN_PREP_COLS = 1024
SUBLANES = 8


def _gdn_prep_kernel(x_ref, halo_ref, w_ref, o_ref, *, tiles_per_seq):
    i = pl.program_id(0)
    c = pl.program_id(1)
    x = x_ref[...]
    tt = x.shape[0]
    halo = jnp.where(i % tiles_per_seq == 0, 0.0, halo_ref[...])
    row8 = lax.broadcasted_iota(I32, halo.shape, 0)
    w = w_ref[...]
    y = x * w[GDN_CONV_TAPS - 1:GDN_CONV_TAPS, :]
    for s in range(1, GDN_CONV_TAPS):
        xs = pltpu.roll(x, s, axis=0)
        first = jnp.where(row8 < s, pltpu.roll(halo, s, axis=0), xs[:SUBLANES])
        xs = jnp.concatenate([first, xs[SUBLANES:]], axis=0)
        y = y + xs * w[GDN_CONV_TAPS - 1 - s:GDN_CONV_TAPS - s, :]
    y = y * jax.nn.sigmoid(y)
    n_qk_tiles = 2 * GDN_KEY_DIM // GDN_PREP_COLS

    @pl.when(c >= n_qk_tiles)
    def _():
        o_ref[...] = y

    @pl.when(c < n_qk_tiles)
    def _():
        qscale = jnp.where(c < n_qk_tiles // 2, GDN_HEAD_DIM ** -0.5, 1.0)
        for g in range(GDN_PREP_COLS // GDN_HEAD_DIM):
            ys = y[:, g * GDN_HEAD_DIM:(g + 1) * GDN_HEAD_DIM]
            inv = lax.rsqrt(jnp.sum(ys * ys, axis=-1, keepdims=True) + RMS_EPS)
            o_ref[:, g * GDN_HEAD_DIM:(g + 1) * GDN_HEAD_DIM] = ys * inv * qscale


def _gdn_prep(h_main, conv_w, *, seq):
    n = h_main.shape[0]
    tt = min(256, seq)
    cw = GDN_PREP_COLS
    return pl.pallas_call(
        functools.partial(_gdn_prep_kernel, tiles_per_seq=seq // tt),
        grid=(n // tt, GDN_CONV_DIM // cw),
        in_specs=[pl.BlockSpec((tt, cw), lambda i, c: (i, c)),
                  pl.BlockSpec((SUBLANES, cw), lambda i, c: (jnp.maximum(i * (tt // SUBLANES) - 1, 0), c)),
                  pl.BlockSpec((GDN_CONV_TAPS, cw), lambda i, c: (0, c))],
        out_specs=pl.BlockSpec((tt, cw), lambda i, c: (i, c)),
        out_shape=jax.ShapeDtypeStruct((n, GDN_CONV_DIM), F32),
        compiler_params=_cparams("parallel", "parallel"),
    )(h_main, h_main, conv_w)


def _gdn_gate_kernel(ba_ref, alog_ref, dtb_ref, beta_ref, g_ref):
    ba = ba_ref[...]
    beta = jax.nn.sigmoid(ba)
    xa = ba + dtb_ref[...]
    softplus = jnp.maximum(xa, 0.0) + jnp.log(1.0 + jnp.exp(-jnp.abs(xa)))
    g = -jnp.exp(alog_ref[...]) * softplus
    hg = GDN_HEAD_GROUP
    for j in range(GDN_V_HEADS // hg):
        beta_ref[j] = beta[:, j * hg:(j + 1) * hg]
        g_ref[j] = g[:, GDN_V_HEADS + j * hg:GDN_V_HEADS + (j + 1) * hg]


def _gdn_gates(h_tail, a_log, dt_bias):
    n = h_tail.shape[0]
    tt = min(512, n)
    ng = GDN_V_HEADS // GDN_HEAD_GROUP
    lane_vec = lambda v: jnp.zeros((1, LANES), F32).at[0, GDN_V_HEADS:2 * GDN_V_HEADS].set(v)
    out = pl.BlockSpec((ng, tt, GDN_HEAD_GROUP), lambda i: (0, i, 0))
    return pl.pallas_call(
        _gdn_gate_kernel,
        grid=(n // tt,),
        in_specs=[pl.BlockSpec((tt, LANES), lambda i: (i, MEM_Q // LANES)),
                  pl.BlockSpec((1, LANES), lambda i: (0, 0)), pl.BlockSpec((1, LANES), lambda i: (0, 0))],
        out_specs=[out, out],
        out_shape=[jax.ShapeDtypeStruct((ng, n, GDN_HEAD_GROUP), F32)] * 2,
        compiler_params=_cparams("parallel"),
    )(h_tail, lane_vec(a_log), lane_vec(dt_bias))


def _dot_tril(ones_b, x):
    acc = None
    for _ in range(3):
        x_b = x.astype(BF16)
        d = jnp.dot(ones_b, x_b, preferred_element_type=F32)
        acc = d if acc is None else acc + d
        x = x - x_b.astype(F32)
    return acc


def _gdn_scan_kernel(q_ref, k_ref, v_ref, z_ref, beta_ref, g_ref, ng_ref, o_ref, state_ref):
    C = GDN_CHUNK
    hd = GDN_HEAD_DIM

    @pl.when(pl.program_id(2) == 0)
    def _():
        state_ref[...] = jnp.zeros_like(state_ref)

    r_i = lax.broadcasted_iota(I32, (C, C), 0)
    c_i = lax.broadcasted_iota(I32, (C, C), 1)
    causal = c_i <= r_i
    strict = c_i < r_i
    tril = causal.astype(F32)
    later = (r_i > c_i).astype(F32)
    eye = (r_i == c_i).astype(F32)
    contract_last = (((1,), (1,)), ((), ()))
    norm_g = ng_ref[...]

    tril_b = tril.astype(BF16)
    chunk_heads = [(c, j) for c in range(GDN_CHUNK_GROUP) for j in range(GDN_HEAD_GROUP)]
    rows = lambda c: slice(c * C, (c + 1) * C)
    hcols = lambda j: slice(j * hd, (j + 1) * hd)

    gc_all = [_dot_tril(tril_b, g_ref[0, rows(c), :]) for c in range(GDN_CHUNK_GROUP)]
    kk, qk = {}, {}
    for c in range(GDN_CHUNK_GROUP):
        for kh in range(GDN_HEAD_GROUP // 2):
            k_b = k_ref[rows(c), hcols(kh)].astype(BF16)
            q_b = q_ref[rows(c), hcols(kh)].astype(BF16)
            kk[c, kh] = lax.dot_general(k_b, k_b, contract_last, preferred_element_type=F32)
            qk[c, kh] = lax.dot_general(q_b, k_b, contract_last, preferred_element_type=F32)
    decay, mpow, tinv = {}, {}, {}
    for c, j in chunk_heads:
        diff = _dot_tril(tril_b, g_ref[0, rows(c), j:j + 1] * later)
        decay[c, j] = jnp.where(causal, jnp.exp(jnp.where(causal, diff, 0.0)), 0.0)
        lower = jnp.where(strict, beta_ref[0, rows(c), j:j + 1] * kk[c, j // 2] * decay[c, j], 0.0)
        mpow[c, j] = -lower
        tinv[c, j] = eye - lower
    for _ in range(5):
        for cj in chunk_heads:
            m_b = mpow[cj].astype(BF16)
            mpow[cj] = jnp.dot(m_b, m_b, preferred_element_type=F32)
        for cj in chunk_heads:
            tinv[cj] = tinv[cj] + jnp.dot(tinv[cj].astype(BF16), mpow[cj].astype(BF16), preferred_element_type=F32)
    u, w, a_intra, q_dec, k_dec_t, g_last = {}, {}, {}, {}, {}, {}
    for c, j in chunk_heads:
        k_h = k_ref[rows(c), hcols(j // 2)]
        beta = beta_ref[0, rows(c), j:j + 1]
        gc = gc_all[c][:, j:j + 1]
        egc = jnp.exp(gc)
        gc_last = gc[C - 1:C, :]
        tinv_b = tinv[c, j].astype(BF16)
        u[c, j] = jnp.dot(tinv_b, (v_ref[rows(c), hcols(j)] * beta).astype(BF16), preferred_element_type=F32)
        w[c, j] = jnp.dot(tinv_b, (k_h * (beta * egc)).astype(BF16), preferred_element_type=F32).astype(BF16)
        a_intra[c, j] = jnp.where(causal, qk[c, j // 2] * decay[c, j], 0.0).astype(BF16)
        q_dec[c, j] = (q_ref[rows(c), hcols(j // 2)] * egc).astype(BF16)
        k_dec_t[c, j] = (k_h * jnp.exp(gc_last - gc)).T.astype(BF16)
        g_last[c, j] = jnp.exp(gc_last)

    heads = range(GDN_HEAD_GROUP)
    state = [state_ref[j] for j in heads]
    for c in range(GDN_CHUNK_GROUP):
        state_b = [state[j].astype(BF16) for j in heads]
        v_new = [u[c, j] - jnp.dot(w[c, j], state_b[j], preferred_element_type=F32) for j in heads]
        v_new_b = [v.astype(BF16) for v in v_new]
        state = [state[j] * g_last[c, j] + jnp.dot(k_dec_t[c, j], v_new_b[j], preferred_element_type=F32)
                 for j in heads]
        for j in heads:
            o = jnp.dot(q_dec[c, j], state_b[j], preferred_element_type=F32) + jnp.dot(
                a_intra[c, j], v_new_b[j], preferred_element_type=F32)
            o = o * lax.rsqrt(jnp.mean(o * o, axis=-1, keepdims=True) + RMS_EPS) * norm_g
            z = z_ref[rows(c), hcols(j)]
            o_ref[rows(c), hcols(j)] = (o * (z * jax.nn.sigmoid(z))).astype(o_ref.dtype)
    for j in heads:
        state_ref[j] = state[j]


def _gdn_scan(qkv, h_main, beta, g, norm_g, *, batch):
    n = qkv.shape[0]
    seq = n // batch
    tr = GDN_CHUNK_GROUP * GDN_CHUNK
    nt = seq // tr
    hg = GDN_HEAD_GROUP
    kw = (hg // 2) * GDN_HEAD_DIM
    vw = hg * GDN_HEAD_DIM
    row = lambda b, h, t: b * nt + t
    return pl.pallas_call(
        _gdn_scan_kernel,
        grid=(batch, GDN_V_HEADS // hg, nt),
        in_specs=[pl.BlockSpec((tr, kw), lambda b, h, t: (row(b, h, t), h)),
                  pl.BlockSpec((tr, kw), lambda b, h, t: (row(b, h, t), GDN_KEY_DIM // kw + h)),
                  pl.BlockSpec((tr, vw), lambda b, h, t: (row(b, h, t), 2 * GDN_KEY_DIM // vw + h)),
                  pl.BlockSpec((tr, vw), lambda b, h, t: (row(b, h, t), GDN_CONV_DIM // vw + h)),
                  pl.BlockSpec((1, tr, hg), lambda b, h, t: (h, row(b, h, t), 0)),
                  pl.BlockSpec((1, tr, hg), lambda b, h, t: (h, row(b, h, t), 0)),
                  pl.BlockSpec((1, GDN_HEAD_DIM), lambda b, h, t: (0, 0))],
        out_specs=pl.BlockSpec((tr, vw), lambda b, h, t: (row(b, h, t), h)),
        out_shape=jax.ShapeDtypeStruct((n, GDN_VAL_DIM), BF16),
        scratch_shapes=[pltpu.VMEM((hg, GDN_HEAD_DIM, GDN_HEAD_DIM), F32)],
        compiler_params=_cparams("parallel", "parallel", "arbitrary"),
    )(qkv, qkv, qkv, h_main, beta, g, norm_g.reshape(1, GDN_HEAD_DIM))


def _gdn_in_weights(w_in):
    n_main = GDN_CONV_DIM + GDN_VAL_DIM
    n_ba = 2 * GDN_V_HEADS
    pad = jnp.zeros((w_in.shape[0], LANES - n_ba), w_in.dtype)
    tail = jnp.concatenate([w_in[:, n_main + n_ba:], w_in[:, n_main:n_main + n_ba], pad], axis=1)
    return w_in[:, :n_main], tail


def _gdn_mixer_pallas(h_main, h_tail, mix_params, *, batch):
    conv_w, a_log, dt_bias, norm_g = mix_params
    seq = h_main.shape[0] // batch
    qkv = _gdn_prep(h_main, conv_w, seq=seq)
    beta, g = _gdn_gates(h_tail, a_log, dt_bias)
    return _gdn_scan(qkv, h_main, beta, g, norm_g, batch=batch)


DSA_TQ = 128
DSA_TK = 512
DSA_HEAD_GROUP = 16
DSA_ATTN_TK = 1024
DSA_INDEX_TQ = 128
DSA_INDEX_HEAD_GROUP = 8
DSA_KEY_W = DSA_KV_LORA + LANES
MASK_BIAS = -2e30
M_INIT = -1e30
I32_MIN = -2 ** 31


def _rope_partner(x, half):
    ax = x.ndim - 1
    n = x.shape[ax]
    lane = lax.broadcasted_iota(I32, x.shape, ax) % (2 * half)
    return jnp.where(lane < half, -pltpu.roll(x, n - half, axis=ax), pltpu.roll(x, half, axis=ax))


def _dsa_prep_kernel(ql_ref, kv_ref, ik_ref, misc_ref, cos_ref, sin_ref, cosi_ref, sini_ref,
                     qg_ref, kvg_ref, ikg_ref, ikb_ref,
                     cq_ref, key_ref, iko_ref, iw_ref):
    half = DSA_ROPE_DIM // 2
    ql = ql_ref[...]
    cq_ref[...] = (ql * lax.rsqrt(jnp.mean(ql * ql, axis=-1, keepdims=True) + RMS_EPS) * qg_ref[...]).astype(BF16)
    kv = kv_ref[...]
    key_ref[:, :DSA_KV_LORA] = (
        kv * lax.rsqrt(jnp.mean(kv * kv, axis=-1, keepdims=True) + RMS_EPS) * kvg_ref[...]).astype(BF16)
    misc = misc_ref[...]
    lane = lax.broadcasted_iota(I32, misc.shape, 1)
    kr = jnp.where(lane < DSA_ROPE_DIM, misc * cos_ref[...] + _rope_partner(misc, half) * sin_ref[...], 0.0)
    key_ref[:, DSA_KV_LORA:] = kr.astype(BF16)
    iw = misc * (IDX_HEADS ** -0.5 * IDX_DIM ** -0.5)
    for h in range(IDX_HEADS):
        iw_ref[h] = iw[:, DSA_ROPE_DIM + h:DSA_ROPE_DIM + h + 1]
    ik = _ln_rows(ik_ref[...], ikg_ref[...], ikb_ref[...])
    iko_ref[...] = (ik * cosi_ref[...] + _rope_partner(ik, half) * sini_ref[...]).astype(BF16)


def _dsa_prep(h, tabs, q_norm_g, kv_norm_g, ik_g, ik_b, *, seq):
    n = h.shape[0]
    tq = min(256, seq)
    nt = seq // tq
    cos4, sin4, cosi, sini = tabs
    tab = pl.BlockSpec((tq, LANES), lambda i: (i % nt, 0))
    vec = lambda w: pl.BlockSpec((1, w), lambda i: (0, 0))
    blk = lambda w, j: pl.BlockSpec((tq, w), lambda i: (i, j))
    return pl.pallas_call(
        _dsa_prep_kernel,
        grid=(n // tq,),
        in_specs=[blk(DSA_Q_LORA, 0), blk(DSA_KV_LORA, 3), blk(LANES, 16), blk(LANES, 17), tab, tab, tab, tab,
                  vec(DSA_Q_LORA), vec(DSA_KV_LORA), vec(IDX_DIM), vec(IDX_DIM)],
        out_specs=[blk(DSA_Q_LORA, 0), blk(DSA_KEY_W, 0), blk(IDX_DIM, 0),
                   pl.BlockSpec((IDX_HEADS, tq, 1), lambda i: (0, i, 0))],
        out_shape=[jax.ShapeDtypeStruct((n, DSA_Q_LORA), BF16), jax.ShapeDtypeStruct((n, DSA_KEY_W), BF16),
                   jax.ShapeDtypeStruct((n, IDX_DIM), BF16), jax.ShapeDtypeStruct((IDX_HEADS, n, 1), F32)],
        compiler_params=_cparams("parallel"),
    )(h, h, h, h, cos4, sin4, cosi, sini, q_norm_g.reshape(1, -1), kv_norm_g.reshape(1, -1),
      ik_g.reshape(1, -1), ik_b.reshape(1, -1))


def _dsa_qpost_kernel(qn_ref, qri_ref, wuk_ref, cos_ref, sin_ref, cosi_ref, sini_ref, q_ref, iq_ref):
    half = DSA_ROPE_DIM // 2
    tq = qn_ref.shape[0]
    n_rope = DSA_HEADS * DSA_ROPE_DIM
    scale = (DSA_NOPE_DIM + DSA_ROPE_DIM) ** -0.5
    qr = qri_ref[:, :n_rope]
    cos = jnp.tile(cos_ref[...], (1, n_rope // LANES))
    sin = jnp.tile(sin_ref[...], (1, n_rope // LANES))
    qr = (qr * cos + _rope_partner(qr, half) * sin) * scale
    zeros = jnp.zeros((tq, LANES - DSA_ROPE_DIM), F32)
    cosi, sini = cosi_ref[...], sini_ref[...]
    for h in range(DSA_HEADS):
        q_ref[h, :, :DSA_KV_LORA] = (jnp.dot(
            qn_ref[:, h * DSA_NOPE_DIM:(h + 1) * DSA_NOPE_DIM], wuk_ref[h],
            preferred_element_type=F32) * scale).astype(BF16)
        q_ref[h, :, DSA_KV_LORA:] = jnp.concatenate(
            [qr[:, h * DSA_ROPE_DIM:(h + 1) * DSA_ROPE_DIM], zeros], axis=1).astype(BF16)
        iq = qri_ref[:, n_rope + h * IDX_DIM:n_rope + (h + 1) * IDX_DIM]
        iq_ref[h] = (iq * cosi + _rope_partner(iq, half) * sini).astype(BF16)


def _dsa_qpost(qn, qri, wuk_t, tabs, *, seq):
    n = qn.shape[0]
    tq = min(128, seq)
    nt = seq // tq
    cos4, sin4, cosi, sini = tabs
    tab = pl.BlockSpec((tq, LANES), lambda i: (i % nt, 0))
    row = lambda w: pl.BlockSpec((tq, w), lambda i: (i, 0))
    hm = lambda w: pl.BlockSpec((DSA_HEADS, tq, w), lambda i: (0, i, 0))
    return pl.pallas_call(
        _dsa_qpost_kernel,
        grid=(n // tq,),
        in_specs=[row(qn.shape[1]), row(qri.shape[1]),
                  pl.BlockSpec(wuk_t.shape, lambda i: (0, 0, 0)), tab, tab, tab, tab],
        out_specs=[hm(DSA_KEY_W), hm(IDX_DIM)],
        out_shape=[jax.ShapeDtypeStruct((DSA_HEADS, n, DSA_KEY_W), BF16),
                   jax.ShapeDtypeStruct((IDX_HEADS, n, IDX_DIM), BF16)],
        compiler_params=_cparams("parallel"),
    )(qn, qri, wuk_t, cos4, sin4, cosi, sini)


def _dsa_index_kernel(iq_ref, iw_ref, ik_ref, bias_ref, key_ref, *, topk):
    qi = pl.program_id(1)
    tq = iq_ref.shape[1]
    tk = DSA_TK
    hg = DSA_INDEX_HEAD_GROUP
    n_kt = ((qi + 1) * tq + tk - 1) // tk
    qpos = qi * tq + lax.broadcasted_iota(I32, (tq, tk), 0)
    lane = lax.broadcasted_iota(I32, (tq, tk), 1)

    def score_tile(kt, c):
        k0 = pl.multiple_of(kt * tk, tk)
        keys = ik_ref[pl.ds(k0, tk), :]
        acc = jnp.zeros((tq, tk), F32)
        for g in range(IDX_HEADS // hg):
            lhs = iq_ref[g * hg:(g + 1) * hg].reshape(hg * tq, IDX_DIM)
            s = lax.dot_general(lhs, keys, (((1,), (1,)), ((), ())), preferred_element_type=F32)
            s = jnp.maximum(s, 0.0).reshape(hg, tq, tk) * iw_ref[g * hg:(g + 1) * hg]
            acc = acc + jnp.sum(s, axis=0)
        acc = jnp.where(k0 + lane <= qpos, acc, -jnp.inf)
        bits = pltpu.bitcast(acc, I32)
        key_ref[:, pl.ds(k0, tk)] = jnp.where(bits < 0, bits ^ 0x7FFFFFFF, bits)
        return c

    lax.fori_loop(0, n_kt, score_tile, 0)

    res = jnp.zeros((tq, 1), I32)
    for bit in range(31, -1, -1):
        bit_c = jnp.int32(I32_MIN if bit == 31 else (1 << bit))
        cand = (res | bit_c) ^ jnp.int32(I32_MIN)

        def count_tile(kt, cnt, cand=cand):
            blk = key_ref[:, pl.ds(pl.multiple_of(kt * tk, tk), tk)]
            ge = (blk >= cand).astype(I32)
            part = ge[:, 0:LANES]
            for j in range(1, tk // LANES):
                part = part + ge[:, j * LANES:(j + 1) * LANES]
            return cnt + part

        cnt = lax.fori_loop(0, n_kt, count_tile, jnp.zeros((tq, LANES), I32))
        total = jnp.sum(cnt, axis=1, keepdims=True)
        res = jnp.where(total >= topk, res | bit_c, res)
    thr = res ^ jnp.int32(I32_MIN)

    bias_ref[...] = jnp.full(bias_ref.shape, MASK_BIAS, bias_ref.dtype)

    def bias_tile(kt, c):
        k0 = pl.multiple_of(kt * tk, tk)
        sel = (key_ref[:, pl.ds(k0, tk)] >= thr) & (k0 + lane <= qpos)
        bias_ref[:, pl.ds(k0, tk)] = jnp.where(sel, 0.0, MASK_BIAS).astype(bias_ref.dtype)
        return c

    lax.fori_loop(0, n_kt, bias_tile, 0)


def _dsa_index(iq, iw, ik, *, batch, topk):
    n = iq.shape[1]
    seq = n // batch
    tq = min(DSA_INDEX_TQ, seq)
    nq = seq // tq
    return pl.pallas_call(
        functools.partial(_dsa_index_kernel, topk=topk),
        grid=(batch, nq),
        in_specs=[pl.BlockSpec((IDX_HEADS, tq, IDX_DIM), lambda b, i: (0, b * nq + i, 0)),
                  pl.BlockSpec((IDX_HEADS, tq, 1), lambda b, i: (0, b * nq + i, 0)),
                  pl.BlockSpec((seq, IDX_DIM), lambda b, i: (b, 0))],
        out_specs=pl.BlockSpec((tq, seq), lambda b, i: (b * nq + i, 0)),
        out_shape=jax.ShapeDtypeStruct((n, seq), BF16),
        scratch_shapes=[pltpu.VMEM((tq, seq), I32)],
        compiler_params=_cparams("parallel", "parallel"),
    )(iq, iw, ik)


def _dsa_attn_kernel(q_ref, key_ref, bias_ref, wuv_ref, o_ref, m_ref, l_ref, acc_ref):
    qi = pl.program_id(1)
    kt = pl.program_id(2)
    tq = q_ref.shape[1]
    tk = key_ref.shape[0]
    hg = DSA_HEAD_GROUP
    rows = hg * tq
    last = ((qi + 1) * tq - 1) // tk

    @pl.when(kt == 0)
    def _():
        m_ref[...] = jnp.full(m_ref.shape, M_INIT, F32)
        l_ref[...] = jnp.zeros_like(l_ref)
        acc_ref[...] = jnp.zeros_like(acc_ref)

    @pl.when(kt <= last)
    def _():
        bias = bias_ref[...].astype(F32)
        keys = key_ref[...]
        vals = key_ref[:, :DSA_KV_LORA]
        for g in range(DSA_HEADS // hg):
            r = slice(g * rows, (g + 1) * rows)
            lhs = q_ref[g * hg:(g + 1) * hg].reshape(rows, DSA_KEY_W)
            s = lax.dot_general(lhs, keys, (((1,), (1,)), ((), ())), preferred_element_type=F32)
            s = (s.reshape(hg, tq, tk) + bias).reshape(rows, tk)
            m_prev = m_ref[r]
            m_cur = jnp.maximum(m_prev, jnp.max(s, axis=1, keepdims=True))
            alpha = jnp.exp(m_prev - m_cur)
            p = jnp.exp(s - m_cur[:, :1])
            l_ref[r] = alpha * l_ref[r] + jnp.sum(p, axis=1, keepdims=True)
            acc_ref[r] = acc_ref[r] * alpha[:, :1] + jnp.dot(p.astype(BF16), vals, preferred_element_type=F32)
            m_ref[r] = m_cur

    @pl.when(kt == pl.num_programs(2) - 1)
    def _():
        for h in range(DSA_HEADS):
            r = slice(h * tq, (h + 1) * tq)
            o_lat = (acc_ref[r] / l_ref[r][:, :1]).astype(BF16)
            o_ref[:, h * DSA_V_DIM:(h + 1) * DSA_V_DIM] = jnp.dot(
                o_lat, wuv_ref[h], preferred_element_type=F32).astype(o_ref.dtype)


def _dsa_attention(q, keys, bias, wuv, *, batch):
    n = keys.shape[0]
    seq = n // batch
    tq = min(DSA_TQ, seq)
    tk = min(DSA_ATTN_TK, seq)
    nq, nk = seq // tq, seq // tk
    last = lambda i: ((i + 1) * tq - 1) // tk
    return pl.pallas_call(
        _dsa_attn_kernel,
        grid=(batch, nq, nk),
        in_specs=[pl.BlockSpec((DSA_HEADS, tq, DSA_KEY_W), lambda b, i, k: (0, b * nq + i, 0)),
                  pl.BlockSpec((tk, DSA_KEY_W), lambda b, i, k: (b * nk + jnp.minimum(k, last(i)), 0)),
                  pl.BlockSpec((tq, tk), lambda b, i, k: (b * nq + i, jnp.minimum(k, last(i)))),
                  pl.BlockSpec(wuv.shape, lambda b, i, k: (0, 0, 0))],
        out_specs=pl.BlockSpec((tq, DSA_HEADS * DSA_V_DIM), lambda b, i, k: (b * nq + i, 0)),
        out_shape=jax.ShapeDtypeStruct((n, DSA_HEADS * DSA_V_DIM), BF16),
        scratch_shapes=[pltpu.VMEM((DSA_HEADS * tq, LANES), F32), pltpu.VMEM((DSA_HEADS * tq, LANES), F32),
                        pltpu.VMEM((DSA_HEADS * tq, DSA_KV_LORA), F32)],
        compiler_params=_cparams("parallel", "parallel", "arbitrary"),
    )(q, keys, bias, wuv)


def _dsa_rope_tables(seq):
    inv_freq = ROPE_THETA ** (-jnp.arange(0, DSA_ROPE_DIM, 2, dtype=F32) / DSA_ROPE_DIM)
    ang = jnp.arange(seq, dtype=F32)[:, None] * inv_freq[None, :]
    c, s = jnp.cos(ang), jnp.sin(ang)
    one, zero = jnp.ones_like(c), jnp.zeros_like(c)
    return (jnp.concatenate([c, c, c, c], axis=1), jnp.concatenate([s, s, s, s], axis=1),
            jnp.concatenate([c, c, one, one], axis=1), jnp.concatenate([s, s, zero, zero], axis=1))


def _dsa_in_weight(w_in):
    o1 = DSA_Q_LORA
    o2 = o1 + DSA_KV_LORA
    o3 = o2 + DSA_ROPE_DIM
    o4 = o3 + IDX_DIM
    o5 = o4 + IDX_HEADS
    pad = jnp.zeros((w_in.shape[0], LANES - DSA_ROPE_DIM - IDX_HEADS), w_in.dtype)
    return jnp.concatenate([w_in[:, :o1], w_in[:, o5:], w_in[:, o1:o2], w_in[:, o3:o4], w_in[:, o2:o3],
                            w_in[:, o4:o5], pad], axis=1)


def _dsa_mixer_pallas(h, mix_params, *, batch):
    q_norm_g, kv_norm_g, w_q_up, w_idx_q, ik_g, ik_b, w_kv_up = mix_params
    n = h.shape[0]
    seq = n // batch
    tabs = _dsa_rope_tables(seq)
    cq, keys, ik, iw = _dsa_prep(h, tabs, q_norm_g, kv_norm_g, ik_g, ik_b, seq=seq)
    wq = w_q_up.reshape(DSA_Q_LORA, DSA_HEADS, DSA_NOPE_DIM + DSA_ROPE_DIM).astype(BF16)
    w_nope = wq[:, :, :DSA_NOPE_DIM].reshape(DSA_Q_LORA, -1)
    w_ri = jnp.concatenate([wq[:, :, DSA_NOPE_DIM:].reshape(DSA_Q_LORA, -1), w_idx_q.astype(BF16)], axis=1)
    qn = _matmul([cq], [w_nope], tm=1024, tn=512, out_dtype=BF16)
    qri = _matmul([cq], [w_ri], tm=1024, tn=512, out_dtype=F32)
    w_kv = w_kv_up.reshape(DSA_KV_LORA, DSA_HEADS, DSA_NOPE_DIM + DSA_V_DIM).astype(BF16)
    wuk_t = jnp.transpose(w_kv[:, :, :DSA_NOPE_DIM], (1, 2, 0))
    wuv = jnp.transpose(w_kv[:, :, DSA_NOPE_DIM:], (1, 0, 2))
    q, iq = _dsa_qpost(qn, qri, wuk_t, tabs, seq=seq)
    bias = _dsa_index(iq, iw, ik, batch=batch, topk=min(DSA_TOPK_MAX, seq // 4))
    return _dsa_attention(q, keys, bias, wuv, batch=batch)


def _trunk(x, mem, mem_ln_g, mem_ln_b, w_mem_kv, layers):
    B, T, D = x.shape
    n = B * T
    nm = mem.shape[1]
    mem_n = _ln_plain(mem.reshape(B * nm, D), mem_ln_g, mem_ln_b)
    mem_kv = _matmul([mem_n], [w_mem_kv.astype(BF16)], tm=512, tn=512, out_dtype=BF16).reshape(B, nm, 2 * MEM_Q)
    x_f = x.reshape(n, D)
    x_b = x_f.astype(BF16)
    for i, (w_in, mix_params, w_out, ln1_g, ln1_b, ffn_params, ln2_g, ln2_b) in enumerate(layers):
        kind = i % N_MIXERS
        n_in = w_in.shape[1]
        n_mix = n_in - MEM_Q
        if kind == 2:
            w_in_b = _dsa_in_weight(w_in).astype(BF16)
            h = _matmul([x_b], [w_in_b], tm=1024, tn=_pick_tn(w_in_b.shape[1]), out_dtype=F32)
            y_mix = _dsa_mixer_pallas(h, mix_params, batch=B)
            y_mem = _mem_attention(h, mem_kv, batch=B, col_block=2)
        elif kind == 0:
            w_main, w_tail = _gdn_in_weights(w_in)
            h_main = _matmul([x_b], [w_main.astype(BF16)], tm=1024, tn=512, out_dtype=F32)
            h_tail = _matmul([x_b], [w_tail.astype(BF16)], tm=1024, tn=w_tail.shape[1], out_dtype=F32)
            y_mix = _gdn_mixer_pallas(h_main, h_tail, mix_params, batch=B)
            y_mem = _mem_attention(h_tail, mem_kv, batch=B)
        else:
            w_in_b = w_in.astype(BF16)
            gw = 3 * DIL_OUT
            h_groups = [_matmul([x_b], [w_in_b[:, g * gw:(g + 1) * gw]], tm=1024, tn=512, out_dtype=BF16)
                        for g in range(len(DIL_GROUPS))]
            q_mem = _matmul([x_b], [w_in_b[:, n_mix:]], tm=1024, tn=MEM_Q, out_dtype=BF16)
            y_mix = _dilated_mixer_pallas(h_groups, batch=B)
            y_mem = _mem_attention(q_mem, mem_kv, batch=B)
        w_out_b = w_out.astype(BF16)
        n_out = y_mix.shape[1]
        y = _matmul([y_mix, y_mem], [w_out_b[:n_out], w_out_b[n_out:]], tm=1024, tn=512, out_dtype=F32)
        x_f, x_lin = _ln_residual(x_f, y, ln1_g, ln1_b)
        x_f, x_b = _moe_block(x_f, x_lin, *ffn_params, ln2_g, ln2_b)
    return x_f.reshape(B, T, D)


def _pick_tn(n):
    for tn in (512, 384, 256, 128):
        if n % tn == 0:
            return tn
    return n


def kernel(x, mem, mem_ln_g, mem_ln_b, w_mem_kv, w_in_0, conv_w_0, a_log_0, dt_bias_0, gdn_norm_g_0, w_out_0, ln1_g_0, ln1_b_0, router_w_0, router_b_0, moe_w1_0, moe_b1_0, moe_w2_0, moe_b2_0, ln2_g_0, ln2_b_0, w_in_1, w_out_1, ln1_g_1, ln1_b_1, router_w_1, router_b_1, moe_w1_1, moe_b1_1, moe_w2_1, moe_b2_1, ln2_g_1, ln2_b_1, w_in_2, q_norm_g_2, kv_norm_g_2, w_q_up_2, w_idx_q_2, idx_k_norm_g_2, idx_k_norm_b_2, w_kv_up_2, w_out_2, ln1_g_2, ln1_b_2, router_w_2, router_b_2, moe_w1_2, moe_b1_2, moe_w2_2, moe_b2_2, ln2_g_2, ln2_b_2, w_in_3, conv_w_3, a_log_3, dt_bias_3, gdn_norm_g_3, w_out_3, ln1_g_3, ln1_b_3, router_w_3, router_b_3, moe_w1_3, moe_b1_3, moe_w2_3, moe_b2_3, ln2_g_3, ln2_b_3):
    layers = (
        (w_in_0, (conv_w_0, a_log_0, dt_bias_0, gdn_norm_g_0), w_out_0, ln1_g_0, ln1_b_0,
         (router_w_0, router_b_0, moe_w1_0, moe_b1_0, moe_w2_0, moe_b2_0), ln2_g_0, ln2_b_0),
        (w_in_1, (), w_out_1, ln1_g_1, ln1_b_1,
         (router_w_1, router_b_1, moe_w1_1, moe_b1_1, moe_w2_1, moe_b2_1), ln2_g_1, ln2_b_1),
        (w_in_2, (q_norm_g_2, kv_norm_g_2, w_q_up_2, w_idx_q_2, idx_k_norm_g_2, idx_k_norm_b_2, w_kv_up_2), w_out_2,
         ln1_g_2, ln1_b_2, (router_w_2, router_b_2, moe_w1_2, moe_b1_2, moe_w2_2, moe_b2_2), ln2_g_2, ln2_b_2),
        (w_in_3, (conv_w_3, a_log_3, dt_bias_3, gdn_norm_g_3), w_out_3, ln1_g_3, ln1_b_3,
         (router_w_3, router_b_3, moe_w1_3, moe_b1_3, moe_w2_3, moe_b2_3), ln2_g_3, ln2_b_3),
    )
    return _trunk(x, mem, mem_ln_g, mem_ln_b, w_mem_kv, layers)
```

```python
import functools
import math

import jax
import jax.numpy as jnp
from jax import lax
from jax.experimental import pallas as pl
from jax.experimental.pallas import tpu as pltpu

F32 = jnp.float32
BF16 = jnp.bfloat16
I32 = jnp.int32
HIGHEST = lax.Precision.HIGHEST

V7X_VMEM_LIMIT_BYTES = 56 * 1024 * 1024

DEPTH = 4
N_MIXERS = 3
MEM_HEADS = 4
MEM_HEAD_DIM = 128
MEM_Q = MEM_HEADS * MEM_HEAD_DIM

GDN_K_HEADS = 16
GDN_V_HEADS = 32
GDN_HEAD_DIM = 128
GDN_KEY_DIM = GDN_K_HEADS * GDN_HEAD_DIM
GDN_VAL_DIM = GDN_V_HEADS * GDN_HEAD_DIM
GDN_CONV_DIM = 2 * GDN_KEY_DIM + GDN_VAL_DIM
GDN_CONV_TAPS = 4
GDN_CHUNK = 64

DIL_GROUPS = ((128, 1), (512, 4), (2048, 16))
DIL_HEADS = 16
DIL_HEAD_DIM = 128
DIL_OUT = DIL_HEADS * DIL_HEAD_DIM

DSA_HEADS = 32
DSA_NOPE_DIM = 128
DSA_ROPE_DIM = 64
DSA_V_DIM = 128
DSA_Q_LORA = 1024
DSA_KV_LORA = 512
IDX_HEADS = 32
IDX_DIM = 128
IDX_ROPE_DIM = 64
DSA_TOPK_MAX = 256
ROPE_THETA = 10000.0

N_EXPERTS = 32
TOP_K = 4
MOE_FF = 384
SWIGLU_ALPHA = 1.702
SWIGLU_LIMIT = 7.0

DEEPNORM_ALPHA = (2 * DEPTH) ** 0.25
LN_EPS = 1e-5
RMS_EPS = 1e-6


def _cparams(*sem):
    return pltpu.CompilerParams(dimension_semantics=sem, vmem_limit_bytes=V7X_VMEM_LIMIT_BYTES)


def _mm_kernel(*refs, n_pairs):
    o_ref = refs[-1]
    acc = None
    for p in range(n_pairs):
        d = jnp.dot(refs[p][...], refs[n_pairs + p][...], preferred_element_type=F32)
        acc = d if acc is None else acc + d
    o_ref[...] = acc.astype(o_ref.dtype)


def _matmul(a_list, w_list, *, tm, tn, out_dtype):
    m = a_list[0].shape[0]
    n = w_list[0].shape[1]
    tm = min(tm, m)
    tn = min(tn, n)
    assert m % tm == 0 and n % tn == 0, (m, n, tm, tn)
    in_specs = [pl.BlockSpec((tm, a.shape[1]), lambda i, j: (i, 0)) for a in a_list]
    in_specs += [pl.BlockSpec((w.shape[0], tn), lambda i, j: (0, j)) for w in w_list]
    return pl.pallas_call(
        functools.partial(_mm_kernel, n_pairs=len(a_list)),
        grid=(m // tm, n // tn),
        in_specs=in_specs,
        out_specs=pl.BlockSpec((tm, tn), lambda i, j: (i, j)),
        out_shape=jax.ShapeDtypeStruct((m, n), out_dtype),
        compiler_params=_cparams("parallel", "arbitrary"),
    )(*a_list, *w_list)


def _ln_rows(v, g, b):
    mu = jnp.mean(v, axis=-1, keepdims=True)
    c = v - mu
    var = jnp.mean(c * c, axis=-1, keepdims=True)
    return c * lax.rsqrt(var + LN_EPS) * g + b


def _ln_res_kernel(x_ref, y_ref, g_ref, b_ref, of_ref, ol_ref):
    out = _ln_rows(DEEPNORM_ALPHA * x_ref[...] + y_ref[...], g_ref[...], b_ref[...])
    of_ref[...] = out
    spr = out.shape[1] // LANES
    _to_row_linear(ol_ref, out, spr, spr)


def _ln_residual(x, y, g, b, *, tr=128):
    n, d = x.shape
    tr = min(tr, n)
    spr = d // LANES
    row = pl.BlockSpec((tr, d), lambda i: (i, 0))
    vec = pl.BlockSpec((1, d), lambda i: (0, 0))
    return pl.pallas_call(
        _ln_res_kernel,
        grid=(n // tr,),
        in_specs=[row, row, vec, vec],
        out_specs=[row, pl.BlockSpec((tr * spr, LANES), lambda i: (i, 0))],
        out_shape=[jax.ShapeDtypeStruct((n, d), F32), jax.ShapeDtypeStruct((n * spr, LANES), F32)],
        compiler_params=_cparams("parallel"),
    )(x, y, g.reshape(1, d), b.reshape(1, d))


def _ln_plain_kernel(x_ref, g_ref, b_ref, ob_ref):
    ob_ref[...] = _ln_rows(x_ref[...], g_ref[...], b_ref[...]).astype(BF16)


def _ln_plain(x, g, b, *, tr=128):
    n, d = x.shape
    tr = min(tr, n)
    row = pl.BlockSpec((tr, d), lambda i: (i, 0))
    vec = pl.BlockSpec((1, d), lambda i: (0, 0))
    return pl.pallas_call(
        _ln_plain_kernel,
        grid=(n // tr,),
        in_specs=[row, vec, vec],
        out_specs=row,
        out_shape=jax.ShapeDtypeStruct((n, d), BF16),
        compiler_params=_cparams("parallel"),
    )(x, g.reshape(1, d), b.reshape(1, d))


def _mem_attn_kernel(q_ref, kv_ref, o_ref):
    scale = MEM_HEAD_DIM ** -0.5
    for h in range(MEM_HEADS):
        lo = h * MEM_HEAD_DIM
        q = q_ref[:, lo:lo + MEM_HEAD_DIM].astype(BF16)
        k = kv_ref[0, :, lo:lo + MEM_HEAD_DIM]
        v = kv_ref[0, :, MEM_Q + lo:MEM_Q + lo + MEM_HEAD_DIM]
        s = lax.dot_general(q, k, (((1,), (1,)), ((), ())), preferred_element_type=F32) * scale
        m = jnp.max(s, axis=-1, keepdims=True)
        p = jnp.exp(s - m)
        l = jnp.sum(p, axis=-1, keepdims=True)
        p = (p / l).astype(BF16)
        o_ref[:, lo:lo + MEM_HEAD_DIM] = jnp.dot(p, v, preferred_element_type=F32).astype(o_ref.dtype)


def _mem_attention(q, mem_kv, *, batch, col_block=0, tq=512):
    n = q.shape[0]
    t = n // batch
    tq = min(tq, t)
    nm = mem_kv.shape[1]
    return pl.pallas_call(
        _mem_attn_kernel,
        grid=(batch, t // tq),
        in_specs=[pl.BlockSpec((tq, MEM_Q), lambda b, i: (b * (t // tq) + i, col_block)),
                  pl.BlockSpec((1, nm, 2 * MEM_Q), lambda b, i: (b, 0, 0))],
        out_specs=pl.BlockSpec((tq, MEM_Q), lambda b, i: (b * (t // tq) + i, 0)),
        out_shape=jax.ShapeDtypeStruct((n, MEM_Q), BF16),
        compiler_params=_cparams("parallel", "parallel"),
    )(q, mem_kv)


MOE_TILE = 256
ROUTER_TILE = 512
COMBINE_TILE = 128


def _router_kernel(x_ref, wt_ref, b_ref, idx_ref, gate_ref, rank_ref, cnt_ref, carry_ref):
    _route_tile(x_ref[...], wt_ref, b_ref, idx_ref, gate_ref, rank_ref, cnt_ref, carry_ref)


def _ln_route_kernel(x_ref, y_ref, g_ref, b_ref, wt_ref, rb_ref, of_ref, ol_ref, idx_ref, gate_ref, rank_ref,
                     cnt_ref, carry_ref):
    out = _ln_rows(DEEPNORM_ALPHA * x_ref[...] + y_ref[...], g_ref[...], b_ref[...])
    of_ref[...] = out
    spr = out.shape[1] // LANES
    _to_row_linear(ol_ref, out, spr, spr)
    _route_tile(out, wt_ref, rb_ref, idx_ref, gate_ref, rank_ref, cnt_ref, carry_ref)


def _ln_residual_route(x, y, g, b, router_w, router_b, *, tr=128):
    n, d = x.shape
    tr = min(tr, n)
    spr = d // LANES
    row = pl.BlockSpec((tr, d), lambda i: (i, 0))
    vec = pl.BlockSpec((1, d), lambda i: (0, 0))
    slab = pl.BlockSpec((TOP_K, tr), lambda i: (0, i))
    cnt = pl.BlockSpec((N_EXPERTS, 1), lambda i: (0, 0))
    return pl.pallas_call(
        _ln_route_kernel,
        grid=(n // tr,),
        in_specs=[row, row, vec, vec, pl.BlockSpec((N_EXPERTS, d), lambda i: (0, 0)), cnt],
        out_specs=[row, pl.BlockSpec((tr * spr, LANES), lambda i: (i, 0)), slab, slab, slab, cnt],
        out_shape=[jax.ShapeDtypeStruct((n, d), F32), jax.ShapeDtypeStruct((n * spr, LANES), F32),
                   jax.ShapeDtypeStruct((TOP_K, n), I32), jax.ShapeDtypeStruct((TOP_K, n), F32),
                   jax.ShapeDtypeStruct((TOP_K, n), I32), jax.ShapeDtypeStruct((N_EXPERTS, 1), F32)],
        scratch_shapes=[pltpu.VMEM((N_EXPERTS, 1), F32)],
        compiler_params=_cparams("arbitrary"),
    )(x, y, g.reshape(1, d), b.reshape(1, d), router_w.T, router_b.reshape(N_EXPERTS, 1))


def _route_tile(x, wt_ref, b_ref, idx_ref, gate_ref, rank_ref, cnt_ref, carry_ref):
    tr = x.shape[0]

    @pl.when(pl.program_id(0) == 0)
    def _():
        carry_ref[...] = jnp.zeros_like(carry_ref)

    logits = lax.dot_general(wt_ref[...], x, (((1,), (1,)), ((), ())),
                             precision=HIGHEST, preferred_element_type=F32) + b_ref[...]
    e_iota = lax.broadcasted_iota(I32, (N_EXPERTS, tr), 0)
    work = logits
    vals, onehots = [], []
    for k in range(TOP_K):
        m = jnp.max(work, axis=0, keepdims=True)
        idx = jnp.min(jnp.where(work == m, e_iota, N_EXPERTS), axis=0, keepdims=True)
        oh = e_iota == idx
        vals.append(m)
        onehots.append(oh)
        idx_ref[k:k + 1, :] = idx
        work = jnp.where(oh, -jnp.inf, work)
    exps = [jnp.exp(v - vals[0]) for v in vals]
    denom = exps[0] + exps[1] + exps[2] + exps[3]
    for k in range(TOP_K):
        gate_ref[k:k + 1, :] = exps[k] / denom
    mask = (onehots[0] | onehots[1] | onehots[2] | onehots[3])
    r_i = lax.broadcasted_iota(I32, (tr, tr), 0)
    c_i = lax.broadcasted_iota(I32, (tr, tr), 1)
    before = (r_i < c_i).astype(BF16)
    excl = jnp.dot(mask.astype(BF16), before, preferred_element_type=F32)
    rank = carry_ref[...] + excl
    for k in range(TOP_K):
        rank_ref[k:k + 1, :] = jnp.sum(jnp.where(onehots[k], rank, 0.0), axis=0, keepdims=True).astype(I32)
    carry_ref[...] += jnp.sum(mask.astype(F32), axis=1, keepdims=True)
    cnt_ref[...] = carry_ref[...]


def _router(x, router_w, router_b):
    n, d = x.shape
    tr = min(ROUTER_TILE, n)
    slab = pl.BlockSpec((TOP_K, tr), lambda i: (0, i))
    return pl.pallas_call(
        _router_kernel,
        grid=(n // tr,),
        in_specs=[pl.BlockSpec((tr, d), lambda i: (i, 0)),
                  pl.BlockSpec((N_EXPERTS, d), lambda i: (0, 0)),
                  pl.BlockSpec((N_EXPERTS, 1), lambda i: (0, 0))],
        out_specs=[slab, slab, slab, pl.BlockSpec((N_EXPERTS, 1), lambda i: (0, 0))],
        out_shape=[jax.ShapeDtypeStruct((TOP_K, n), I32), jax.ShapeDtypeStruct((TOP_K, n), F32),
                   jax.ShapeDtypeStruct((TOP_K, n), I32), jax.ShapeDtypeStruct((N_EXPERTS, 1), F32)],
        scratch_shapes=[pltpu.VMEM((N_EXPERTS, 1), F32)],
        compiler_params=_cparams("arbitrary"),
    )(x, router_w.T, router_b.reshape(N_EXPERTS, 1))


LANES = 128
ROW_PAD = 8


def _gather_rows(idx_ref, base, src_hbm, dst_ref, sem, n_rows, spr, src_pitch, dst_pitch):
    unroll = 8
    assert n_rows % unroll == 0

    def body(blk, c):
        for u in range(unroll):
            r = blk * unroll + u
            src = pl.multiple_of(idx_ref[base + r] * src_pitch, SUBLANES)
            dst = pl.multiple_of(r * dst_pitch, SUBLANES)
            pltpu.make_async_copy(src_hbm.at[pl.ds(src, spr)], dst_ref.at[pl.ds(dst, spr)], sem).start()
        return c
    lax.fori_loop(0, n_rows // unroll, body, 0)


def _from_row_linear(ref, first, n_rows, spr, pitch):
    return jnp.concatenate([ref[pl.ds(first * pitch + c, n_rows, stride=pitch), :] for c in range(spr)], axis=1)


def _to_row_linear(ref, val, spr, pitch):
    n_rows = val.shape[0]
    for c in range(spr):
        ref[pl.ds(c, n_rows, stride=pitch), :] = val[:, c * LANES:(c + 1) * LANES]
    for c in range(spr, pitch):
        ref[pl.ds(c, n_rows, stride=pitch), :] = jnp.zeros((n_rows, LANES), val.dtype)


def _expert_kernel(tile_e_ref, n_tiles_ref, rowtok_ref, x_hbm, w1g_ref, w1l_ref, b1g_ref, b1l_ref, w2_ref, b2_ref,
                   o_ref, xbuf, sem):
    i = pl.program_id(0)
    spr = w1g_ref.shape[1] // LANES
    pitch = spr + ROW_PAD
    tm = xbuf.shape[1] // pitch
    slot = i % 2
    n_tiles = n_tiles_ref[0]

    @pl.when(i == 0)
    def _():
        _gather_rows(rowtok_ref, 0, x_hbm, xbuf.at[0], sem.at[0], tm, spr, spr, pitch)

    for nxt in (0, 1):
        @pl.when((i + 1 < n_tiles) & (slot != nxt))
        def _(nxt=nxt):
            _gather_rows(rowtok_ref, (i + 1) * tm, x_hbm, xbuf.at[nxt], sem.at[nxt], tm, spr, spr, pitch)

    @pl.when(i < n_tiles)
    def _():
        done = xbuf.at[slot, pl.ds(0, tm * spr)]
        pltpu.make_async_copy(done, done, sem.at[slot]).wait()
        xs = _from_row_linear(xbuf.at[slot], 0, tm, spr, pitch).astype(BF16)
        glu = jnp.dot(xs, w1g_ref[0], preferred_element_type=F32) + b1g_ref[0]
        lin = jnp.dot(xs, w1l_ref[0], preferred_element_type=F32) + b1l_ref[0]
        glu = jnp.minimum(glu, SWIGLU_LIMIT)
        lin = jnp.clip(lin, -SWIGLU_LIMIT, SWIGLU_LIMIT)
        act = glu * jax.nn.sigmoid(SWIGLU_ALPHA * glu) * (lin + 1.0)
        y = jnp.dot(act.astype(BF16), w2_ref[0], preferred_element_type=F32) + b2_ref[0]
        _to_row_linear(o_ref, y, spr, pitch)

    @pl.when(i >= n_tiles)
    def _():
        o_ref[...] = jnp.zeros_like(o_ref)


def _w1_prep_kernel(w_ref, o_ref):
    two_ff = w_ref.shape[2]
    src = lax.broadcasted_iota(I32, (two_ff, two_ff), 0)
    dst = lax.broadcasted_iota(I32, (two_ff, two_ff), 1)
    want = jnp.where(dst < two_ff // 2, 2 * dst, 2 * (dst - two_ff // 2) + 1)
    perm = (src == want).astype(BF16)
    o_ref[0] = jnp.dot(w_ref[0].astype(BF16), perm, preferred_element_type=F32).astype(BF16)


def _w1_prep(w1, *, tr=2048):
    e, d, two_ff = w1.shape
    tr = min(tr, d)
    blk = pl.BlockSpec((1, tr, two_ff), lambda i, j: (i, j, 0))
    return pl.pallas_call(
        _w1_prep_kernel,
        grid=(e, d // tr),
        in_specs=[blk],
        out_specs=blk,
        out_shape=jax.ShapeDtypeStruct(w1.shape, BF16),
        compiler_params=_cparams("parallel", "parallel"),
    )(w1)


def _expert_ffn(x_lin, tile_e, n_tiles, rowtok, w1p, b1g, b1l, w2, b2, *, max_tiles):
    tm = MOE_TILE
    d, ff = w1p.shape[1], w1p.shape[2] // 2
    spr = d // LANES
    pitch = spr + ROW_PAD
    wmap = lambda i, te, nt, rt: (te[i], 0, 0)
    w1g = w1l = w1p
    return pl.pallas_call(
        _expert_kernel,
        grid_spec=pltpu.PrefetchScalarGridSpec(
            num_scalar_prefetch=3,
            grid=(max_tiles,),
            in_specs=[pl.BlockSpec(memory_space=pl.ANY),
                      pl.BlockSpec((1, d, ff), wmap), pl.BlockSpec((1, d, ff), lambda i, te, nt, rt: (te[i], 0, 1)),
                      pl.BlockSpec((1, 1, ff), wmap), pl.BlockSpec((1, 1, ff), wmap),
                      pl.BlockSpec((1, ff, d), wmap), pl.BlockSpec((1, 1, d), wmap)],
            out_specs=pl.BlockSpec((tm * pitch, LANES), lambda i, te, nt, rt: (i, 0)),
            scratch_shapes=[pltpu.VMEM((2, tm * pitch, LANES), F32), pltpu.SemaphoreType.DMA((2,))]),
        out_shape=jax.ShapeDtypeStruct((max_tiles * tm * pitch, LANES), F32),
        compiler_params=_cparams("arbitrary"),
    )(tile_e, n_tiles, rowtok, x_lin, w1g, w1l, b1g, b1l, w2, b2)


def _combine_kernel(pos_ref, ys_hbm, x_ref, gate_ref, g_ref, b_ref, of_ref, ob_ref, ybuf, sem):
    i = pl.program_id(0)
    nsteps = pl.num_programs(0)
    tt, d = x_ref.shape
    spr = d // LANES
    pitch = spr + ROW_PAD
    n = nsteps * tt
    slot = i % 2

    def fetch(step, s):
        for k in range(TOP_K):
            _gather_rows(pos_ref, k * n + step * tt, ys_hbm, ybuf.at[s, pl.ds(k * tt * pitch, tt * pitch)],
                         sem.at[s], tt, spr, pitch, pitch)

    @pl.when(i == 0)
    def _():
        fetch(0, 0)

    for nxt in (0, 1):
        @pl.when((i + 1 < nsteps) & (slot != nxt))
        def _(nxt=nxt):
            fetch(i + 1, nxt)

    done = ybuf.at[slot, pl.ds(0, TOP_K * tt * spr)]
    pltpu.make_async_copy(done, done, sem.at[slot]).wait()
    gate = gate_ref[...]
    moe = None
    for k in range(TOP_K):
        term = gate[:, k:k + 1] * _from_row_linear(ybuf.at[slot], k * tt, tt, spr, pitch)
        moe = term if moe is None else moe + term
    out = _ln_rows(DEEPNORM_ALPHA * x_ref[...] + moe, g_ref[...], b_ref[...])
    of_ref[...] = out
    ob_ref[...] = out.astype(BF16)


def _combine_ln(pos_flat, ys, x, gate_t, g, b):
    n, d = x.shape
    tt = min(COMBINE_TILE, n)
    pitch = d // LANES + ROW_PAD
    row = pl.BlockSpec((tt, d), lambda i, p: (i, 0))
    vec = pl.BlockSpec((1, d), lambda i, p: (0, 0))
    return pl.pallas_call(
        _combine_kernel,
        grid_spec=pltpu.PrefetchScalarGridSpec(
            num_scalar_prefetch=1,
            grid=(n // tt,),
            in_specs=[pl.BlockSpec(memory_space=pl.ANY), row,
                      pl.BlockSpec((tt, TOP_K), lambda i, p: (i, 0)), vec, vec],
            out_specs=[row, row],
            scratch_shapes=[pltpu.VMEM((2, TOP_K * tt * pitch, LANES), F32), pltpu.SemaphoreType.DMA((2,))]),
        out_shape=[jax.ShapeDtypeStruct((n, d), F32), jax.ShapeDtypeStruct((n, d), BF16)],
        compiler_params=_cparams("arbitrary"),
    )(pos_flat, ys, x, gate_t, g.reshape(1, d), b.reshape(1, d))


def _moe_block(x_f32, x_lin, router_w, router_b, w1, b1, w2, b2, ln_g, ln_b, routing=None):
    n, d = x_f32.shape
    tm = MOE_TILE
    top_i, gate, rank, counts = routing if routing is not None else _router(x_f32, router_w, router_b)
    counts = counts[:, 0].astype(I32)
    padded = (counts + tm - 1) // tm * tm
    ends = jnp.cumsum(padded)
    starts = ends - padded
    max_tiles = (n * TOP_K) // tm + N_EXPERTS
    e_ids = jnp.arange(N_EXPERTS, dtype=I32)
    start_of = jnp.sum(jnp.where(top_i[:, :, None] == e_ids, starts, 0), axis=-1)
    pos = start_of + rank
    tile_start = jnp.arange(max_tiles, dtype=I32) * tm
    tile_e = jnp.minimum(jnp.sum(tile_start[:, None] >= ends[None, :], axis=1), N_EXPERTS - 1).astype(I32)
    n_tiles = (ends[-1] // tm).astype(I32).reshape(1)
    tok = jnp.broadcast_to(jnp.arange(n, dtype=I32)[None, :], (TOP_K, n))
    rowtok = jnp.zeros((max_tiles * tm,), I32).at[pos.reshape(-1)].set(tok.reshape(-1))
    b1g = b1[:, None, 0::2]
    b1l = b1[:, None, 1::2]
    ys = _expert_ffn(x_lin, tile_e, n_tiles, rowtok, _w1_prep(w1), b1g, b1l, w2.astype(BF16), b2[:, None, :],
                     max_tiles=max_tiles)
    return _combine_ln(pos.reshape(-1), ys, x_f32, gate.T, ln_g, ln_b)


DIL_BAND = 128
DIL_HEADS_PER_STEP = 4
DIL_MASKED = -1e30


def _dil_attn_kernel(q_ref, kp_ref, ko_ref, vp_ref, vo_ref, o_ref, lse_ref):
    nblk = pl.program_id(2)
    w = DIL_BAND
    e = DIL_HEAD_DIM
    scale = e ** -0.5
    qi = lax.broadcasted_iota(I32, (w, w), 0)
    kj = lax.broadcasted_iota(I32, (w, w), 1)
    own_ok = kj <= qi
    prev_ok = (kj >= qi) & (nblk > 0)
    contract_last = (((1,), (1,)), ((), ()))
    heads = range(DIL_HEADS_PER_STEP)
    cols = [slice(hh * e, (hh + 1) * e) for hh in heads]
    s_own = [lax.dot_general(q_ref[0, :, c], ko_ref[0, :, c], contract_last, preferred_element_type=F32) for c in cols]
    s_prev = [lax.dot_general(q_ref[0, :, c], kp_ref[0, :, c], contract_last, preferred_element_type=F32) for c in cols]
    s_own = [jnp.where(own_ok, s * scale, DIL_MASKED) for s in s_own]
    s_prev = [jnp.where(prev_ok, s * scale, DIL_MASKED) for s in s_prev]
    m = [jnp.maximum(jnp.max(a, axis=1, keepdims=True), jnp.max(b, axis=1, keepdims=True))
         for a, b in zip(s_own, s_prev)]
    p_own = [jnp.exp(s - mm) for s, mm in zip(s_own, m)]
    p_prev = [jnp.exp(s - mm) for s, mm in zip(s_prev, m)]
    l = [jnp.sum(a, axis=1, keepdims=True) + jnp.sum(b, axis=1, keepdims=True) for a, b in zip(p_own, p_prev)]
    o = [jnp.dot(p.astype(BF16), vo_ref[0, :, c], preferred_element_type=F32) for p, c in zip(p_own, cols)]
    o = [acc + jnp.dot(p.astype(BF16), vp_ref[0, :, c], preferred_element_type=F32)
         for acc, p, c in zip(o, p_prev, cols)]
    for hh in heads:
        o_ref[0, :, cols[hh]] = o[hh] / l[hh]
        lse_ref[0, :, cols[hh]] = jnp.broadcast_to(m[hh] + jnp.log(l[hh]), (w, e))


def _dil_group_attention(h, dilation, *, batch):
    gi = 0
    n, wid = h.shape
    seq = n // batch
    d = dilation
    ls = seq // d
    nb = ls // DIL_BAND
    hw = DIL_HEADS_PER_STEP * DIL_HEAD_DIM
    hsteps = DIL_OUT // hw
    wb = wid // hw
    assert wid % hw == 0 and ls % DIL_BAND == 0
    hv = h.reshape(batch, ls, d * wid)
    base = lambda j: (gi * 3 + j) * hsteps
    own = lambda j: pl.BlockSpec((1, DIL_BAND, hw), lambda b, r, s, c: (b, s, r * wb + base(j) + c))
    prev = lambda j: pl.BlockSpec((1, DIL_BAND, hw),
                                  lambda b, r, s, c: (b, jnp.maximum(s - 1, 0), r * wb + base(j) + c))
    out = pl.BlockSpec((1, DIL_BAND, hw), lambda b, r, s, c: (b, s, r * hsteps + c))
    o, lse = pl.pallas_call(
        _dil_attn_kernel,
        grid=(batch, d, nb, hsteps),
        in_specs=[own(0), prev(1), own(1), prev(2), own(2)],
        out_specs=[out, out],
        out_shape=[jax.ShapeDtypeStruct((batch, ls, d * DIL_OUT), F32)] * 2,
        compiler_params=_cparams("parallel", "parallel", "parallel", "parallel"),
    )(hv, hv, hv, hv, hv)
    return o.reshape(n, DIL_OUT), lse.reshape(n, DIL_OUT)


def _dil_combine_kernel(o0, o1, o2, l0, l1, l2, y_ref):
    a, b, c = l0[...], l1[...], l2[...]
    m = jnp.maximum(jnp.maximum(a, b), c)
    ea, eb, ec = jnp.exp(a - m), jnp.exp(b - m), jnp.exp(c - m)
    y_ref[...] = ((ea * o0[...] + eb * o1[...] + ec * o2[...]) / (ea + eb + ec)).astype(y_ref.dtype)


def _dilated_mixer_pallas(h_groups, *, batch):
    n = h_groups[0].shape[0]
    outs, lses = [], []
    for h, (window, dilation) in zip(h_groups, DIL_GROUPS):
        assert window // dilation == DIL_BAND
        o, lse = _dil_group_attention(h, dilation, batch=batch)
        outs.append(o)
        lses.append(lse)
    tt = min(512, n)
    blk = pl.BlockSpec((tt, 512), lambda i, c: (i, c))
    return pl.pallas_call(
        _dil_combine_kernel,
        grid=(n // tt, DIL_OUT // 512),
        in_specs=[blk] * 6,
        out_specs=blk,
        out_shape=jax.ShapeDtypeStruct((n, DIL_OUT), BF16),
        compiler_params=_cparams("parallel", "parallel"),
    )(*outs, *lses)


GDN_HEAD_GROUP = 4
GDN_CHUNK_GROUP = 8
GDN_PREP_COLS = 1024
SUBLANES = 8


def _gdn_prep_kernel(x_ref, halo_ref, w_ref, o_ref, *, tiles_per_seq):
    i = pl.program_id(0)
    c = pl.program_id(1)
    x = x_ref[...]
    tt = x.shape[0]
    halo = jnp.where(i % tiles_per_seq == 0, 0.0, halo_ref[...])
    row8 = lax.broadcasted_iota(I32, halo.shape, 0)
    w = w_ref[...]
    y = x * w[GDN_CONV_TAPS - 1:GDN_CONV_TAPS, :]
    for s in range(1, GDN_CONV_TAPS):
        xs = pltpu.roll(x, s, axis=0)
        first = jnp.where(row8 < s, pltpu.roll(halo, s, axis=0), xs[:SUBLANES])
        xs = jnp.concatenate([first, xs[SUBLANES:]], axis=0)
        y = y + xs * w[GDN_CONV_TAPS - 1 - s:GDN_CONV_TAPS - s, :]
    y = y * jax.nn.sigmoid(y)
    n_qk_tiles = 2 * GDN_KEY_DIM // GDN_PREP_COLS

    @pl.when(c >= n_qk_tiles)
    def _():
        o_ref[...] = y

    @pl.when(c < n_qk_tiles)
    def _():
        qscale = jnp.where(c < n_qk_tiles // 2, GDN_HEAD_DIM ** -0.5, 1.0)
        for g in range(GDN_PREP_COLS // GDN_HEAD_DIM):
            ys = y[:, g * GDN_HEAD_DIM:(g + 1) * GDN_HEAD_DIM]
            inv = lax.rsqrt(jnp.sum(ys * ys, axis=-1, keepdims=True) + RMS_EPS)
            o_ref[:, g * GDN_HEAD_DIM:(g + 1) * GDN_HEAD_DIM] = ys * inv * qscale


def _gdn_prep(h_main, conv_w, *, seq):
    n = h_main.shape[0]
    tt = min(256, seq)
    cw = GDN_PREP_COLS
    return pl.pallas_call(
        functools.partial(_gdn_prep_kernel, tiles_per_seq=seq // tt),
        grid=(n // tt, GDN_CONV_DIM // cw),
        in_specs=[pl.BlockSpec((tt, cw), lambda i, c: (i, c)),
                  pl.BlockSpec((SUBLANES, cw), lambda i, c: (jnp.maximum(i * (tt // SUBLANES) - 1, 0), c)),
                  pl.BlockSpec((GDN_CONV_TAPS, cw), lambda i, c: (0, c))],
        out_specs=pl.BlockSpec((tt, cw), lambda i, c: (i, c)),
        out_shape=jax.ShapeDtypeStruct((n, GDN_CONV_DIM), F32),
        compiler_params=_cparams("parallel", "parallel"),
    )(h_main, h_main, conv_w)


def _gdn_gate_kernel(ba_ref, alog_ref, dtb_ref, beta_ref, g_ref):
    ba = ba_ref[...]
    beta = jax.nn.sigmoid(ba)
    xa = ba + dtb_ref[...]
    softplus = jnp.maximum(xa, 0.0) + jnp.log(1.0 + jnp.exp(-jnp.abs(xa)))
    g = -jnp.exp(alog_ref[...]) * softplus
    hg = GDN_HEAD_GROUP
    for j in range(GDN_V_HEADS // hg):
        beta_ref[j] = beta[:, j * hg:(j + 1) * hg]
        g_ref[j] = g[:, GDN_V_HEADS + j * hg:GDN_V_HEADS + (j + 1) * hg]


def _gdn_gates(h_tail, a_log, dt_bias):
    n = h_tail.shape[0]
    tt = min(512, n)
    ng = GDN_V_HEADS // GDN_HEAD_GROUP
    lane_vec = lambda v: jnp.zeros((1, LANES), F32).at[0, GDN_V_HEADS:2 * GDN_V_HEADS].set(v)
    out = pl.BlockSpec((ng, tt, GDN_HEAD_GROUP), lambda i: (0, i, 0))
    return pl.pallas_call(
        _gdn_gate_kernel,
        grid=(n // tt,),
        in_specs=[pl.BlockSpec((tt, LANES), lambda i: (i, MEM_Q // LANES)),
                  pl.BlockSpec((1, LANES), lambda i: (0, 0)), pl.BlockSpec((1, LANES), lambda i: (0, 0))],
        out_specs=[out, out],
        out_shape=[jax.ShapeDtypeStruct((ng, n, GDN_HEAD_GROUP), F32)] * 2,
        compiler_params=_cparams("parallel"),
    )(h_tail, lane_vec(a_log), lane_vec(dt_bias))


def _dot_tril(ones_b, x):
    acc = None
    for _ in range(3):
        x_b = x.astype(BF16)
        d = jnp.dot(ones_b, x_b, preferred_element_type=F32)
        acc = d if acc is None else acc + d
        x = x - x_b.astype(F32)
    return acc


def _gdn_scan_kernel(q_ref, k_ref, v_ref, z_ref, beta_ref, g_ref, ng_ref, o_ref, state_ref):
    C = GDN_CHUNK
    hd = GDN_HEAD_DIM

    @pl.when(pl.program_id(2) == 0)
    def _():
        state_ref[...] = jnp.zeros_like(state_ref)

    r_i = lax.broadcasted_iota(I32, (C, C), 0)
    c_i = lax.broadcasted_iota(I32, (C, C), 1)
    causal = c_i <= r_i
    strict = c_i < r_i
    tril = causal.astype(F32)
    later = (r_i > c_i).astype(F32)
    eye = (r_i == c_i).astype(F32)
    contract_last = (((1,), (1,)), ((), ()))
    norm_g = ng_ref[...]

    tril_b = tril.astype(BF16)
    chunk_heads = [(c, j) for c in range(GDN_CHUNK_GROUP) for j in range(GDN_HEAD_GROUP)]
    rows = lambda c: slice(c * C, (c + 1) * C)
    hcols = lambda j: slice(j * hd, (j + 1) * hd)

    gc_all = [_dot_tril(tril_b, g_ref[0, rows(c), :]) for c in range(GDN_CHUNK_GROUP)]
    kk, qk = {}, {}
    for c in range(GDN_CHUNK_GROUP):
        for kh in range(GDN_HEAD_GROUP // 2):
            k_b = k_ref[rows(c), hcols(kh)].astype(BF16)
            q_b = q_ref[rows(c), hcols(kh)].astype(BF16)
            kk[c, kh] = lax.dot_general(k_b, k_b, contract_last, preferred_element_type=F32)
            qk[c, kh] = lax.dot_general(q_b, k_b, contract_last, preferred_element_type=F32)
    decay, mpow, tinv = {}, {}, {}
    for c, j in chunk_heads:
        diff = _dot_tril(tril_b, g_ref[0, rows(c), j:j + 1] * later)
        decay[c, j] = jnp.where(causal, jnp.exp(jnp.where(causal, diff, 0.0)), 0.0)
        lower = jnp.where(strict, beta_ref[0, rows(c), j:j + 1] * kk[c, j // 2] * decay[c, j], 0.0)
        mpow[c, j] = -lower
        tinv[c, j] = eye - lower
    for _ in range(5):
        for cj in chunk_heads:
            m_b = mpow[cj].astype(BF16)
            mpow[cj] = jnp.dot(m_b, m_b, preferred_element_type=F32)
        for cj in chunk_heads:
            tinv[cj] = tinv[cj] + jnp.dot(tinv[cj].astype(BF16), mpow[cj].astype(BF16), preferred_element_type=F32)
    u, w, a_intra, q_dec, k_dec_t, g_last = {}, {}, {}, {}, {}, {}
    for c, j in chunk_heads:
        k_h = k_ref[rows(c), hcols(j // 2)]
        beta = beta_ref[0, rows(c), j:j + 1]
        gc = gc_all[c][:, j:j + 1]
        egc = jnp.exp(gc)
        gc_last = gc[C - 1:C, :]
        tinv_b = tinv[c, j].astype(BF16)
        u[c, j] = jnp.dot(tinv_b, (v_ref[rows(c), hcols(j)] * beta).astype(BF16), preferred_element_type=F32)
        w[c, j] = jnp.dot(tinv_b, (k_h * (beta * egc)).astype(BF16), preferred_element_type=F32).astype(BF16)
        a_intra[c, j] = jnp.where(causal, qk[c, j // 2] * decay[c, j], 0.0).astype(BF16)
        q_dec[c, j] = (q_ref[rows(c), hcols(j // 2)] * egc).astype(BF16)
        k_dec_t[c, j] = (k_h * jnp.exp(gc_last - gc)).T.astype(BF16)
        g_last[c, j] = jnp.exp(gc_last)

    heads = range(GDN_HEAD_GROUP)
    state = [state_ref[j] for j in heads]
    for c in range(GDN_CHUNK_GROUP):
        state_b = [state[j].astype(BF16) for j in heads]
        v_new = [u[c, j] - jnp.dot(w[c, j], state_b[j], preferred_element_type=F32) for j in heads]
        v_new_b = [v.astype(BF16) for v in v_new]
        state = [state[j] * g_last[c, j] + jnp.dot(k_dec_t[c, j], v_new_b[j], preferred_element_type=F32)
                 for j in heads]
        for j in heads:
            o = jnp.dot(q_dec[c, j], state_b[j], preferred_element_type=F32) + jnp.dot(
                a_intra[c, j], v_new_b[j], preferred_element_type=F32)
            o = o * lax.rsqrt(jnp.mean(o * o, axis=-1, keepdims=True) + RMS_EPS) * norm_g
            z = z_ref[rows(c), hcols(j)]
            o_ref[rows(c), hcols(j)] = (o * (z * jax.nn.sigmoid(z))).astype(o_ref.dtype)
    for j in heads:
        state_ref[j] = state[j]


def _gdn_scan(qkv, h_main, beta, g, norm_g, *, batch):
    n = qkv.shape[0]
    seq = n // batch
    tr = GDN_CHUNK_GROUP * GDN_CHUNK
    nt = seq // tr
    hg = GDN_HEAD_GROUP
    kw = (hg // 2) * GDN_HEAD_DIM
    vw = hg * GDN_HEAD_DIM
    row = lambda b, h, t: b * nt + t
    return pl.pallas_call(
        _gdn_scan_kernel,
        grid=(batch, GDN_V_HEADS // hg, nt),
        in_specs=[pl.BlockSpec((tr, kw), lambda b, h, t: (row(b, h, t), h)),
                  pl.BlockSpec((tr, kw), lambda b, h, t: (row(b, h, t), GDN_KEY_DIM // kw + h)),
                  pl.BlockSpec((tr, vw), lambda b, h, t: (row(b, h, t), 2 * GDN_KEY_DIM // vw + h)),
                  pl.BlockSpec((tr, vw), lambda b, h, t: (row(b, h, t), GDN_CONV_DIM // vw + h)),
                  pl.BlockSpec((1, tr, hg), lambda b, h, t: (h, row(b, h, t), 0)),
                  pl.BlockSpec((1, tr, hg), lambda b, h, t: (h, row(b, h, t), 0)),
                  pl.BlockSpec((1, GDN_HEAD_DIM), lambda b, h, t: (0, 0))],
        out_specs=pl.BlockSpec((tr, vw), lambda b, h, t: (row(b, h, t), h)),
        out_shape=jax.ShapeDtypeStruct((n, GDN_VAL_DIM), BF16),
        scratch_shapes=[pltpu.VMEM((hg, GDN_HEAD_DIM, GDN_HEAD_DIM), F32)],
        compiler_params=_cparams("parallel", "parallel", "arbitrary"),
    )(qkv, qkv, qkv, h_main, beta, g, norm_g.reshape(1, GDN_HEAD_DIM))


def _gdn_in_weights(w_in):
    n_main = GDN_CONV_DIM + GDN_VAL_DIM
    n_ba = 2 * GDN_V_HEADS
    pad = jnp.zeros((w_in.shape[0], LANES - n_ba), w_in.dtype)
    tail = jnp.concatenate([w_in[:, n_main + n_ba:], w_in[:, n_main:n_main + n_ba], pad], axis=1)
    return w_in[:, :n_main], tail


def _gdn_mixer_pallas(h_main, h_tail, mix_params, *, batch):
    conv_w, a_log, dt_bias, norm_g = mix_params
    seq = h_main.shape[0] // batch
    qkv = _gdn_prep(h_main, conv_w, seq=seq)
    beta, g = _gdn_gates(h_tail, a_log, dt_bias)
    return _gdn_scan(qkv, h_main, beta, g, norm_g, batch=batch)


DSA_TQ = 128
DSA_TK = 512
DSA_HEAD_GROUP = 16
DSA_ATTN_TK = 1024
DSA_INDEX_TQ = 128
DSA_INDEX_HEAD_GROUP = 8
DSA_KEY_W = DSA_KV_LORA + LANES
MASK_BIAS = -2e30
M_INIT = -1e30
I32_MIN = -2 ** 31


def _rope_partner(x, half):
    ax = x.ndim - 1
    n = x.shape[ax]
    lane = lax.broadcasted_iota(I32, x.shape, ax) % (2 * half)
    return jnp.where(lane < half, -pltpu.roll(x, n - half, axis=ax), pltpu.roll(x, half, axis=ax))


def _dsa_prep_kernel(ql_ref, kv_ref, ik_ref, misc_ref, cos_ref, sin_ref, cosi_ref, sini_ref,
                     qg_ref, kvg_ref, ikg_ref, ikb_ref,
                     cq_ref, key_ref, iko_ref, iw_ref):
    half = DSA_ROPE_DIM // 2
    ql = ql_ref[...]
    cq_ref[...] = (ql * lax.rsqrt(jnp.mean(ql * ql, axis=-1, keepdims=True) + RMS_EPS) * qg_ref[...]).astype(BF16)
    kv = kv_ref[...]
    key_ref[:, :DSA_KV_LORA] = (
        kv * lax.rsqrt(jnp.mean(kv * kv, axis=-1, keepdims=True) + RMS_EPS) * kvg_ref[...]).astype(BF16)
    misc = misc_ref[...]
    lane = lax.broadcasted_iota(I32, misc.shape, 1)
    kr = jnp.where(lane < DSA_ROPE_DIM, misc * cos_ref[...] + _rope_partner(misc, half) * sin_ref[...], 0.0)
    key_ref[:, DSA_KV_LORA:] = kr.astype(BF16)
    iw = misc * (IDX_HEADS ** -0.5 * IDX_DIM ** -0.5)
    for h in range(IDX_HEADS):
        iw_ref[h] = iw[:, DSA_ROPE_DIM + h:DSA_ROPE_DIM + h + 1]
    ik = _ln_rows(ik_ref[...], ikg_ref[...], ikb_ref[...])
    iko_ref[...] = (ik * cosi_ref[...] + _rope_partner(ik, half) * sini_ref[...]).astype(BF16)


def _dsa_prep(h, tabs, q_norm_g, kv_norm_g, ik_g, ik_b, *, seq):
    n = h.shape[0]
    tq = min(256, seq)
    nt = seq // tq
    cos4, sin4, cosi, sini = tabs
    tab = pl.BlockSpec((tq, LANES), lambda i: (i % nt, 0))
    vec = lambda w: pl.BlockSpec((1, w), lambda i: (0, 0))
    blk = lambda w, j: pl.BlockSpec((tq, w), lambda i: (i, j))
    return pl.pallas_call(
        _dsa_prep_kernel,
        grid=(n // tq,),
        in_specs=[blk(DSA_Q_LORA, 0), blk(DSA_KV_LORA, 3), blk(LANES, 16), blk(LANES, 17), tab, tab, tab, tab,
                  vec(DSA_Q_LORA), vec(DSA_KV_LORA), vec(IDX_DIM), vec(IDX_DIM)],
        out_specs=[blk(DSA_Q_LORA, 0), blk(DSA_KEY_W, 0), blk(IDX_DIM, 0),
                   pl.BlockSpec((IDX_HEADS, tq, 1), lambda i: (0, i, 0))],
        out_shape=[jax.ShapeDtypeStruct((n, DSA_Q_LORA), BF16), jax.ShapeDtypeStruct((n, DSA_KEY_W), BF16),
                   jax.ShapeDtypeStruct((n, IDX_DIM), BF16), jax.ShapeDtypeStruct((IDX_HEADS, n, 1), F32)],
        compiler_params=_cparams("parallel"),
    )(h, h, h, h, cos4, sin4, cosi, sini, q_norm_g.reshape(1, -1), kv_norm_g.reshape(1, -1),
      ik_g.reshape(1, -1), ik_b.reshape(1, -1))


def _dsa_qpost_kernel(qn_ref, qri_ref, wuk_ref, cos_ref, sin_ref, cosi_ref, sini_ref, q_ref, iq_ref):
    half = DSA_ROPE_DIM // 2
    tq = qn_ref.shape[0]
    n_rope = DSA_HEADS * DSA_ROPE_DIM
    scale = (DSA_NOPE_DIM + DSA_ROPE_DIM) ** -0.5
    qr = qri_ref[:, :n_rope]
    cos = jnp.tile(cos_ref[...], (1, n_rope // LANES))
    sin = jnp.tile(sin_ref[...], (1, n_rope // LANES))
    qr = (qr * cos + _rope_partner(qr, half) * sin) * scale
    zeros = jnp.zeros((tq, LANES - DSA_ROPE_DIM), F32)
    cosi, sini = cosi_ref[...], sini_ref[...]
    for h in range(DSA_HEADS):
        q_ref[h, :, :DSA_KV_LORA] = (jnp.dot(
            qn_ref[:, h * DSA_NOPE_DIM:(h + 1) * DSA_NOPE_DIM], wuk_ref[h],
            preferred_element_type=F32) * scale).astype(BF16)
        q_ref[h, :, DSA_KV_LORA:] = jnp.concatenate(
            [qr[:, h * DSA_ROPE_DIM:(h + 1) * DSA_ROPE_DIM], zeros], axis=1).astype(BF16)
        iq = qri_ref[:, n_rope + h * IDX_DIM:n_rope + (h + 1) * IDX_DIM]
        iq_ref[h] = (iq * cosi + _rope_partner(iq, half) * sini).astype(BF16)


def _dsa_qpost(qn, qri, wuk_t, tabs, *, seq):
    n = qn.shape[0]
    tq = min(128, seq)
    nt = seq // tq
    cos4, sin4, cosi, sini = tabs
    tab = pl.BlockSpec((tq, LANES), lambda i: (i % nt, 0))
    row = lambda w: pl.BlockSpec((tq, w), lambda i: (i, 0))
    hm = lambda w: pl.BlockSpec((DSA_HEADS, tq, w), lambda i: (0, i, 0))
    return pl.pallas_call(
        _dsa_qpost_kernel,
        grid=(n // tq,),
        in_specs=[row(qn.shape[1]), row(qri.shape[1]),
                  pl.BlockSpec(wuk_t.shape, lambda i: (0, 0, 0)), tab, tab, tab, tab],
        out_specs=[hm(DSA_KEY_W), hm(IDX_DIM)],
        out_shape=[jax.ShapeDtypeStruct((DSA_HEADS, n, DSA_KEY_W), BF16),
                   jax.ShapeDtypeStruct((IDX_HEADS, n, IDX_DIM), BF16)],
        compiler_params=_cparams("parallel"),
    )(qn, qri, wuk_t, cos4, sin4, cosi, sini)


def _dsa_index_kernel(iq_ref, iw_ref, ik_ref, bias_ref, key_ref, *, topk):
    qi = pl.program_id(1)
    tq = iq_ref.shape[1]
    tk = DSA_TK
    hg = DSA_INDEX_HEAD_GROUP
    n_kt = ((qi + 1) * tq + tk - 1) // tk
    qpos = qi * tq + lax.broadcasted_iota(I32, (tq, tk), 0)
    lane = lax.broadcasted_iota(I32, (tq, tk), 1)

    def score_tile(kt, c):
        k0 = pl.multiple_of(kt * tk, tk)
        keys = ik_ref[pl.ds(k0, tk), :]
        acc = jnp.zeros((tq, tk), F32)
        for g in range(IDX_HEADS // hg):
            lhs = iq_ref[g * hg:(g + 1) * hg].reshape(hg * tq, IDX_DIM)
            s = lax.dot_general(lhs, keys, (((1,), (1,)), ((), ())), preferred_element_type=F32)
            s = jnp.maximum(s, 0.0).reshape(hg, tq, tk) * iw_ref[g * hg:(g + 1) * hg]
            acc = acc + jnp.sum(s, axis=0)
        acc = jnp.where(k0 + lane <= qpos, acc, -jnp.inf)
        bits = pltpu.bitcast(acc, I32)
        key_ref[:, pl.ds(k0, tk)] = jnp.where(bits < 0, bits ^ 0x7FFFFFFF, bits)
        return c

    lax.fori_loop(0, n_kt, score_tile, 0)

    res = jnp.zeros((tq, 1), I32)
    for bit in range(31, -1, -1):
        bit_c = jnp.int32(I32_MIN if bit == 31 else (1 << bit))
        cand = (res | bit_c) ^ jnp.int32(I32_MIN)

        def count_tile(kt, cnt, cand=cand):
            blk = key_ref[:, pl.ds(pl.multiple_of(kt * tk, tk), tk)]
            ge = (blk >= cand).astype(I32)
            part = ge[:, 0:LANES]
            for j in range(1, tk // LANES):
                part = part + ge[:, j * LANES:(j + 1) * LANES]
            return cnt + part

        cnt = lax.fori_loop(0, n_kt, count_tile, jnp.zeros((tq, LANES), I32))
        total = jnp.sum(cnt, axis=1, keepdims=True)
        res = jnp.where(total >= topk, res | bit_c, res)
    thr = res ^ jnp.int32(I32_MIN)

    bias_ref[...] = jnp.full(bias_ref.shape, MASK_BIAS, bias_ref.dtype)

    def bias_tile(kt, c):
        k0 = pl.multiple_of(kt * tk, tk)
        sel = (key_ref[:, pl.ds(k0, tk)] >= thr) & (k0 + lane <= qpos)
        bias_ref[:, pl.ds(k0, tk)] = jnp.where(sel, 0.0, MASK_BIAS).astype(bias_ref.dtype)
        return c

    lax.fori_loop(0, n_kt, bias_tile, 0)


def _dsa_index(iq, iw, ik, *, batch, topk):
    n = iq.shape[1]
    seq = n // batch
    tq = min(DSA_INDEX_TQ, seq)
    nq = seq // tq
    return pl.pallas_call(
        functools.partial(_dsa_index_kernel, topk=topk),
        grid=(batch, nq),
        in_specs=[pl.BlockSpec((IDX_HEADS, tq, IDX_DIM), lambda b, i: (0, b * nq + i, 0)),
                  pl.BlockSpec((IDX_HEADS, tq, 1), lambda b, i: (0, b * nq + i, 0)),
                  pl.BlockSpec((seq, IDX_DIM), lambda b, i: (b, 0))],
        out_specs=pl.BlockSpec((tq, seq), lambda b, i: (b * nq + i, 0)),
        out_shape=jax.ShapeDtypeStruct((n, seq), BF16),
        scratch_shapes=[pltpu.VMEM((tq, seq), I32)],
        compiler_params=_cparams("parallel", "parallel"),
    )(iq, iw, ik)


def _dsa_attn_kernel(q_ref, key_ref, bias_ref, wuv_ref, o_ref, m_ref, l_ref, acc_ref):
    qi = pl.program_id(1)
    kt = pl.program_id(2)
    tq = q_ref.shape[1]
    tk = key_ref.shape[0]
    hg = DSA_HEAD_GROUP
    rows = hg * tq
    last = ((qi + 1) * tq - 1) // tk

    @pl.when(kt == 0)
    def _():
        m_ref[...] = jnp.full(m_ref.shape, M_INIT, F32)
        l_ref[...] = jnp.zeros_like(l_ref)
        acc_ref[...] = jnp.zeros_like(acc_ref)

    @pl.when(kt <= last)
    def _():
        bias = bias_ref[...].astype(F32)
        keys = key_ref[...]
        vals = key_ref[:, :DSA_KV_LORA]
        for g in range(DSA_HEADS // hg):
            r = slice(g * rows, (g + 1) * rows)
            lhs = q_ref[g * hg:(g + 1) * hg].reshape(rows, DSA_KEY_W)
            s = lax.dot_general(lhs, keys, (((1,), (1,)), ((), ())), preferred_element_type=F32)
            s = (s.reshape(hg, tq, tk) + bias).reshape(rows, tk)
            m_prev = m_ref[r]
            m_cur = jnp.maximum(m_prev, jnp.max(s, axis=1, keepdims=True))
            alpha = jnp.exp(m_prev - m_cur)
            p = jnp.exp(s - m_cur[:, :1])
            l_ref[r] = alpha * l_ref[r] + jnp.sum(p, axis=1, keepdims=True)
            acc_ref[r] = acc_ref[r] * alpha[:, :1] + jnp.dot(p.astype(BF16), vals, preferred_element_type=F32)
            m_ref[r] = m_cur

    @pl.when(kt == pl.num_programs(2) - 1)
    def _():
        for h in range(DSA_HEADS):
            r = slice(h * tq, (h + 1) * tq)
            o_lat = (acc_ref[r] / l_ref[r][:, :1]).astype(BF16)
            o_ref[:, h * DSA_V_DIM:(h + 1) * DSA_V_DIM] = jnp.dot(
                o_lat, wuv_ref[h], preferred_element_type=F32).astype(o_ref.dtype)


def _dsa_attention(q, keys, bias, wuv, *, batch):
    n = keys.shape[0]
    seq = n // batch
    tq = min(DSA_TQ, seq)
    tk = min(DSA_ATTN_TK, seq)
    nq, nk = seq // tq, seq // tk
    last = lambda i: ((i + 1) * tq - 1) // tk
    return pl.pallas_call(
        _dsa_attn_kernel,
        grid=(batch, nq, nk),
        in_specs=[pl.BlockSpec((DSA_HEADS, tq, DSA_KEY_W), lambda b, i, k: (0, b * nq + i, 0)),
                  pl.BlockSpec((tk, DSA_KEY_W), lambda b, i, k: (b * nk + jnp.minimum(k, last(i)), 0)),
                  pl.BlockSpec((tq, tk), lambda b, i, k: (b * nq + i, jnp.minimum(k, last(i)))),
                  pl.BlockSpec(wuv.shape, lambda b, i, k: (0, 0, 0))],
        out_specs=pl.BlockSpec((tq, DSA_HEADS * DSA_V_DIM), lambda b, i, k: (b * nq + i, 0)),
        out_shape=jax.ShapeDtypeStruct((n, DSA_HEADS * DSA_V_DIM), BF16),
        scratch_shapes=[pltpu.VMEM((DSA_HEADS * tq, LANES), F32), pltpu.VMEM((DSA_HEADS * tq, LANES), F32),
                        pltpu.VMEM((DSA_HEADS * tq, DSA_KV_LORA), F32)],
        compiler_params=_cparams("parallel", "parallel", "arbitrary"),
    )(q, keys, bias, wuv)


def _dsa_rope_tables(seq):
    inv_freq = ROPE_THETA ** (-jnp.arange(0, DSA_ROPE_DIM, 2, dtype=F32) / DSA_ROPE_DIM)
    ang = jnp.arange(seq, dtype=F32)[:, None] * inv_freq[None, :]
    c, s = jnp.cos(ang), jnp.sin(ang)
    one, zero = jnp.ones_like(c), jnp.zeros_like(c)
    return (jnp.concatenate([c, c, c, c], axis=1), jnp.concatenate([s, s, s, s], axis=1),
            jnp.concatenate([c, c, one, one], axis=1), jnp.concatenate([s, s, zero, zero], axis=1))


def _dsa_in_weight(w_in):
    o1 = DSA_Q_LORA
    o2 = o1 + DSA_KV_LORA
    o3 = o2 + DSA_ROPE_DIM
    o4 = o3 + IDX_DIM
    o5 = o4 + IDX_HEADS
    pad = jnp.zeros((w_in.shape[0], LANES - DSA_ROPE_DIM - IDX_HEADS), w_in.dtype)
    return jnp.concatenate([w_in[:, :o1], w_in[:, o5:], w_in[:, o1:o2], w_in[:, o3:o4], w_in[:, o2:o3],
                            w_in[:, o4:o5], pad], axis=1)


def _dsa_mixer_pallas(h, mix_params, *, batch):
    q_norm_g, kv_norm_g, w_q_up, w_idx_q, ik_g, ik_b, w_kv_up = mix_params
    n = h.shape[0]
    seq = n // batch
    tabs = _dsa_rope_tables(seq)
    cq, keys, ik, iw = _dsa_prep(h, tabs, q_norm_g, kv_norm_g, ik_g, ik_b, seq=seq)
    wq = w_q_up.reshape(DSA_Q_LORA, DSA_HEADS, DSA_NOPE_DIM + DSA_ROPE_DIM).astype(BF16)
    w_nope = wq[:, :, :DSA_NOPE_DIM].reshape(DSA_Q_LORA, -1)
    w_ri = jnp.concatenate([wq[:, :, DSA_NOPE_DIM:].reshape(DSA_Q_LORA, -1), w_idx_q.astype(BF16)], axis=1)
    qn = _matmul([cq], [w_nope], tm=1024, tn=512, out_dtype=BF16)
    qri = _matmul([cq], [w_ri], tm=1024, tn=512, out_dtype=F32)
    w_kv = w_kv_up.reshape(DSA_KV_LORA, DSA_HEADS, DSA_NOPE_DIM + DSA_V_DIM).astype(BF16)
    wuk_t = jnp.transpose(w_kv[:, :, :DSA_NOPE_DIM], (1, 2, 0))
    wuv = jnp.transpose(w_kv[:, :, DSA_NOPE_DIM:], (1, 0, 2))
    q, iq = _dsa_qpost(qn, qri, wuk_t, tabs, seq=seq)
    bias = _dsa_index(iq, iw, ik, batch=batch, topk=min(DSA_TOPK_MAX, seq // 4))
    return _dsa_attention(q, keys, bias, wuv, batch=batch)


def _trunk(x, mem, mem_ln_g, mem_ln_b, w_mem_kv, layers):
    B, T, D = x.shape
    n = B * T
    nm = mem.shape[1]
    mem_n = _ln_plain(mem.reshape(B * nm, D), mem_ln_g, mem_ln_b)
    mem_kv = _matmul([mem_n], [w_mem_kv.astype(BF16)], tm=512, tn=512, out_dtype=BF16).reshape(B, nm, 2 * MEM_Q)
    x_f = x.reshape(n, D)
    x_b = x_f.astype(BF16)
    for i, (w_in, mix_params, w_out, ln1_g, ln1_b, ffn_params, ln2_g, ln2_b) in enumerate(layers):
        kind = i % N_MIXERS
        n_in = w_in.shape[1]
        n_mix = n_in - MEM_Q
        if kind == 2:
            w_in_b = _dsa_in_weight(w_in).astype(BF16)
            h = _matmul([x_b], [w_in_b], tm=1024, tn=_pick_tn(w_in_b.shape[1]), out_dtype=F32)
            y_mix = _dsa_mixer_pallas(h, mix_params, batch=B)
            y_mem = _mem_attention(h, mem_kv, batch=B, col_block=2)
        elif kind == 0:
            w_main, w_tail = _gdn_in_weights(w_in)
            h_main = _matmul([x_b], [w_main.astype(BF16)], tm=1024, tn=512, out_dtype=F32)
            h_tail = _matmul([x_b], [w_tail.astype(BF16)], tm=1024, tn=w_tail.shape[1], out_dtype=F32)
            y_mix = _gdn_mixer_pallas(h_main, h_tail, mix_params, batch=B)
            y_mem = _mem_attention(h_tail, mem_kv, batch=B)
        else:
            w_in_b = w_in.astype(BF16)
            gw = 3 * DIL_OUT
            h_groups = [_matmul([x_b], [w_in_b[:, g * gw:(g + 1) * gw]], tm=1024, tn=512, out_dtype=BF16)
                        for g in range(len(DIL_GROUPS))]
            q_mem = _matmul([x_b], [w_in_b[:, n_mix:]], tm=1024, tn=MEM_Q, out_dtype=BF16)
            y_mix = _dilated_mixer_pallas(h_groups, batch=B)
            y_mem = _mem_attention(q_mem, mem_kv, batch=B)
        w_out_b = w_out.astype(BF16)
        n_out = y_mix.shape[1]
        y = _matmul([y_mix, y_mem], [w_out_b[:n_out], w_out_b[n_out:]], tm=1024, tn=512, out_dtype=F32)
        x_f, x_lin, *routing = _ln_residual_route(x_f, y, ln1_g, ln1_b, ffn_params[0], ffn_params[1])
        x_f, x_b = _moe_block(x_f, x_lin, *ffn_params, ln2_g, ln2_b, routing=routing)
    return x_f.reshape(B, T, D)


def _pick_tn(n):
    for tn in (512, 384, 256, 128):
        if n % tn == 0:
            return tn
    return n


def kernel(x, mem, mem_ln_g, mem_ln_b, w_mem_kv, w_in_0, conv_w_0, a_log_0, dt_bias_0, gdn_norm_g_0, w_out_0, ln1_g_0, ln1_b_0, router_w_0, router_b_0, moe_w1_0, moe_b1_0, moe_w2_0, moe_b2_0, ln2_g_0, ln2_b_0, w_in_1, w_out_1, ln1_g_1, ln1_b_1, router_w_1, router_b_1, moe_w1_1, moe_b1_1, moe_w2_1, moe_b2_1, ln2_g_1, ln2_b_1, w_in_2, q_norm_g_2, kv_norm_g_2, w_q_up_2, w_idx_q_2, idx_k_norm_g_2, idx_k_norm_b_2, w_kv_up_2, w_out_2, ln1_g_2, ln1_b_2, router_w_2, router_b_2, moe_w1_2, moe_b1_2, moe_w2_2, moe_b2_2, ln2_g_2, ln2_b_2, w_in_3, conv_w_3, a_log_3, dt_bias_3, gdn_norm_g_3, w_out_3, ln1_g_3, ln1_b_3, router_w_3, router_b_3, moe_w1_3, moe_b1_3, moe_w2_3, moe_b2_3, ln2_g_3, ln2_b_3):
    layers = (
        (w_in_0, (conv_w_0, a_log_0, dt_bias_0, gdn_norm_g_0), w_out_0, ln1_g_0, ln1_b_0,
         (router_w_0, router_b_0, moe_w1_0, moe_b1_0, moe_w2_0, moe_b2_0), ln2_g_0, ln2_b_0),
        (w_in_1, (), w_out_1, ln1_g_1, ln1_b_1,
         (router_w_1, router_b_1, moe_w1_1, moe_b1_1, moe_w2_1, moe_b2_1), ln2_g_1, ln2_b_1),
        (w_in_2, (q_norm_g_2, kv_norm_g_2, w_q_up_2, w_idx_q_2, idx_k_norm_g_2, idx_k_norm_b_2, w_kv_up_2), w_out_2,
         ln1_g_2, ln1_b_2, (router_w_2, router_b_2, moe_w1_2, moe_b1_2, moe_w2_2, moe_b2_2), ln2_g_2, ln2_b_2),
        (w_in_3, (conv_w_3, a_log_3, dt_bias_3, gdn_norm_g_3), w_out_3, ln1_g_3, ln1_b_3,
         (router_w_3, router_b_3, moe_w1_3, moe_b1_3, moe_w2_3, moe_b2_3), ln2_g_3, ln2_b_3),
    )
    return _trunk(x, mem, mem_ln_g, mem_ln_b, w_mem_kv, layers)
```
